```python
import math
import jax, jax.numpy as jnp
from jax import lax
import numpy as np

D_MODEL = 1024
BATCH = 8
SEQ = 2048
DEPTH = 2
DEC_BATCH = 128
DEC_SEQ = 8
PAST_LEN = 16384
PAGE_SIZE = 128

MIX_WIDTH = D_MODEL
W_A = MIX_WIDTH // 4
W_B = MIX_WIDTH // 4
W_C = MIX_WIDTH // 4
W_D = MIX_WIDTH - W_A - W_B - W_C
CONV_W = 4
RG_HEADS = 4
RG_HD = W_A // RG_HEADS
RG_C = 8.0
SSD_HEAD_DIM = 64
SSD_HEADS = W_B // SSD_HEAD_DIM
SSD_GROUPS = 2
SSD_STATE = 64
SSD_CHUNK = 64
SSD_BC = SSD_GROUPS * SSD_STATE
SSD_CONV_CH = W_B + 2 * SSD_BC
GDN_HEAD_DIM = 64
GDN_HEADS = W_C // GDN_HEAD_DIM
GDN_CHUNK = 64
GDN_CONV_CH = 3 * W_C
S5_GROUP = 16
S5_GROUPS = W_D // S5_GROUP
S5_STATE = 64
ALPHA = (2.0 * DEPTH) ** 0.25
BETA_OUT = (8.0 * DEPTH) ** -0.25
IN_SPLITS = (W_A, W_A,
             SSD_CONV_CH, SSD_HEADS, W_B,
             GDN_CONV_CH, GDN_HEADS, GDN_HEADS, W_C,
             W_D, W_D)
IN_COLS = sum(IN_SPLITS)

kernel_name = "hybrid_rglru_ssd_gdn_s5_step"

F32 = jnp.float32


def split_cols(z, sizes):
    idx = np.cumsum(np.array(sizes))[:-1].tolist()
    return jnp.split(z, idx, axis=-1)


def layer_norm(x, g, b, eps=1e-5):
    xf = x.astype(F32)
    mu = jnp.mean(xf, -1, keepdims=True)
    xc = xf - mu
    var = jnp.mean(xc * xc, -1, keepdims=True)
    return (xc * lax.rsqrt(var + eps) * g.astype(F32) + b.astype(F32)).astype(x.dtype)


def rms_norm(x, g, eps=1e-6):
    xf = x.astype(F32)
    return xf * lax.rsqrt(jnp.mean(xf * xf, -1, keepdims=True) + eps) * g.astype(F32)


def l2norm(x, eps=1e-6):
    return x * lax.rsqrt(jnp.sum(x * x, -1, keepdims=True) + eps)


def causal_conv(u, buf, w, b):
    L = u.shape[1]
    full = jnp.concatenate([buf.astype(u.dtype), u], axis=1)
    ff = full.astype(F32)
    y = b.astype(F32) + sum(ff[:, j:j + L] * w[j].astype(F32) for j in range(CONV_W))
    return y, full[:, L:]


def linear_scan(a, b, h0):
    b = b.at[:, 0].add(a[:, 0] * h0)

    def comb(l, r):
        return r[0] * l[0], r[0] * l[1] + r[1]

    _, h = lax.associative_scan(comb, (a, b), axis=1)
    return h


def complex_linear_scan(a_re, a_im, b_re, b_im, h0_re, h0_im):
    b_re = b_re.at[:, 0].add(a_re[:, 0] * h0_re - a_im[:, 0] * h0_im)
    b_im = b_im.at[:, 0].add(a_re[:, 0] * h0_im + a_im[:, 0] * h0_re)

    def comb(l, r):
        lar, lai, lbr, lbi = l
        rar, rai, rbr, rbi = r
        return (rar * lar - rai * lai, rar * lai + rai * lar,
                rar * lbr - rai * lbi + rbr, rar * lbi + rai * lbr + rbi)

    _, _, h_re, h_im = lax.associative_scan(comb, (a_re, a_im, b_re, b_im), axis=1)
    return h_re, h_im


def rglru_mixer(xr, conv_buf, h0, p):
    xc, new_buf = causal_conv(xr, conv_buf, p['rg_conv_w'], p['rg_conv_b'])
    bsz, L, _ = xc.shape
    xh = xc.reshape(bsz, L, RG_HEADS, RG_HD)
    gate_r = jax.nn.sigmoid(jnp.einsum('blhi,hij->blhj', xh, p['rg_gate_a_w'].astype(F32)).reshape(bsz, L, W_A)
                            + p['rg_gate_a_b'].astype(F32))
    gate_i = jax.nn.sigmoid(jnp.einsum('blhi,hij->blhj', xh, p['rg_gate_x_w'].astype(F32)).reshape(bsz, L, W_A)
                            + p['rg_gate_x_b'].astype(F32))
    log_a = -RG_C * jax.nn.softplus(-p['rg_lambda'].astype(F32)) * gate_r
    a = jnp.exp(log_a)
    b = jnp.sqrt(-jnp.expm1(2.0 * log_a)) * (gate_i * xc)
    h = linear_scan(a, b, h0.astype(F32))
    return h, new_buf, h[:, -1]


def ssd_mixer(xbc, dt_raw, conv_buf, h0, p):
    xbc, new_buf = causal_conv(xbc, conv_buf, p['ssd_conv_w'], p['ssd_conv_b'])
    xbc = jax.nn.silu(xbc)
    bsz, L, _ = xbc.shape
    x, bm, cm = split_cols(xbc, (W_B, SSD_BC, SSD_BC))
    c = math.gcd(L, SSD_CHUNK)
    nc = L // c
    rep = SSD_HEADS // SSD_GROUPS
    x = x.reshape(bsz, nc, c, SSD_HEADS, SSD_HEAD_DIM)
    bm = jnp.repeat(bm.reshape(bsz, nc, c, SSD_GROUPS, SSD_STATE), rep, axis=3)
    cm = jnp.repeat(cm.reshape(bsz, nc, c, SSD_GROUPS, SSD_STATE), rep, axis=3)
    dt = jax.nn.softplus(dt_raw.astype(F32) + p['ssd_dt_bias'].astype(F32)).reshape(bsz, nc, c, SSD_HEADS)
    a = -jnp.exp(p['ssd_a_log'].astype(F32))
    acum = jnp.cumsum(dt * a, axis=2)
    incl = jnp.tril(jnp.ones((c, c), bool))
    seg = jnp.where(incl[:, :, None], acum[:, :, :, None, :] - acum[:, :, None, :, :], -jnp.inf)
    lmat = jnp.exp(seg)
    cb = jnp.einsum('bzthn,bzshn->bztsh', cm, bm)
    y = jnp.einsum('bztsh,bzshp->bzthp', cb * lmat * dt[:, :, None, :, :], x)
    chunk_state = jnp.einsum('bzsh,bzshn,bzshp->bzhpn', jnp.exp(acum[:, :, -1:] - acum) * dt, bm, x)
    chunk_decay = jnp.exp(acum[:, :, -1])

    def step(h, inp):
        cs, cd = inp
        return cd[..., None, None] * h + cs, h

    h_last, h_prev = lax.scan(step, h0.astype(F32),
                              (jnp.moveaxis(chunk_state, 1, 0), jnp.moveaxis(chunk_decay, 1, 0)))
    h_prev = jnp.moveaxis(h_prev, 0, 1)
    y = y + jnp.einsum('bzthn,bzhpn->bzthp', cm * jnp.exp(acum)[..., None], h_prev)
    y = y + p['ssd_d'].astype(F32)[:, None] * x
    return y.reshape(bsz, L, W_B), new_buf, h_last


def gdn_mixer(qkv, beta_raw, decay_raw, conv_buf, s0, p):
    qkv, new_buf = causal_conv(qkv, conv_buf, p['gdn_conv_w'], p['gdn_conv_b'])
    qkv = jax.nn.silu(qkv)
    bsz, L, _ = qkv.shape
    qkv = qkv.reshape(bsz, L, 3, GDN_HEADS, GDN_HEAD_DIM)
    q = l2norm(qkv[:, :, 0]) * (GDN_HEAD_DIM ** -0.5)
    k = l2norm(qkv[:, :, 1])
    v = qkv[:, :, 2]
    beta = jax.nn.sigmoid(beta_raw.astype(F32))
    g = -jnp.exp(p['gdn_a_log'].astype(F32)) * jax.nn.softplus(
        decay_raw.astype(F32) + p['gdn_dt_bias'].astype(F32))
    c = math.gcd(L, GDN_CHUNK)
    nc = L // c

    def chunk_heads(t):
        return jnp.swapaxes(t.reshape((bsz, nc, c) + t.shape[2:]), 2, 3)

    q, k, v, beta, g = (chunk_heads(t) for t in (q, k, v, beta, g))
    decay = jnp.cumsum(g, axis=-1)
    incl = jnp.tril(jnp.ones((c, c), bool))
    strict = jnp.tril(jnp.ones((c, c), bool), -1)
    gamma = jnp.exp(jnp.where(incl, decay[..., :, None] - decay[..., None, :], -jnp.inf))
    kk = jnp.einsum('bzhtd,bzhsd->bzhts', k, k)
    m = jnp.where(strict, beta[..., :, None] * kk * gamma, 0.0)
    eye = jnp.eye(c, dtype=F32)
    tmat = lax.linalg.triangular_solve(eye + m, jnp.broadcast_to(eye, m.shape),
                                       left_side=True, lower=True, unit_diagonal=True)
    value = tmat @ (beta[..., None] * v)
    kcum = tmat @ ((beta * jnp.exp(decay))[..., None] * k)
    qk = jnp.einsum('bzhtd,bzhsd->bzhts', q, k) * gamma
    qdec = q * jnp.exp(decay)[..., None]
    kdec = k * jnp.exp(decay[..., -1:] - decay)[..., None]
    chunk_decay = jnp.exp(decay[..., -1])

    def step(s, inp):
        value_i, kcum_i, qdec_i, qk_i, kdec_i, cd_i = inp
        w = value_i - jnp.einsum('bhtk,bhkv->bhtv', kcum_i, s)
        o = jnp.einsum('bhtk,bhkv->bhtv', qdec_i, s) + jnp.einsum('bhts,bhsv->bhtv', qk_i, w)
        s = cd_i[..., None, None] * s + jnp.einsum('bhsk,bhsv->bhkv', kdec_i, w)
        return s, o

    xs = tuple(jnp.moveaxis(t, 1, 0) for t in (value, kcum, qdec, qk, kdec, chunk_decay))
    s_last, o = lax.scan(step, s0.astype(F32), xs)
    o = jnp.swapaxes(jnp.moveaxis(o, 0, 1), 2, 3).reshape(bsz, L, GDN_HEADS, GDN_HEAD_DIM)
    o = rms_norm(o, p['gdn_norm_w']).reshape(bsz, L, W_C)
    return o, new_buf, s_last


def s5_mixer(u, h0_re, h0_im, p):
    bsz, L, _ = u.shape
    ug = u.astype(F32).reshape(bsz, L, S5_GROUPS, S5_GROUP)
    dt = jnp.exp(p['s5_log_dt'].astype(F32))[:, None]
    lr = p['s5_lambda_re'].astype(F32)
    li = p['s5_lambda_im'].astype(F32)
    mag = jnp.exp(lr * dt)
    ang = li * dt
    ab_re = mag * jnp.cos(ang)
    ab_im = mag * jnp.sin(ang)
    den = lr * lr + li * li
    f_re = ((ab_re - 1.0) * lr + ab_im * li) / den
    f_im = (ab_im * lr - (ab_re - 1.0) * li) / den
    b_re = p['s5_b_re'].astype(F32)
    b_im = p['s5_b_im'].astype(F32)
    bb_re = f_re[..., None] * b_re - f_im[..., None] * b_im
    bb_im = f_re[..., None] * b_im + f_im[..., None] * b_re
    bu_re = jnp.einsum('blgi,gni->blgn', ug, bb_re)
    bu_im = jnp.einsum('blgi,gni->blgn', ug, bb_im)
    shp = bu_re.shape
    h_re, h_im = complex_linear_scan(jnp.broadcast_to(ab_re, shp), jnp.broadcast_to(ab_im, shp),
                                     bu_re, bu_im, h0_re.astype(F32), h0_im.astype(F32))
    y = (jnp.einsum('blgn,gin->blgi', h_re, p['s5_c_re'].astype(F32))
         - jnp.einsum('blgn,gin->blgi', h_im, p['s5_c_im'].astype(F32))
         + p['s5_d'].astype(F32).reshape(S5_GROUPS, S5_GROUP) * ug)
    y = jax.nn.gelu(y.reshape(bsz, L, W_D), approximate=False)
    y = y * jax.nn.sigmoid(jnp.einsum('ble,ef->blf', y, p['s5_glu_w'].astype(F32)) + p['s5_glu_b'].astype(F32))
    return y, h_re[:, -1], h_im[:, -1]


def layer_forward(x, states, p):
    conv_a, h_a, conv_b, h_b, conv_c, s_c, s5_re, s5_im = states
    z = jnp.einsum('bld,de->ble', x, p['w_in'])
    (a_x, a_gate, b_xbc, b_dt, b_gate, c_qkv, c_beta, c_decay, c_gate,
     d_u, d_gate) = split_cols(z, IN_SPLITS)
    ya, conv_a, h_a = rglru_mixer(a_x, conv_a, h_a, p)
    ya = ya * jax.nn.silu(a_gate.astype(F32))
    yb, conv_b, h_b = ssd_mixer(b_xbc, b_dt, conv_b, h_b, p)
    yb = rms_norm(yb * jax.nn.silu(b_gate.astype(F32)), p['ssd_norm_w'])
    yc, conv_c, s_c = gdn_mixer(c_qkv, c_beta, c_decay, conv_c, s_c, p)
    yc = yc * jax.nn.silu(c_gate.astype(F32))
    yd, s5_re, s5_im = s5_mixer(d_u, s5_re, s5_im, p)
    yd = yd * jax.nn.silu(d_gate.astype(F32))
    mix = jnp.concatenate([ya, yb, yc, yd], axis=-1).astype(x.dtype)
    out = jnp.einsum('ble,ed->bld', mix, p['w_out'])
    x = layer_norm(ALPHA * x + out, p['ln_g'], p['ln_b'])
    return x, (conv_a, h_a, conv_b, h_b, conv_c, s_c, s5_re, s5_im)


def zero_states(bsz):
    return (jnp.zeros((bsz, CONV_W - 1, W_A), F32),
            jnp.zeros((bsz, W_A), F32),
            jnp.zeros((bsz, CONV_W - 1, SSD_CONV_CH), F32),
            jnp.zeros((bsz, SSD_HEADS, SSD_HEAD_DIM, SSD_STATE), F32),
            jnp.zeros((bsz, CONV_W - 1, GDN_CONV_CH), F32),
            jnp.zeros((bsz, GDN_HEADS, GDN_HEAD_DIM, GDN_HEAD_DIM), F32),
            jnp.zeros((bsz, S5_GROUPS, S5_STATE), F32),
            jnp.zeros((bsz, S5_GROUPS, S5_STATE), F32))


def setup_inputs(seed: int = 0) -> dict:
    key = jax.random.key(seed)
    ks = iter(jax.random.split(key, 64))

    def nrm(shape, s):
        return s * jax.random.normal(next(ks), shape, F32)

    def uni(shape, lo, hi):
        return jax.random.uniform(next(ks), shape, F32, lo, hi)

    def dt_bias(shape):
        dt = jnp.exp(uni(shape, math.log(1e-3), math.log(1e-1)))
        return dt + jnp.log(-jnp.expm1(-dt))

    rg_s = uni((DEPTH, W_A), 0.9, 0.999) ** (1.0 / RG_C)
    n_idx = jnp.arange(S5_STATE, dtype=F32)
    return {
        'x_prompt': nrm((BATCH, SEQ, D_MODEL), 1.0),
        'x_sample': nrm((DEC_BATCH, DEC_SEQ, D_MODEL), 1.0),
        'cache_rglru_conv': nrm((DEPTH, DEC_BATCH, CONV_W - 1, W_A), 1.0),
        'state_rglru': nrm((DEPTH, DEC_BATCH, W_A), 0.5),
        'cache_ssd_conv': nrm((DEPTH, DEC_BATCH, CONV_W - 1, SSD_CONV_CH), 1.0),
        'state_ssd': nrm((DEPTH, DEC_BATCH, SSD_HEADS, SSD_HEAD_DIM, SSD_STATE), 0.1),
        'cache_gdn_conv': nrm((DEPTH, DEC_BATCH, CONV_W - 1, GDN_CONV_CH), 1.0),
        'state_gdn': nrm((DEPTH, DEC_BATCH, GDN_HEADS, GDN_HEAD_DIM, GDN_HEAD_DIM), 0.1),
        'state_s5_re': nrm((DEPTH, DEC_BATCH, S5_GROUPS, S5_STATE), 0.1),
        'state_s5_im': nrm((DEPTH, DEC_BATCH, S5_GROUPS, S5_STATE), 0.1),
        'w_in': nrm((DEPTH, D_MODEL, IN_COLS), D_MODEL ** -0.5),
        'w_out': nrm((DEPTH, MIX_WIDTH, D_MODEL), BETA_OUT * MIX_WIDTH ** -0.5),
        'ln_g': 1.0 + nrm((DEPTH, D_MODEL), 0.01),
        'ln_b': nrm((DEPTH, D_MODEL), 0.01),
        'rg_conv_w': nrm((DEPTH, CONV_W, W_A), CONV_W ** -0.5),
        'rg_conv_b': nrm((DEPTH, W_A), 0.01),
        'rg_gate_a_w': nrm((DEPTH, RG_HEADS, RG_HD, RG_HD), RG_HD ** -0.5),
        'rg_gate_a_b': nrm((DEPTH, W_A), 0.01),
        'rg_gate_x_w': nrm((DEPTH, RG_HEADS, RG_HD, RG_HD), RG_HD ** -0.5),
        'rg_gate_x_b': nrm((DEPTH, W_A), 0.01),
        'rg_lambda': jnp.log(rg_s) - jnp.log1p(-rg_s),
        'ssd_conv_w': nrm((DEPTH, CONV_W, SSD_CONV_CH), CONV_W ** -0.5),
        'ssd_conv_b': nrm((DEPTH, SSD_CONV_CH), 0.01),
        'ssd_dt_bias': dt_bias((DEPTH, SSD_HEADS)),
        'ssd_a_log': jnp.log(uni((DEPTH, SSD_HEADS), 1.0, 16.0)),
        'ssd_d': 1.0 + nrm((DEPTH, SSD_HEADS), 0.01),
        'ssd_norm_w': 1.0 + nrm((DEPTH, W_B), 0.01),
        'gdn_conv_w': nrm((DEPTH, CONV_W, GDN_CONV_CH), CONV_W ** -0.5),
        'gdn_conv_b': nrm((DEPTH, GDN_CONV_CH), 0.01),
        'gdn_dt_bias': dt_bias((DEPTH, GDN_HEADS)),
        'gdn_a_log': jnp.log(uni((DEPTH, GDN_HEADS), 1.0, 16.0)),
        'gdn_norm_w': 1.0 + nrm((DEPTH, GDN_HEAD_DIM), 0.01),
        's5_lambda_re': -0.5 + nrm((DEPTH, S5_GROUPS, S5_STATE), 0.01),
        's5_lambda_im': math.pi * n_idx + nrm((DEPTH, S5_GROUPS, S5_STATE), 0.01),
        's5_log_dt': uni((DEPTH, S5_GROUPS), math.log(1e-3), math.log(1e-1)),
        's5_b_re': nrm((DEPTH, S5_GROUPS, S5_STATE, S5_GROUP), (2.0 * S5_GROUP) ** -0.5),
        's5_b_im': nrm((DEPTH, S5_GROUPS, S5_STATE, S5_GROUP), (2.0 * S5_GROUP) ** -0.5),
        's5_c_re': nrm((DEPTH, S5_GROUPS, S5_GROUP, S5_STATE), S5_STATE ** -0.5),
        's5_c_im': nrm((DEPTH, S5_GROUPS, S5_GROUP, S5_STATE), S5_STATE ** -0.5),
        's5_d': nrm((DEPTH, W_D), 1.0),
        's5_glu_w': nrm((DEPTH, W_D, W_D), W_D ** -0.5),
        's5_glu_b': nrm((DEPTH, W_D), 0.01),
    }


def reference(x_prompt, x_sample, cache_rglru_conv, state_rglru, cache_ssd_conv, state_ssd,
              cache_gdn_conv, state_gdn, state_s5_re, state_s5_im,
              w_in, w_out, ln_g, ln_b,
              rg_conv_w, rg_conv_b, rg_gate_a_w, rg_gate_a_b, rg_gate_x_w, rg_gate_x_b, rg_lambda,
              ssd_conv_w, ssd_conv_b, ssd_dt_bias, ssd_a_log, ssd_d, ssd_norm_w,
              gdn_conv_w, gdn_conv_b, gdn_dt_bias, gdn_a_log, gdn_norm_w,
              s5_lambda_re, s5_lambda_im, s5_log_dt, s5_b_re, s5_b_im, s5_c_re, s5_c_im,
              s5_d, s5_glu_w, s5_glu_b):
    def layer_params(l):
        return dict(w_in=w_in[l], w_out=w_out[l], ln_g=ln_g[l], ln_b=ln_b[l],
                    rg_conv_w=rg_conv_w[l], rg_conv_b=rg_conv_b[l],
                    rg_gate_a_w=rg_gate_a_w[l], rg_gate_a_b=rg_gate_a_b[l],
                    rg_gate_x_w=rg_gate_x_w[l], rg_gate_x_b=rg_gate_x_b[l], rg_lambda=rg_lambda[l],
                    ssd_conv_w=ssd_conv_w[l], ssd_conv_b=ssd_conv_b[l], ssd_dt_bias=ssd_dt_bias[l],
                    ssd_a_log=ssd_a_log[l], ssd_d=ssd_d[l], ssd_norm_w=ssd_norm_w[l],
                    gdn_conv_w=gdn_conv_w[l], gdn_conv_b=gdn_conv_b[l], gdn_dt_bias=gdn_dt_bias[l],
                    gdn_a_log=gdn_a_log[l], gdn_norm_w=gdn_norm_w[l],
                    s5_lambda_re=s5_lambda_re[l], s5_lambda_im=s5_lambda_im[l], s5_log_dt=s5_log_dt[l],
                    s5_b_re=s5_b_re[l], s5_b_im=s5_b_im[l], s5_c_re=s5_c_re[l], s5_c_im=s5_c_im[l],
                    s5_d=s5_d[l], s5_glu_w=s5_glu_w[l], s5_glu_b=s5_glu_b[l])

    sample_states = (cache_rglru_conv, state_rglru, cache_ssd_conv, state_ssd,
                     cache_gdn_conv, state_gdn, state_s5_re, state_s5_im)
    prompt_init = zero_states(x_prompt.shape[0])
    yp = x_prompt
    ys = x_sample
    p_new = []
    s_new = []
    for l in range(DEPTH):
        prm = layer_params(l)
        yp, st_p = layer_forward(yp, prompt_init, prm)
        ys, st_s = layer_forward(ys, tuple(s[l] for s in sample_states), prm)
        p_new.append(st_p)
        s_new.append(st_s)

    (p_rg_conv, p_rg, p_ssd_conv, p_ssd, p_gdn_conv, p_gdn, p_s5_re, p_s5_im) = [
        jnp.stack([st[i] for st in p_new]) for i in range(8)]
    (s_rg_conv, s_rg, s_ssd_conv, s_ssd, s_gdn_conv, s_gdn, s_s5_re, s_s5_im) = [
        jnp.stack([st[i] for st in s_new]) for i in range(8)]
    return (yp, ys,
            p_rg_conv, p_rg, p_ssd_conv, p_ssd, p_gdn_conv, p_gdn, p_s5_re, p_s5_im,
            s_rg_conv, s_rg, s_ssd_conv, s_ssd, s_gdn_conv, s_gdn, s_s5_re, s_s5_im)
```

```python
import functools
import math

import numpy as np
import jax
import jax.numpy as jnp
from jax import lax
from jax.experimental import pallas as pl
from jax.experimental.pallas import tpu as pltpu

F32 = jnp.float32
BF16 = jnp.bfloat16

D_MODEL = 1024
DEPTH = 2
BW = 256
HD = 64
NH = BW // HD
SSD_GROUPS = 2
CONV_W = 4
HIST = 8
S5_GROUPS = 16
S5_GROUP = 16
S5_STATE = 64
S5N = S5_GROUPS * S5_STATE
RG_HEADS = 4
RG_C = 8.0
ALPHA = (2.0 * DEPTH) ** 0.25
GDN_CHUNK = 64
ROWS = 256
VMEM_LIMIT_BYTES = 60 * 1024 * 1024

NCONV = 7 * BW
C_AX, C_BX, C_BB, C_BC, C_Q, C_K, C_V = (k * BW for k in range(7))
NREST = 8 * BW
R_GATE, R_DU, R_DT, R_BETA, R_DECAY = 0, 4 * BW, 5 * BW, 6 * BW, 7 * BW

(V_RG_LAMBDA, V_SSD_DT_BIAS, V_SSD_A_LOG, V_SSD_D, V_SSD_NORM, V_GDN_DT_BIAS, V_GDN_A_LOG, V_GDN_NORM,
 V_S5_D, V_GLU_B) = range(10)
V_LN_G, V_LN_B, V_S5_LRE, V_S5_LIM, V_S5_LOGDT = range(5)


def _dot(a, b):
    return jnp.dot(a, b, preferred_element_type=F32)


def _dot_nt(a, b):
    return lax.dot_general(a, b, (((1,), (1,)), ((), ())), preferred_element_type=F32)


def _sigmoid(x):
    return jax.nn.sigmoid(x)


def _silu(x):
    return x * jax.nn.sigmoid(x)


def _softplus(x):
    return jnp.maximum(x, 0.0) + jnp.log1p(jnp.exp(-jnp.abs(x)))


def _split_bf16(x, pieces):
    out = []
    r = x
    for k in range(pieces):
        p = r.astype(BF16)
        out.append(p)
        if k + 1 < pieces:
            r = r - p.astype(F32)
    return out


def _exact_left(mask_bf16, x):
    acc = None
    for p in _split_bf16(x, 3):
        t = _dot(mask_bf16, p)
        acc = t if acc is None else acc + t
    return acc


def _iota(shape, dim):
    return lax.broadcasted_iota(jnp.int32, shape, dim)


def _log2(n):
    k = int(round(math.log2(n)))
    assert 1 << k == n
    return k


def _bd_mask(t):
    return (_iota((NH * t, BW), 0) >> _log2(t)) == (_iota((NH * t, BW), 1) >> _log2(HD))


def _bd(x, mask):
    return jnp.where(mask, jnp.tile(x, (NH, 1)), 0.0).astype(BF16)


def _diag_blocks(full, mask256):
    fm = jnp.where(mask256, full, 0.0)
    return fm[0:HD] + fm[HD:2 * HD] + fm[2 * HD:3 * HD] + fm[3 * HD:4 * HD]


def _seg_diff(c, bdm):
    t = c.shape[0]
    j = _iota((t, BW), 1) & (HD - 1)
    p1, p2, p3 = (p.astype(F32) for p in _split_bf16(c, 3))
    zero = jnp.zeros_like(c)
    lhs = jnp.where(j == 0, p1, jnp.where(j == 1, p2, jnp.where(j == 2, p3, jnp.where(j < 6, 1.0, zero))))
    rhs = jnp.where(j < 3, 1.0, jnp.where(j == 3, -p1, jnp.where(j == 4, -p2, jnp.where(j == 5, -p3, zero))))
    return _dot_nt(lhs.astype(BF16), _bd(rhs, bdm))


def _layer_kernel(nseq, lc,
                  x_ref, conv_in, rg_in, ssd_in, gdn_in, s5_in,
                  w_conv, w_rest, w_out, v1024, v256, cwb, wg, bg, glu_w, bre, bim, cre, cim,
                  lseq, mseq, lgdn, mgdn,
                  y_ref, conv_out, rg_out, ssd_out, gdn_out, s5_out,
                  zext, zr, mix, wb_s, wc_s, tab):
    rows = nseq * lc
    nblk = rows // 8
    blk_per_seq = lc // 8
    first_call_step = (pl.program_id(0) == 0) & (pl.program_id(1) == 0)

    @pl.when(first_call_step)
    def _():
        lr = v1024[V_S5_LRE:V_S5_LRE + 1, :]
        li = v1024[V_S5_LIM:V_S5_LIM + 1, :]
        dt = jnp.exp(v1024[V_S5_LOGDT:V_S5_LOGDT + 1, :])
        mag = jnp.exp(lr * dt)
        ang = li * dt
        ar = mag * jnp.cos(ang)
        ai = mag * jnp.sin(ang)
        den = lr * lr + li * li
        fr = ((ar - 1.0) * lr + ai * li) / den
        fi = (ai * lr - (ar - 1.0) * li) / den
        wb_s[:, 0:S5N] = (fr * bre[...] - fi * bim[...]).astype(BF16)
        wb_s[:, S5N:2 * S5N] = (fr * bim[...] + fi * bre[...]).astype(BF16)
        wc_s[0:S5N, :] = cre[...].astype(BF16)
        wc_s[S5N:2 * S5N, :] = (-cim[...]).astype(BF16)

        def cmul(p, q):
            return p[0] * q[0] - p[1] * q[1], p[0] * q[1] + p[1] * q[0]

        pw = [(ar, ai)]
        for _k in range(7):
            pw.append(cmul(pw[-1], pw[0]))
        row = _iota((8, S5N), 0)
        for idx, k in enumerate((1, 2, 4)):
            tab[2 * idx] = jnp.where(row >= k, pw[k - 1][0], 0.0)
            tab[2 * idx + 1] = jnp.where(row >= k, pw[k - 1][1], 0.0)
        pr = jnp.broadcast_to(pw[7][0], (8, S5N))
        pi = jnp.broadcast_to(pw[7][1], (8, S5N))
        for r in range(7):
            pr = jnp.where(row == r, pw[r][0], pr)
            pi = jnp.where(row == r, pw[r][1], pi)
        tab[6] = pr
        tab[7] = pi

    @pl.when(pl.program_id(1) == 0)
    def _():
        conv_out[...] = conv_in[...]
        rg_out[...] = rg_in[...]
        ssd_out[...] = ssd_in[...]
        gdn_out[...] = gdn_in[...]
        s5_out[...] = s5_in[...]

    def vec(r):
        return v256[r:r + 1, :]

    xb = x_ref[...].reshape(rows, D_MODEL).astype(BF16)
    zext[:, 0:HIST, :] = conv_out[...]
    zext[:, HIST:HIST + lc, :] = _dot(xb, w_conv[...]).reshape(nseq, lc, NCONV)
    zr[...] = _dot(xb, w_rest[...])
    conv_out[...] = zext[:, lc:lc + HIST, :]

    def conv(c0, width):
        acc = cwb[CONV_W:CONV_W + 1, c0:c0 + width]
        for tap in range(CONV_W):
            off = HIST - (CONV_W - 1) + tap
            acc = acc + cwb[tap:tap + 1, c0:c0 + width] * zext[:, off:off + lc, c0:c0 + width]
        return acc.reshape(rows, width)

    def rest(c0, width=BW):
        return zr[:, c0:c0 + width]

    bdm_rows = _bd_mask(rows)
    bdm_hd = _bd_mask(HD)

    xc = conv(C_AX, BW)
    gts = _dot(xc.astype(BF16), wg[...]) + bg[...]
    gate_r = _sigmoid(gts[:, 0:BW])
    gate_i = _sigmoid(gts[:, BW:2 * BW])
    log_a = (-RG_C * _softplus(-vec(V_RG_LAMBDA))) * gate_r
    a = jnp.exp(log_a)
    bv = jnp.sqrt(-jnp.tanh(log_a) * (a * a + 1.0)) * (gate_i * xc)
    a3 = a.reshape(nblk, 8, BW)
    b3 = bv.reshape(nblk, 8, BW)
    row3 = _iota((nblk, 8, BW), 1)
    for k in (1, 2, 4):
        sa = jnp.where(row3 >= k, pltpu.roll(a3, k, 1), 1.0)
        sb = jnp.where(row3 >= k, pltpu.roll(b3, k, 1), 0.0)
        b3 = b3 + a3 * sb
        a3 = a3 * sa
    hs = []
    carry = None
    for blk in range(nblk):
        q = blk // blk_per_seq
        if blk % blk_per_seq == 0:
            carry = rg_out[q]
        hb = b3[blk] + a3[blk] * carry
        carry = hb[7:8, :]
        if (blk + 1) % blk_per_seq == 0:
            rg_out[q] = carry
        hs.append(hb)
    h_rg = jnp.concatenate(hs, axis=0)
    mix[:, 0:BW] = h_rg * _silu(rest(R_GATE))

    xs = _silu(conv(C_BX, BW))
    bm = _silu(conv(C_BB, BW))
    cm = _silu(conv(C_BC, BW))
    dt = _softplus(rest(R_DT) + vec(V_SSD_DT_BIAS))
    da = dt * (-jnp.exp(vec(V_SSD_A_LOG)))
    acum = _exact_left(lseq[...], da)
    atot = _exact_left(mseq[...], da)
    t_i = _iota((rows, NH * rows), 0)
    s_i = _iota((rows, NH * rows), 1) & (rows - 1)
    valid = (s_i <= t_i) & ((s_i >> _log2(lc)) == (t_i >> _log2(lc)))
    lmat = jnp.where(valid, jnp.exp(jnp.minimum(_seg_diff(acum, bdm_rows), 0.0)), 0.0)
    cb = _dot_nt(cm.astype(BF16), _bd(bm, bdm_rows))
    xdt = xs * dt
    y_ssd = _dot((cb * lmat).astype(BF16), _bd(xdt, bdm_rows)) + vec(V_SSD_D) * xs
    cme = (cm * jnp.exp(acum)).astype(BF16)
    wbt = (bm * jnp.exp(atot - acum)).T.astype(BF16)
    y_inter = []
    for q in range(nseq):
        r0 = q * lc
        ht = ssd_out[q]
        y_inter.append(_dot(cme[r0:r0 + lc], _bd(ht, bdm_hd)))
        if nseq == 1:
            xq = xdt.astype(BF16)
        else:
            in_q = (_iota((rows, BW), 0) >> _log2(lc)) == q
            xq = jnp.where(in_q, xdt, 0.0).astype(BF16)
        upd = _diag_blocks(_dot(wbt, xq), bdm_hd)
        ssd_out[q] = jnp.exp(atot[r0:r0 + 1, :]) * ht + upd
    y_ssd = y_ssd + (y_inter[0] if nseq == 1 else jnp.concatenate(y_inter, axis=0))
    yb = y_ssd * _silu(rest(R_GATE + BW))
    yb = yb * lax.rsqrt(jnp.mean(yb * yb, axis=-1, keepdims=True) + 1e-6) * vec(V_SSD_NORM)
    mix[:, BW:2 * BW] = yb

    ones_bd = bdm_hd.astype(BF16)

    def head_sum(v):
        p1, p2 = _split_bf16(v, 2)
        return _dot(p1, ones_bd) + _dot(p2, ones_bd)

    qc = _silu(conv(C_Q, BW))
    kc = _silu(conv(C_K, BW))
    vc = _silu(conv(C_V, BW))
    qn = qc * lax.rsqrt(head_sum(qc * qc) + 1e-6) * (HD ** -0.5)
    kn = kc * lax.rsqrt(head_sum(kc * kc) + 1e-6)
    beta = _sigmoid(rest(R_BETA))
    g = (-jnp.exp(vec(V_GDN_A_LOG))) * _softplus(rest(R_DECAY) + vec(V_GDN_DT_BIAS))
    decay = _exact_left(lgdn[...], g)
    dtot = _exact_left(mgdn[...], g)
    edec = jnp.exp(decay)
    qdec = qn * edec
    kdec = kn * jnp.exp(dtot - decay)
    kbe = kn * beta * edec
    vb = vc * beta

    unit = min(lc, GDN_CHUNK)
    units = GDN_CHUNK // unit
    t_c = _iota((GDN_CHUNK, BW), 0)
    s_c = _iota((GDN_CHUNK, BW), 1) & (HD - 1)
    same_unit = (t_c >> _log2(unit)) == (s_c >> _log2(unit))
    valid_incl = same_unit & (s_c <= t_c)
    valid_strict = same_unit & (s_c < t_c)
    eye_cat = jnp.where(s_c == t_c, 1.0, 0.0)
    o_chunks = []
    for c in range(rows // GDN_CHUNK):
        r0 = c * GDN_CHUNK
        sl = slice(r0, r0 + GDN_CHUNK)
        kn_c = kn[sl]
        qk_kk = _dot_nt(jnp.concatenate([qn[sl], kn_c], axis=0).astype(BF16), _bd(kn_c, bdm_hd))
        eg = jnp.where(valid_incl, jnp.exp(jnp.minimum(_seg_diff(decay[sl], bdm_hd), 0.0)), 0.0)
        qkg = qk_kk[0:GDN_CHUNK] * eg
        m = jnp.where(valid_strict, beta[sl] * qk_kk[GDN_CHUNK:2 * GDN_CHUNK] * eg, 0.0)
        rm = eye_cat - m
        p = _dot(m.astype(BF16), _bd(m, bdm_hd))
        for _step in range(_log2(unit) - 2):
            pr2 = _dot(jnp.concatenate([p, rm], axis=0).astype(BF16), _bd(p, bdm_hd))
            p = pr2[0:GDN_CHUNK]
            rm = rm + pr2[GDN_CHUNK:2 * GDN_CHUNK]
        rm = rm + _dot(rm.astype(BF16), _bd(p, bdm_hd))
        tm = rm.astype(BF16)
        value = _dot(tm, _bd(vb[sl], bdm_hd))
        kcum = _dot(tm, _bd(kbe[sl], bdm_hd))
        qdec_c = qdec[sl]
        kdec_t = kdec[sl].T.astype(BF16)
        dtot_c = dtot[sl]
        w_parts, o_parts = [], []
        states = []
        for u in range(units):
            q = (r0 + u * unit) // lc
            u0 = u * unit
            s_q = gdn_out[q]
            states.append(s_q)
            kq = _dot(jnp.concatenate([kcum[u0:u0 + unit], qdec_c[u0:u0 + unit]], axis=0).astype(BF16),
                      _bd(s_q, bdm_hd))
            w_parts.append(value[u0:u0 + unit] - kq[0:unit])
            o_parts.append(kq[unit:2 * unit])
        w = w_parts[0] if units == 1 else jnp.concatenate(w_parts, axis=0)
        o_state = o_parts[0] if units == 1 else jnp.concatenate(o_parts, axis=0)
        o_chunks.append(o_state + _dot(qkg.astype(BF16), _bd(w, bdm_hd)))
        for u in range(units):
            q = (r0 + u * unit) // lc
            u0 = u * unit
            if units == 1:
                wq = w.astype(BF16)
            else:
                wq = jnp.where((t_c >> _log2(unit)) == u, w, 0.0).astype(BF16)
            upd = _diag_blocks(_dot(kdec_t, wq), bdm_hd)
            gdn_out[q] = jnp.exp(dtot_c[u0:u0 + 1, :]) * states[u] + upd
    o = o_chunks[0] if len(o_chunks) == 1 else jnp.concatenate(o_chunks, axis=0)
    o = o * lax.rsqrt(head_sum(o * o) * (1.0 / HD) + 1e-6) * vec(V_GDN_NORM)
    mix[:, 2 * BW:3 * BW] = o * _silu(rest(R_GATE + 2 * BW))

    du = rest(R_DU)
    bu = _dot(du.astype(BF16), wb_s[...])
    b_re = bu[:, 0:S5N].reshape(nblk, 8, S5N)
    b_im = bu[:, S5N:2 * S5N].reshape(nblk, 8, S5N)
    for idx, k in enumerate((1, 2, 4)):
        tr = tab[2 * idx][None]
        ti = tab[2 * idx + 1][None]
        sr = pltpu.roll(b_re, k, 1)
        si = pltpu.roll(b_im, k, 1)
        b_re, b_im = b_re + tr * sr - ti * si, b_im + tr * si + ti * sr
    pw_re = tab[6]
    pw_im = tab[7]
    h_re, h_im = [], []
    cr = ci = None
    for blk in range(nblk):
        q = blk // blk_per_seq
        if blk % blk_per_seq == 0:
            cr = s5_out[q, :, 0:S5N]
            ci = s5_out[q, :, S5N:2 * S5N]
        hr = b_re[blk] + pw_re * cr - pw_im * ci
        hi = b_im[blk] + pw_re * ci + pw_im * cr
        cr = hr[7:8, :]
        ci = hi[7:8, :]
        if (blk + 1) % blk_per_seq == 0:
            s5_out[q, :, 0:S5N] = cr
            s5_out[q, :, S5N:2 * S5N] = ci
        h_re.append(hr)
        h_im.append(hi)
    h_re = jnp.concatenate(h_re, axis=0).astype(BF16)
    h_im = jnp.concatenate(h_im, axis=0).astype(BF16)
    y5 = _dot(h_re, wc_s[0:S5N, :]) + _dot(h_im, wc_s[S5N:2 * S5N, :]) + vec(V_S5_D) * du
    y5 = 0.5 * y5 * (1.0 + lax.erf(y5 * math.sqrt(0.5)))
    y5 = y5 * _sigmoid(_dot(y5.astype(BF16), glu_w[...]) + vec(V_GLU_B))
    mix[:, 3 * BW:4 * BW] = y5 * _silu(rest(R_GATE + 3 * BW))

    res = ALPHA * x_ref[...].reshape(rows, D_MODEL) + _dot(mix[...].astype(BF16), w_out[...])
    mu = jnp.mean(res, axis=-1, keepdims=True)
    rc = res - mu
    var = jnp.mean(rc * rc, axis=-1, keepdims=True)
    y = rc * lax.rsqrt(var + 1e-5) * v1024[V_LN_G:V_LN_G + 1, :] + v1024[V_LN_B:V_LN_B + 1, :]
    y_ref[...] = y.reshape(y_ref.shape)


def _seq_masks(rows, unit):
    t = np.arange(rows)
    same = (t[:, None] // unit) == (t[None, :] // unit)
    incl = same & (t[None, :] <= t[:, None])
    return jnp.asarray(incl, BF16), jnp.asarray(same, BF16)


def _head_rep(v):
    return jnp.repeat(v, HD)


_SSD_REP_IDX = np.concatenate(
    [np.arange(BW)]
    + [BW + (h // (NH // SSD_GROUPS)) * HD + np.arange(HD) for h in range(NH)]
    + [BW + SSD_GROUPS * HD + (h // (NH // SSD_GROUPS)) * HD + np.arange(HD) for h in range(NH)])


def _in_proj_columns():
    sizes = (BW, BW, BW + 2 * SSD_GROUPS * HD, NH, BW, 3 * BW, NH, NH, BW, BW, BW)
    (a_x, a_gate, b_xbc, b_dt, b_gate, c_qkv, c_beta, c_decay, c_gate, d_u, d_gate) = np.cumsum((0,) + sizes)[:-1]
    conv_idx = np.concatenate([a_x + np.arange(BW), b_xbc + _SSD_REP_IDX, c_qkv + np.arange(3 * BW)])
    rep = np.repeat(np.arange(NH), HD)
    rest_idx = np.concatenate([a_gate + np.arange(BW), b_gate + np.arange(BW), c_gate + np.arange(BW),
                               d_gate + np.arange(BW), d_u + np.arange(BW), b_dt + rep, c_beta + rep, c_decay + rep])
    return conv_idx, rest_idx


_CONV_IDX, _REST_IDX = _in_proj_columns()


def _block_diag(blocks):
    n, r, c = blocks.shape
    eye = jnp.eye(n, dtype=blocks.dtype)
    return (eye[:, None, :, None] * blocks[:, :, None, :]).reshape(n * r, n * c)


def _pad_rows(a, n):
    return jnp.concatenate([a, jnp.zeros((n - a.shape[0],) + a.shape[1:], a.dtype)], axis=0)


def _layer_call(x, states, prm, nseq, lc):
    bsz, seqlen, _ = x.shape
    rows = nseq * lc
    assert bsz % nseq == 0 and seqlen % lc == 0 and rows % GDN_CHUNK == 0 and lc % 8 == 0
    grid = (bsz // nseq, seqlen // lc)
    lseq, mseq = _seq_masks(rows, lc)
    lgdn, mgdn = _seq_masks(rows, min(lc, GDN_CHUNK))

    def batch_spec(shape):
        return pl.BlockSpec((nseq,) + shape, lambda i, j: (i,) + (0,) * len(shape))

    def const_spec(a):
        return pl.BlockSpec(a.shape, lambda i, j: (0,) * a.ndim)

    x_spec = pl.BlockSpec((nseq, lc, D_MODEL), lambda i, j: (i, j, 0))
    state_specs = [batch_spec(s.shape[1:]) for s in states]
    consts = [prm['w_conv'], prm['w_rest'], prm['w_out'], prm['v1024'], prm['v256'], prm['cwb'], prm['wg'], prm['bg'],
              prm['glu_w'], prm['bre'], prm['bim'], prm['cre'], prm['cim'], lseq, mseq, lgdn, mgdn]
    out_shape = [jax.ShapeDtypeStruct(x.shape, F32)] + [jax.ShapeDtypeStruct(s.shape, F32) for s in states]
    scratch = [
        pltpu.VMEM((nseq, HIST + lc, NCONV), F32),
        pltpu.VMEM((rows, NREST), F32),
        pltpu.VMEM((rows, 4 * BW), F32),
        pltpu.VMEM((S5_GROUPS * S5_GROUP, 2 * S5N), BF16),
        pltpu.VMEM((2 * S5N, S5_GROUPS * S5_GROUP), BF16),
        pltpu.VMEM((8, 8, S5N), F32),
    ]
    outs = pl.pallas_call(
        functools.partial(_layer_kernel, nseq, lc),
        grid=grid,
        in_specs=[x_spec] + state_specs + [const_spec(a) for a in consts],
        out_specs=[x_spec] + state_specs,
        out_shape=out_shape,
        scratch_shapes=scratch,
        compiler_params=pltpu.CompilerParams(dimension_semantics=("arbitrary", "arbitrary"),
                                             vmem_limit_bytes=VMEM_LIMIT_BYTES),
    )(x, *states, *consts)
    return outs[0], tuple(outs[1:])


def _prep_layer_params(l, w_in, w_out, ln_g, ln_b, rg_conv_w, rg_conv_b, rg_gate_a_w, rg_gate_a_b, rg_gate_x_w,
                       rg_gate_x_b, rg_lambda, ssd_conv_w, ssd_conv_b, ssd_dt_bias, ssd_a_log, ssd_d, ssd_norm_w,
                       gdn_conv_w, gdn_conv_b, gdn_dt_bias, gdn_a_log, gdn_norm_w, s5_lambda_re, s5_lambda_im,
                       s5_log_dt, s5_b_re, s5_b_im, s5_c_re, s5_c_im, s5_d, s5_glu_w, s5_glu_b):
    w = w_in[l]
    cw = jnp.concatenate([rg_conv_w[l], ssd_conv_w[l][:, _SSD_REP_IDX], gdn_conv_w[l]], axis=1)
    cb = jnp.concatenate([rg_conv_b[l], ssd_conv_b[l][_SSD_REP_IDX], gdn_conv_b[l]])[None, :]
    v256 = jnp.stack([rg_lambda[l], _head_rep(ssd_dt_bias[l]), _head_rep(ssd_a_log[l]), _head_rep(ssd_d[l]),
                      ssd_norm_w[l], _head_rep(gdn_dt_bias[l]), _head_rep(gdn_a_log[l]), jnp.tile(gdn_norm_w[l], NH),
                      s5_d[l], s5_glu_b[l]])
    v1024 = jnp.stack([ln_g[l], ln_b[l], s5_lambda_re[l].reshape(-1), s5_lambda_im[l].reshape(-1),
                       jnp.repeat(s5_log_dt[l], S5_STATE)])
    return dict(
        w_conv=w[:, _CONV_IDX].astype(BF16),
        w_rest=w[:, _REST_IDX].astype(BF16),
        w_out=w_out[l].astype(BF16),
        v1024=_pad_rows(v1024, 8),
        v256=_pad_rows(v256, 16),
        cwb=_pad_rows(jnp.concatenate([cw, cb], axis=0), 8),
        wg=jnp.concatenate([_block_diag(rg_gate_a_w[l]), _block_diag(rg_gate_x_w[l])], axis=1).astype(BF16),
        bg=jnp.concatenate([rg_gate_a_b[l], rg_gate_x_b[l]])[None, :],
        glu_w=s5_glu_w[l].astype(BF16),
        bre=_block_diag(jnp.swapaxes(s5_b_re[l], 1, 2)),
        bim=_block_diag(jnp.swapaxes(s5_b_im[l], 1, 2)),
        cre=_block_diag(jnp.swapaxes(s5_c_re[l], 1, 2)),
        cim=_block_diag(jnp.swapaxes(s5_c_im[l], 1, 2)),
    )


def _states_to_kernel(conv_a, h_a, conv_b, h_b, conv_c, s_c, s5_re, s5_im):
    bsz = h_a.shape[0]
    conv = jnp.concatenate([conv_a, conv_b[:, :, _SSD_REP_IDX], conv_c], axis=-1)
    conv = jnp.concatenate([jnp.zeros((bsz, HIST - (CONV_W - 1), NCONV), F32), conv], axis=1)
    ssd = jnp.transpose(h_b, (0, 3, 1, 2)).reshape(bsz, HD, BW)
    gdn = jnp.transpose(s_c, (0, 2, 1, 3)).reshape(bsz, HD, BW)
    s5 = jnp.concatenate([s5_re.reshape(bsz, 1, S5N), s5_im.reshape(bsz, 1, S5N)], axis=-1)
    return conv, h_a[:, None, :], ssd, gdn, s5


def _states_from_kernel(conv, rg, ssd, gdn, s5):
    bsz = rg.shape[0]
    hist = conv[:, HIST - (CONV_W - 1):, :]
    rep = NH // SSD_GROUPS
    conv_b = jnp.concatenate(
        [hist[..., C_BX:C_BX + BW]]
        + [hist[..., C_BB + g * rep * HD:C_BB + g * rep * HD + HD] for g in range(SSD_GROUPS)]
        + [hist[..., C_BC + g * rep * HD:C_BC + g * rep * HD + HD] for g in range(SSD_GROUPS)], axis=-1)
    return (hist[..., C_AX:C_AX + BW], rg[:, 0, :], conv_b,
            jnp.transpose(ssd.reshape(bsz, HD, NH, HD), (0, 2, 3, 1)),
            hist[..., C_Q:C_Q + 3 * BW],
            jnp.transpose(gdn.reshape(bsz, HD, NH, HD), (0, 2, 1, 3)),
            s5[:, 0, 0:S5N].reshape(bsz, S5_GROUPS, S5_STATE),
            s5[:, 0, S5N:2 * S5N].reshape(bsz, S5_GROUPS, S5_STATE))


def kernel(x_prompt, x_sample, cache_rglru_conv, state_rglru, cache_ssd_conv, state_ssd, cache_gdn_conv, state_gdn,
           state_s5_re, state_s5_im, w_in, w_out, ln_g, ln_b, rg_conv_w, rg_conv_b, rg_gate_a_w, rg_gate_a_b,
           rg_gate_x_w, rg_gate_x_b, rg_lambda, ssd_conv_w, ssd_conv_b, ssd_dt_bias, ssd_a_log, ssd_d, ssd_norm_w,
           gdn_conv_w, gdn_conv_b, gdn_dt_bias, gdn_a_log, gdn_norm_w, s5_lambda_re, s5_lambda_im, s5_log_dt,
           s5_b_re, s5_b_im, s5_c_re, s5_c_im, s5_d, s5_glu_w, s5_glu_b):
    weights = (w_in, w_out, ln_g, ln_b, rg_conv_w, rg_conv_b, rg_gate_a_w, rg_gate_a_b, rg_gate_x_w, rg_gate_x_b,
               rg_lambda, ssd_conv_w, ssd_conv_b, ssd_dt_bias, ssd_a_log, ssd_d, ssd_norm_w, gdn_conv_w, gdn_conv_b,
               gdn_dt_bias, gdn_a_log, gdn_norm_w, s5_lambda_re, s5_lambda_im, s5_log_dt, s5_b_re, s5_b_im,
               s5_c_re, s5_c_im, s5_d, s5_glu_w, s5_glu_b)
    sample_states = (cache_rglru_conv, state_rglru, cache_ssd_conv, state_ssd, cache_gdn_conv, state_gdn,
                     state_s5_re, state_s5_im)
    pb = x_prompt.shape[0]
    sb, sl = x_sample.shape[0], x_sample.shape[1]
    prompt_init = (jnp.zeros((pb, HIST, NCONV), F32), jnp.zeros((pb, 1, BW), F32), jnp.zeros((pb, HD, BW), F32),
                   jnp.zeros((pb, HD, BW), F32), jnp.zeros((pb, 1, 2 * S5N), F32))
    yp, ys = x_prompt, x_sample
    p_new, s_new = [], []
    for l in range(DEPTH):
        prm = _prep_layer_params(l, *weights)
        yp, st_p = _layer_call(yp, prompt_init, prm, nseq=1, lc=ROWS)
        ys, st_s = _layer_call(ys, _states_to_kernel(*(s[l] for s in sample_states)), prm, nseq=ROWS // sl, lc=sl)
        p_new.append(_states_from_kernel(*st_p))
        s_new.append(_states_from_kernel(*st_s))
    p_out = [jnp.stack([st[k] for st in p_new]) for k in range(8)]
    s_out = [jnp.stack([st[k] for st in s_new]) for k in range(8)]
    return (yp, ys, *p_out, *s_out)
```

```python
import functools
import math

import numpy as np
import jax
import jax.numpy as jnp
from jax import lax
from jax.experimental import pallas as pl
from jax.experimental.pallas import tpu as pltpu

F32 = jnp.float32
BF16 = jnp.bfloat16

D_MODEL = 1024
DEPTH = 2
BW = 256
HD = 64
NH = BW // HD
SSD_GROUPS = 2
CONV_W = 4
HIST = 8
S5_GROUPS = 16
S5_GROUP = 16
S5_STATE = 64
S5N = S5_GROUPS * S5_STATE
LANES = 128
SUBLANES = 8
RG_C = 8.0
ALPHA = (2.0 * DEPTH) ** 0.25
CHUNK = 64
ROWS = 512
SAMPLE_SEQS = 32
VMEM_LIMIT_BYTES = 60 * 1024 * 1024

NCONV = 6 * BW
C_AX, C_BX, C_BBC, C_Q, C_K, C_V = (k * BW for k in range(6))
NREST = 8 * BW
R_GATE, R_DU, R_DT, R_BETA, R_DECAY = 0, 4 * BW, 5 * BW, 6 * BW, 7 * BW

(V_RG_LAMBDA, V_SSD_DT_BIAS, V_SSD_A_LOG, V_SSD_D, V_SSD_NORM, V_GDN_DT_BIAS, V_GDN_A_LOG, V_GDN_NORM,
 V_S5_D, V_GLU_B) = range(10)
V_LN_G, V_LN_B, V_S5_LRE, V_S5_LIM, V_S5_LOGDT = range(5)


def _dot(a, b):
    return jnp.dot(a, b, preferred_element_type=F32)


def _dot_nt(a, b):
    return lax.dot_general(a, b, (((1,), (1,)), ((), ())), preferred_element_type=F32)


def _sigmoid(x):
    return jax.nn.sigmoid(x)


def _silu(x):
    return x * jax.nn.sigmoid(x)


def _softplus(x):
    return jnp.maximum(x, 0.0) + jnp.log1p(jnp.exp(-jnp.abs(x)))


def _split_bf16(x, pieces):
    out = []
    r = x
    for k in range(pieces):
        p = r.astype(BF16)
        out.append(p)
        if k + 1 < pieces:
            r = r - p.astype(F32)
    return out


def _exact_left(mask_bf16, x):
    acc = None
    for p in _split_bf16(x, 3):
        t = _dot(mask_bf16, p)
        acc = t if acc is None else acc + t
    return acc


def _iota(shape, dim):
    return lax.broadcasted_iota(jnp.int32, shape, dim)


def _log2(n):
    k = int(round(math.log2(n)))
    assert 1 << k == n
    return k


def _bd(x, mask):
    return jnp.where(mask, jnp.tile(x, (NH, 1)), 0.0).astype(BF16)


def _diag_blocks(full, mask):
    fm = jnp.where(mask, full, 0.0)
    return fm[0:HD] + fm[HD:2 * HD] + fm[2 * HD:3 * HD] + fm[3 * HD:4 * HD]


def _seg_diff(c, mask):
    j = _iota(c.shape, 1) & (HD - 1)
    p1, p2, p3 = (p.astype(F32) for p in _split_bf16(c, 3))
    zero = jnp.zeros_like(c)
    lhs = jnp.where(j == 0, p1, jnp.where(j == 1, p2, jnp.where(j == 2, p3, jnp.where(j < 6, 1.0, zero))))
    rhs = jnp.where(j < 3, 1.0, jnp.where(j == 3, -p1, jnp.where(j == 4, -p2, jnp.where(j == 5, -p3, zero))))
    return _dot_nt(lhs.astype(BF16), _bd(rhs, mask))


def _scan_pitch(lc):
    return lc if (lc // SUBLANES) % 2 == 1 else lc + SUBLANES


def _layer_kernel(nseq, lc,
                  x_ref, ca_in, cb_in, cc_in, rg_in, ssd_in, gdn_in, s5r_in, s5i_in,
                  w_conv, w_rest, w_out, v1024, v256, cwb, wg, bg, glu_w, bre, bim, cre, cim, lm_ref,
                  y_ref, ca_out, cb_out, cc_out, rg_out, ssd_out, gdn_out, s5r_out, s5i_out,
                  zext, zr, mix, wb_s, wc_s, tab, rgbuf, s5buf):
    rows = nseq * lc
    pitch = _scan_pitch(lc)
    ngroups = nseq // SUBLANES
    unit = min(lc, CHUNK)
    units = CHUNK // unit
    nchunks = rows // CHUNK
    first_call_step = (pl.program_id(0) == 0) & (pl.program_id(1) == 0)

    @pl.when(first_call_step)
    def _():
        lr = v1024[V_S5_LRE:V_S5_LRE + 1, :]
        li = v1024[V_S5_LIM:V_S5_LIM + 1, :]
        dt = jnp.exp(v1024[V_S5_LOGDT:V_S5_LOGDT + 1, :])
        mag = jnp.exp(lr * dt)
        ang = li * dt
        ar = mag * jnp.cos(ang)
        ai = mag * jnp.sin(ang)
        den = lr * lr + li * li
        fr = ((ar - 1.0) * lr + ai * li) / den
        fi = (ai * lr - (ar - 1.0) * li) / den
        wb_s[:, 0:S5N] = (fr * bre[...] - fi * bim[...]).astype(BF16)
        wb_s[:, S5N:2 * S5N] = (fr * bim[...] + fi * bre[...]).astype(BF16)
        wc_s[0:S5N, :] = cre[...].astype(BF16)
        wc_s[S5N:2 * S5N, :] = (-cim[...]).astype(BF16)
        tab[0:1, :] = ar
        tab[1:2, :] = ai

    @pl.when(pl.program_id(1) == 0)
    def _():
        ca_out[...] = ca_in[...]
        cb_out[...] = cb_in[...]
        cc_out[...] = cc_in[...]
        rg_out[...] = rg_in[...]
        ssd_out[...] = ssd_in[...]
        gdn_out[...] = gdn_in[...]
        s5r_out[...] = s5r_in[...]
        s5i_out[...] = s5i_in[...]

    def vec(r):
        return v256[r:r + 1, :]

    hd_mask = (_iota((BW, BW), 0) >> _log2(HD)) == (_iota((BW, BW), 1) >> _log2(HD))
    bd = functools.partial(_bd, mask=hd_mask)

    h0 = HIST - (CONV_W - 1)
    xb = x_ref[...].reshape(rows, D_MODEL).astype(BF16)
    zext[:, h0:HIST, C_AX:C_AX + BW] = ca_out[...]
    zext[:, h0:HIST, C_BX:C_BX + 2 * BW] = cb_out[...]
    zext[:, h0:HIST, C_Q:C_Q + 3 * BW] = cc_out[...]
    zext[:, HIST:HIST + lc, :] = _dot(xb, w_conv[...]).reshape(nseq, lc, NCONV)
    zr[...] = _dot(xb, w_rest[...])
    ca_out[...] = zext[:, lc + h0:lc + HIST, C_AX:C_AX + BW]
    cb_out[...] = zext[:, lc + h0:lc + HIST, C_BX:C_BX + 2 * BW]
    cc_out[...] = zext[:, lc + h0:lc + HIST, C_Q:C_Q + 3 * BW]

    def conv(c0, width):
        acc = cwb[CONV_W:CONV_W + 1, c0:c0 + width]
        for tap in range(CONV_W):
            acc = acc + cwb[tap:tap + 1, c0:c0 + width] * zext[:, h0 + tap:h0 + tap + lc, c0:c0 + width]
        return acc.reshape(rows, width)

    def rest(c0, width=BW):
        return zr[:, c0:c0 + width]

    def to_slabs(buf, first, val):
        for k in range(val.shape[1] // LANES):
            piece = val[:, k * LANES:(k + 1) * LANES]
            if pitch == lc:
                buf[first + k] = piece
            else:
                for q in range(nseq):
                    buf[first + k, q * pitch:q * pitch + lc, :] = piece[q * lc:(q + 1) * lc]

    def from_slabs(buf, first, n):
        cols = []
        for k in range(n):
            if pitch == lc:
                cols.append(buf[first + k])
            else:
                cols.append(jnp.concatenate([buf[first + k, q * pitch:q * pitch + lc, :] for q in range(nseq)], axis=0))
        return cols[0] if n == 1 else jnp.concatenate(cols, axis=1)

    def seq_rows(g, t):
        return pl.ds(g * SUBLANES * pitch + t, SUBLANES, stride=pitch)

    xc = conv(C_AX, BW)
    gts = _dot(xc.astype(BF16), wg[...]) + bg[...]
    gate_r = _sigmoid(gts[:, 0:BW])
    gate_i = _sigmoid(gts[:, BW:2 * BW])
    log_a = (-RG_C * _softplus(-vec(V_RG_LAMBDA))) * gate_r
    a = jnp.exp(log_a)
    to_slabs(rgbuf, 0, a)
    to_slabs(rgbuf, 2, jnp.sqrt(-jnp.tanh(log_a) * (a * a + 1.0)) * (gate_i * xc))
    for g in range(ngroups):
        gs = slice(g * SUBLANES, (g + 1) * SUBLANES)
        h = [rg_out[gs, k * LANES:(k + 1) * LANES] for k in range(2)]
        for t in range(lc):
            idx = seq_rows(g, t)
            for k in range(2):
                h[k] = rgbuf[k, idx, :] * h[k] + rgbuf[2 + k, idx, :]
                rgbuf[2 + k, idx, :] = h[k]
        for k in range(2):
            rg_out[gs, k * LANES:(k + 1) * LANES] = h[k]
    mix[:, 0:BW] = from_slabs(rgbuf, 2, 2) * _silu(rest(R_GATE))

    du = rest(R_DU)
    dub = du.astype(BF16)
    for m in range(2 * S5N // BW):
        to_slabs(s5buf, 2 * m, _dot(dub, wb_s[:, m * BW:(m + 1) * BW]))
    nsl = S5N // LANES
    a_re = [jnp.broadcast_to(tab[0:1, k * LANES:(k + 1) * LANES], (SUBLANES, LANES)) for k in range(nsl)]
    a_im = [jnp.broadcast_to(tab[1:2, k * LANES:(k + 1) * LANES], (SUBLANES, LANES)) for k in range(nsl)]
    for g in range(ngroups):
        gs = slice(g * SUBLANES, (g + 1) * SUBLANES)
        hr = [s5r_out[gs, k * LANES:(k + 1) * LANES] for k in range(nsl)]
        hi = [s5i_out[gs, k * LANES:(k + 1) * LANES] for k in range(nsl)]
        for t in range(lc):
            idx = seq_rows(g, t)
            for k in range(nsl):
                nr = a_re[k] * hr[k] - a_im[k] * hi[k] + s5buf[k, idx, :]
                ni = a_re[k] * hi[k] + a_im[k] * hr[k] + s5buf[nsl + k, idx, :]
                hr[k], hi[k] = nr, ni
                s5buf[k, idx, :] = nr
                s5buf[nsl + k, idx, :] = ni
        for k in range(nsl):
            s5r_out[gs, k * LANES:(k + 1) * LANES] = hr[k]
            s5i_out[gs, k * LANES:(k + 1) * LANES] = hi[k]
    y5 = vec(V_S5_D) * du
    for m in range(2 * S5N // BW):
        y5 = y5 + _dot(from_slabs(s5buf, 2 * m, 2).astype(BF16), wc_s[m * BW:(m + 1) * BW, :])
    y5 = 0.5 * y5 * (1.0 + lax.erf(y5 * math.sqrt(0.5)))
    y5 = y5 * _sigmoid(_dot(y5.astype(BF16), glu_w[...]) + vec(V_GLU_B))
    mix[:, 3 * BW:4 * BW] = y5 * _silu(rest(R_GATE + 3 * BW))

    ones_bd = hd_mask.astype(BF16)

    def head_sum(v):
        p1, p2 = _split_bf16(v, 2)
        return _dot(p1, ones_bd) + _dot(p2, ones_bd)

    xs = _silu(conv(C_BX, BW))
    bc = _silu(conv(C_BBC, BW))
    lane = _iota((rows, LANES), 1)

    def head_rep(pair):
        rolled = pltpu.roll(pair, HD, 1)
        return jnp.concatenate([jnp.where(lane < HD, pair, rolled), jnp.where(lane < HD, rolled, pair)], axis=1)

    bm = head_rep(bc[:, 0:LANES])
    cm = head_rep(bc[:, LANES:2 * LANES])
    dt = _softplus(rest(R_DT) + vec(V_SSD_DT_BIAS))
    da = dt * (-jnp.exp(vec(V_SSD_A_LOG)))
    xdt = xs * dt

    qc = _silu(conv(C_Q, BW))
    kc = _silu(conv(C_K, BW))
    vc = _silu(conv(C_V, BW))
    qn = qc * lax.rsqrt(head_sum(qc * qc) + 1e-6) * (HD ** -0.5)
    kn = kc * lax.rsqrt(head_sum(kc * kc) + 1e-6)
    beta = _sigmoid(rest(R_BETA))
    gdec = (-jnp.exp(vec(V_GDN_A_LOG))) * _softplus(rest(R_DECAY) + vec(V_GDN_DT_BIAS))
    vb = vc * beta

    t_c = _iota((CHUNK, BW), 0)
    s_c = _iota((CHUNK, BW), 1) & (HD - 1)
    same_unit = (t_c >> _log2(unit)) == (s_c >> _log2(unit))
    valid_incl = same_unit & (s_c <= t_c)
    valid_strict = same_unit & (s_c < t_c)
    eye_cat = jnp.where(s_c == t_c, 1.0, 0.0)
    lm = lm_ref[...]
    if units == 1:
        lm = lm[0:CHUNK]
    chunks = range(nchunks)
    sls = [slice(c * CHUNK, (c + 1) * CHUNK) for c in chunks]

    acum, atot, decay, dtot = [], [], [], []
    for c in chunks:
        cs = _exact_left(lm, jnp.concatenate([da[sls[c]], gdec[sls[c]]], axis=1))
        acum.append(cs[0:CHUNK, 0:BW])
        decay.append(cs[0:CHUNK, BW:2 * BW])
        if units == 1:
            atot.append(cs[CHUNK - 1:CHUNK, 0:BW])
            dtot.append(cs[CHUNK - 1:CHUNK, BW:2 * BW])
        else:
            atot.append(cs[CHUNK:2 * CHUNK, 0:BW])
            dtot.append(cs[CHUNK:2 * CHUNK, BW:2 * BW])

    lmat = [jnp.where(valid_incl, jnp.exp(jnp.minimum(_seg_diff(acum[c], hd_mask), 0.0)), 0.0) for c in chunks]
    cbm = [_dot_nt(cm[sls[c]].astype(BF16), bd(bm[sls[c]])) for c in chunks]
    y_ssd = [_dot((cbm[c] * lmat[c]).astype(BF16), bd(xdt[sls[c]])) for c in chunks]
    for c in chunks:
        cme = (cm[sls[c]] * jnp.exp(acum[c])).astype(BF16)
        wbt = (bm[sls[c]] * jnp.exp(atot[c] - acum[c])).T.astype(BF16)
        xdt_c = xdt[sls[c]]
        parts = []
        for u in range(units):
            q = (c * CHUNK + u * unit) // lc
            us = slice(u * unit, (u + 1) * unit)
            ht = ssd_out[q]
            parts.append(_dot(cme[us], bd(ht)))
            xq = xdt_c if units == 1 else jnp.where((t_c >> _log2(unit)) == u, xdt_c, 0.0)
            upd = _diag_blocks(_dot(wbt, xq.astype(BF16)), hd_mask)
            r0 = u * unit if units > 1 else 0
            ssd_out[q] = jnp.exp(atot[c][r0:r0 + 1, :]) * ht + upd
        y_ssd[c] = y_ssd[c] + (parts[0] if units == 1 else jnp.concatenate(parts, axis=0))
    yb = jnp.concatenate(y_ssd, axis=0) + vec(V_SSD_D) * xs
    yb = yb * _silu(rest(R_GATE + BW))
    yb = yb * lax.rsqrt(jnp.mean(yb * yb, axis=-1, keepdims=True) + 1e-6) * vec(V_SSD_NORM)
    mix[:, BW:2 * BW] = yb

    eg = [jnp.where(valid_incl, jnp.exp(jnp.minimum(_seg_diff(decay[c], hd_mask), 0.0)), 0.0) for c in chunks]
    qk_kk = [_dot_nt(jnp.concatenate([qn[sls[c]], kn[sls[c]]], axis=0).astype(BF16), bd(kn[sls[c]])) for c in chunks]
    qkg = [qk_kk[c][0:CHUNK] * eg[c] for c in chunks]
    mm = [jnp.where(valid_strict, beta[sls[c]] * qk_kk[c][CHUNK:2 * CHUNK] * eg[c], 0.0) for c in chunks]
    rm = [eye_cat - mm[c] for c in chunks]
    pw = [_dot(mm[c].astype(BF16), bd(mm[c])) for c in chunks]
    for _step in range(_log2(unit) - 2):
        pr2 = [_dot(jnp.concatenate([pw[c], rm[c]], axis=0).astype(BF16), bd(pw[c])) for c in chunks]
        pw = [pr2[c][0:CHUNK] for c in chunks]
        rm = [rm[c] + pr2[c][CHUNK:2 * CHUNK] for c in chunks]
    rm = [(rm[c] + _dot(rm[c].astype(BF16), bd(pw[c]))).astype(BF16) for c in chunks]
    edec = [jnp.exp(decay[c]) for c in chunks]
    value = [_dot(rm[c], bd(vb[sls[c]])) for c in chunks]
    kcum = [_dot(rm[c], bd(kn[sls[c]] * beta[sls[c]] * edec[c])) for c in chunks]
    o_chunks = []
    for c in chunks:
        qdec_c = qn[sls[c]] * edec[c]
        kdec_t = (kn[sls[c]] * jnp.exp(dtot[c] - decay[c])).T.astype(BF16)
        w_parts, o_parts, states = [], [], []
        for u in range(units):
            q = (c * CHUNK + u * unit) // lc
            us = slice(u * unit, (u + 1) * unit)
            s_q = gdn_out[q]
            states.append(s_q)
            kq = _dot(jnp.concatenate([kcum[c][us], qdec_c[us]], axis=0).astype(BF16), bd(s_q))
            w_parts.append(value[c][us] - kq[0:unit])
            o_parts.append(kq[unit:2 * unit])
        w = w_parts[0] if units == 1 else jnp.concatenate(w_parts, axis=0)
        o_state = o_parts[0] if units == 1 else jnp.concatenate(o_parts, axis=0)
        o_chunks.append(o_state + _dot(qkg[c].astype(BF16), bd(w)))
        for u in range(units):
            q = (c * CHUNK + u * unit) // lc
            wq = w if units == 1 else jnp.where((t_c >> _log2(unit)) == u, w, 0.0)
            upd = _diag_blocks(_dot(kdec_t, wq.astype(BF16)), hd_mask)
            r0 = u * unit if units > 1 else 0
            gdn_out[q] = jnp.exp(dtot[c][r0:r0 + 1, :]) * states[u] + upd
    o = jnp.concatenate(o_chunks, axis=0)
    o = o * lax.rsqrt(head_sum(o * o) * (1.0 / HD) + 1e-6) * vec(V_GDN_NORM)
    mix[:, 2 * BW:3 * BW] = o * _silu(rest(R_GATE + 2 * BW))

    res = ALPHA * x_ref[...].reshape(rows, D_MODEL) + _dot(mix[...].astype(BF16), w_out[...])
    mu = jnp.mean(res, axis=-1, keepdims=True)
    rc = res - mu
    var = jnp.mean(rc * rc, axis=-1, keepdims=True)
    y = rc * lax.rsqrt(var + 1e-5) * v1024[V_LN_G:V_LN_G + 1, :] + v1024[V_LN_B:V_LN_B + 1, :]
    y_ref[...] = y.reshape(y_ref.shape)


def _chunk_masks(unit):
    t = np.arange(CHUNK)
    same = (t[:, None] // unit) == (t[None, :] // unit)
    incl = same & (t[None, :] <= t[:, None])
    return jnp.asarray(np.concatenate([incl, same], axis=0), BF16)


def _block_diag(blocks):
    *lead, n, r, c = blocks.shape
    eye = jnp.eye(n, dtype=blocks.dtype)
    out = eye[:, None, :, None] * blocks[..., :, :, None, :]
    return out.reshape(*lead, n * r, n * c)


def _pad_axis1(a, n):
    return jnp.concatenate([a, jnp.zeros((a.shape[0], n - a.shape[1]) + a.shape[2:], a.dtype)], axis=1)


def _prep_params(w_in, w_out, ln_g, ln_b, rg_conv_w, rg_conv_b, rg_gate_a_w, rg_gate_a_b, rg_gate_x_w, rg_gate_x_b,
                 rg_lambda, ssd_conv_w, ssd_conv_b, ssd_dt_bias, ssd_a_log, ssd_d, ssd_norm_w, gdn_conv_w, gdn_conv_b,
                 gdn_dt_bias, gdn_a_log, gdn_norm_w, s5_lambda_re, s5_lambda_im, s5_log_dt, s5_b_re, s5_b_im,
                 s5_c_re, s5_c_im, s5_d, s5_glu_w, s5_glu_b):
    sizes = (BW, BW, BW + 2 * SSD_GROUPS * HD, NH, BW, 3 * BW, NH, NH, BW, BW, BW)
    (a_x, a_gate, b_xbc, b_dt, b_gate, c_qkv, c_beta, c_decay, c_gate, d_u, d_gate) = np.cumsum((0,) + sizes)[:-1].tolist()

    def cols(c0, n):
        return w_in[:, :, c0:c0 + n]

    def rep(c0):
        return jnp.repeat(cols(c0, NH), HD, axis=-1)

    def head_rep(v):
        return jnp.repeat(v, HD, axis=-1)

    cw = jnp.concatenate([rg_conv_w, ssd_conv_w, gdn_conv_w], axis=-1)
    cb = jnp.concatenate([rg_conv_b, ssd_conv_b, gdn_conv_b], axis=-1)[:, None, :]
    v256 = jnp.stack([rg_lambda, head_rep(ssd_dt_bias), head_rep(ssd_a_log), head_rep(ssd_d), ssd_norm_w,
                      head_rep(gdn_dt_bias), head_rep(gdn_a_log), jnp.tile(gdn_norm_w, (1, NH)), s5_d, s5_glu_b], axis=1)
    v1024 = jnp.stack([ln_g, ln_b, s5_lambda_re.reshape(DEPTH, S5N), s5_lambda_im.reshape(DEPTH, S5N),
                       jnp.repeat(s5_log_dt, S5_STATE, axis=-1)], axis=1)
    return [
        jnp.concatenate([cols(a_x, BW), cols(b_xbc, 2 * BW), cols(c_qkv, 3 * BW)], axis=-1).astype(BF16),
        jnp.concatenate([cols(a_gate, BW), cols(b_gate, BW), cols(c_gate, BW), cols(d_gate, BW), cols(d_u, BW),
                         rep(b_dt), rep(c_beta), rep(c_decay)], axis=-1).astype(BF16),
        w_out.astype(BF16),
        _pad_axis1(v1024, 8),
        _pad_axis1(v256, 16),
        _pad_axis1(jnp.concatenate([cw, cb], axis=1), 8),
        jnp.concatenate([_block_diag(rg_gate_a_w), _block_diag(rg_gate_x_w)], axis=-1).astype(BF16),
        jnp.concatenate([rg_gate_a_b, rg_gate_x_b], axis=-1)[:, None, :],
        s5_glu_w.astype(BF16),
        _block_diag(jnp.swapaxes(s5_b_re, -1, -2)),
        _block_diag(jnp.swapaxes(s5_b_im, -1, -2)),
        _block_diag(jnp.swapaxes(s5_c_re, -1, -2)),
        _block_diag(jnp.swapaxes(s5_c_im, -1, -2)),
    ]


def _layer_call(l, x, states, params, nseq, lc):
    bsz, seqlen, _ = x.shape
    rows = nseq * lc
    assert bsz % nseq == 0 and seqlen % lc == 0 and rows % CHUNK == 0 and lc % SUBLANES == 0 and nseq % SUBLANES == 0
    assert CHUNK % min(lc, CHUNK) == 0 and lc % min(lc, CHUNK) == 0
    pitch = _scan_pitch(lc)
    grid = (bsz // nseq, seqlen // lc)

    def layer_spec(a, batched):
        shape = a.shape[1:]
        if batched:
            return pl.BlockSpec((None, nseq) + shape[1:], lambda i, j: (l, i) + (0,) * (len(shape) - 1))
        return pl.BlockSpec((None,) + shape, lambda i, j: (l,) + (0,) * len(shape), pipeline_mode=pl.Buffered(1))

    lm = _chunk_masks(min(lc, CHUNK))
    x_spec = pl.BlockSpec((nseq, lc, D_MODEL), lambda i, j: (i, j, 0))
    state_specs = [layer_spec(s, True) for s in states]
    in_specs = ([x_spec] + state_specs + [layer_spec(p, False) for p in params]
                + [pl.BlockSpec(lm.shape, lambda i, j: (0, 0), pipeline_mode=pl.Buffered(1))])
    out_state_specs = [pl.BlockSpec((nseq,) + s.shape[2:], lambda i, j, n=s.ndim - 2: (i,) + (0,) * n) for s in states]
    out_shape = [jax.ShapeDtypeStruct(x.shape, F32)] + [jax.ShapeDtypeStruct(s.shape[1:], F32) for s in states]
    scratch = [
        pltpu.VMEM((nseq, HIST + lc, NCONV), F32),
        pltpu.VMEM((rows, NREST), F32),
        pltpu.VMEM((rows, 4 * BW), F32),
        pltpu.VMEM((S5_GROUPS * S5_GROUP, 2 * S5N), BF16),
        pltpu.VMEM((2 * S5N, S5_GROUPS * S5_GROUP), BF16),
        pltpu.VMEM((SUBLANES, S5N), F32),
        pltpu.VMEM((4, nseq * pitch, LANES), F32),
        pltpu.VMEM((2 * S5N // LANES, nseq * pitch, LANES), F32),
    ]
    outs = pl.pallas_call(
        functools.partial(_layer_kernel, nseq, lc),
        grid=grid,
        in_specs=in_specs,
        out_specs=[x_spec] + out_state_specs,
        out_shape=out_shape,
        scratch_shapes=scratch,
        compiler_params=pltpu.CompilerParams(dimension_semantics=("arbitrary", "arbitrary"),
                                             vmem_limit_bytes=VMEM_LIMIT_BYTES),
    )(x, *states, *params, lm)
    return outs[0], outs[1:]


def _states_to_kernel(conv_a, h_a, conv_b, h_b, conv_c, s_c, s5_re, s5_im):
    d, bsz = h_a.shape[0], h_a.shape[1]
    ssd = jnp.transpose(h_b, (0, 1, 4, 2, 3)).reshape(d, bsz, HD, BW)
    gdn = jnp.transpose(s_c, (0, 1, 3, 2, 4)).reshape(d, bsz, HD, BW)
    return [conv_a, conv_b, conv_c, h_a, ssd, gdn, s5_re.reshape(d, bsz, S5N), s5_im.reshape(d, bsz, S5N)]


def _states_from_kernel(per_layer):
    conv_a, conv_b, conv_c, rg, ssd, gdn, s5r, s5i = (jnp.stack(t) for t in zip(*per_layer))
    d, bsz = rg.shape[0], rg.shape[1]
    return (conv_a, rg, conv_b,
            jnp.transpose(ssd.reshape(d, bsz, HD, NH, HD), (0, 1, 3, 4, 2)),
            conv_c,
            jnp.transpose(gdn.reshape(d, bsz, HD, NH, HD), (0, 1, 3, 2, 4)),
            s5r.reshape(d, bsz, S5_GROUPS, S5_STATE), s5i.reshape(d, bsz, S5_GROUPS, S5_STATE))


def kernel(x_prompt, x_sample, cache_rglru_conv, state_rglru, cache_ssd_conv, state_ssd, cache_gdn_conv, state_gdn,
           state_s5_re, state_s5_im, w_in, w_out, ln_g, ln_b, rg_conv_w, rg_conv_b, rg_gate_a_w, rg_gate_a_b,
           rg_gate_x_w, rg_gate_x_b, rg_lambda, ssd_conv_w, ssd_conv_b, ssd_dt_bias, ssd_a_log, ssd_d, ssd_norm_w,
           gdn_conv_w, gdn_conv_b, gdn_dt_bias, gdn_a_log, gdn_norm_w, s5_lambda_re, s5_lambda_im, s5_log_dt,
           s5_b_re, s5_b_im, s5_c_re, s5_c_im, s5_d, s5_glu_w, s5_glu_b):
    params = _prep_params(w_in, w_out, ln_g, ln_b, rg_conv_w, rg_conv_b, rg_gate_a_w, rg_gate_a_b, rg_gate_x_w,
                          rg_gate_x_b, rg_lambda, ssd_conv_w, ssd_conv_b, ssd_dt_bias, ssd_a_log, ssd_d, ssd_norm_w,
                          gdn_conv_w, gdn_conv_b, gdn_dt_bias, gdn_a_log, gdn_norm_w, s5_lambda_re, s5_lambda_im,
                          s5_log_dt, s5_b_re, s5_b_im, s5_c_re, s5_c_im, s5_d, s5_glu_w, s5_glu_b)
    sample_states = _states_to_kernel(cache_rglru_conv, state_rglru, cache_ssd_conv, state_ssd, cache_gdn_conv,
                                      state_gdn, state_s5_re, state_s5_im)
    pb, pl_len = x_prompt.shape[0], x_prompt.shape[1]
    sl = x_sample.shape[1]
    prompt_states = [jnp.zeros((DEPTH, pb) + s.shape[2:], F32) for s in sample_states]
    lc_p = ROWS // pb
    assert pl_len % lc_p == 0
    yp, ys = x_prompt, x_sample
    p_new, s_new = [], []
    for l in range(DEPTH):
        yp, st_p = _layer_call(l, yp, prompt_states, params, nseq=pb, lc=lc_p)
        ys, st_s = _layer_call(l, ys, sample_states, params, nseq=SAMPLE_SEQS, lc=sl)
        p_new.append(st_p)
        s_new.append(st_s)
    return (yp, ys, *_states_from_kernel(p_new), *_states_from_kernel(s_new))
```

```python
import functools
import math

import numpy as np
import jax
import jax.numpy as jnp
from jax import lax
from jax.experimental import pallas as pl
from jax.experimental.pallas import tpu as pltpu

F32 = jnp.float32
BF16 = jnp.bfloat16

D_MODEL = 1024
DEPTH = 2
BW = 256
HD = 64
NH = BW // HD
SSD_GROUPS = 2
CONV_W = 4
HIST = 8
S5_GROUPS = 16
S5_GROUP = 16
S5_STATE = 64
S5N = S5_GROUPS * S5_STATE
LANES = 128
SUBLANES = 8
RG_C = 8.0
ALPHA = (2.0 * DEPTH) ** 0.25
CHUNK = 64
ROWS = 512
SAMPLE_SEQS = 32
VMEM_LIMIT_BYTES = 60 * 1024 * 1024

NCONV = 6 * BW
C_AX, C_BX, C_BBC, C_Q, C_K, C_V = (k * BW for k in range(6))
NREST = 8 * BW
R_GATE, R_DU, R_DT, R_BETA, R_DECAY = 0, 4 * BW, 5 * BW, 6 * BW, 7 * BW

(V_RG_LAMBDA, V_SSD_DT_BIAS, V_SSD_A_LOG, V_SSD_D, V_SSD_NORM, V_GDN_DT_BIAS, V_GDN_A_LOG, V_GDN_NORM,
 V_S5_D, V_GLU_B) = range(10)
V_LN_G, V_LN_B, V_S5_LRE, V_S5_LIM, V_S5_LOGDT = range(5)


def _dot(a, b):
    return jnp.dot(a, b, preferred_element_type=F32)


def _dot_nt(a, b):
    return lax.dot_general(a, b, (((1,), (1,)), ((), ())), preferred_element_type=F32)


def _sigmoid(x):
    return jax.nn.sigmoid(x)


def _silu(x):
    return x * jax.nn.sigmoid(x)


def _softplus(x):
    return jnp.maximum(x, 0.0) + jnp.log1p(jnp.exp(-jnp.abs(x)))


def _split_bf16(x, pieces):
    out = []
    r = x
    for k in range(pieces):
        p = r.astype(BF16)
        out.append(p)
        if k + 1 < pieces:
            r = r - p.astype(F32)
    return out


def _exact_left(mask_bf16, x):
    acc = None
    for p in _split_bf16(x, 3):
        t = _dot(mask_bf16, p)
        acc = t if acc is None else acc + t
    return acc


def _iota(shape, dim):
    return lax.broadcasted_iota(jnp.int32, shape, dim)


def _log2(n):
    k = int(round(math.log2(n)))
    assert 1 << k == n
    return k


def _bd(x, mask):
    return jnp.where(mask, jnp.tile(x, (NH, 1)), 0.0).astype(BF16)


def _diag_blocks(full, mask):
    fm = jnp.where(mask, full, 0.0)
    return fm[0:HD] + fm[HD:2 * HD] + fm[2 * HD:3 * HD] + fm[3 * HD:4 * HD]


def _scan_pitch(lc):
    return lc if (lc // SUBLANES) % 2 == 1 else lc + SUBLANES


def _layer_kernel(nseq, lc,
                  x_ref, ca_in, cb_in, cc_in, rg_in, ssd_in, gdn_in, s5r_in, s5i_in,
                  w_conv, w_rest, w_out, v1024, v256, cwb, wg, bg, glu_w, bre, bim, cre, cim, lm_ref,
                  y_ref, ca_out, cb_out, cc_out, rg_out, ssd_out, gdn_out, s5r_out, s5i_out,
                  zext, zr, mix, wb_s, wc_s, tab, rgbuf, s5buf):
    rows = nseq * lc
    pitch = _scan_pitch(lc)
    ngroups = nseq // SUBLANES
    unit = min(lc, CHUNK)
    units = CHUNK // unit
    nchunks = rows // CHUNK
    first_call_step = (pl.program_id(0) == 0) & (pl.program_id(1) == 0)

    @pl.when(first_call_step)
    def _():
        lr = v1024[V_S5_LRE:V_S5_LRE + 1, :]
        li = v1024[V_S5_LIM:V_S5_LIM + 1, :]
        dt = jnp.exp(v1024[V_S5_LOGDT:V_S5_LOGDT + 1, :])
        mag = jnp.exp(lr * dt)
        ang = li * dt
        ar = mag * jnp.cos(ang)
        ai = mag * jnp.sin(ang)
        den = lr * lr + li * li
        fr = ((ar - 1.0) * lr + ai * li) / den
        fi = (ai * lr - (ar - 1.0) * li) / den
        wb_s[:, 0:S5N] = (fr * bre[...] - fi * bim[...]).astype(BF16)
        wb_s[:, S5N:2 * S5N] = (fr * bim[...] + fi * bre[...]).astype(BF16)
        wc_s[0:S5N, :] = cre[...].astype(BF16)
        wc_s[S5N:2 * S5N, :] = (-cim[...]).astype(BF16)
        tab[0:1, :] = ar
        tab[1:2, :] = ai

    @pl.when(pl.program_id(1) == 0)
    def _():
        ca_out[...] = ca_in[...]
        cb_out[...] = cb_in[...]
        cc_out[...] = cc_in[...]
        rg_out[...] = rg_in[...]
        ssd_out[...] = ssd_in[...]
        gdn_out[...] = gdn_in[...]
        s5r_out[...] = s5r_in[...]
        s5i_out[...] = s5i_in[...]

    def vec(r):
        return v256[r:r + 1, :]

    hd_mask = (_iota((BW, BW), 0) >> _log2(HD)) == (_iota((BW, BW), 1) >> _log2(HD))
    bd = functools.partial(_bd, mask=hd_mask)

    h0 = HIST - (CONV_W - 1)

    def conv(c0):
        acc = cwb[CONV_W:CONV_W + 1, c0:c0 + BW]
        for tap in range(CONV_W):
            acc = acc + cwb[tap:tap + 1, c0:c0 + BW] * zext[:, h0 + tap:h0 + tap + lc, c0:c0 + BW]
        return acc.reshape(rows, BW)

    def rest(c0):
        return zr[:, c0:c0 + BW]

    def to_slabs(buf, first, val):
        for k in range(val.shape[1] // LANES):
            piece = val[:, k * LANES:(k + 1) * LANES]
            if pitch == lc:
                buf[first + k] = piece
            else:
                for q in range(nseq):
                    buf[first + k, q * pitch:q * pitch + lc, :] = piece[q * lc:(q + 1) * lc]

    def from_slabs(buf, first, n):
        cols = []
        for k in range(n):
            if pitch == lc:
                cols.append(buf[first + k])
            else:
                cols.append(jnp.concatenate([buf[first + k, q * pitch:q * pitch + lc, :] for q in range(nseq)], axis=0))
        return cols[0] if n == 1 else jnp.concatenate(cols, axis=1)

    def seq_rows(g, t):
        return pl.ds(g * SUBLANES * pitch + t, SUBLANES, stride=pitch)

    ones_bd = hd_mask.astype(BF16)

    def head_sum(v):
        p1, p2 = _split_bf16(v, 2)
        return _dot(p1, ones_bd) + _dot(p2, ones_bd)

    lane = _iota((rows, LANES), 1)

    def head_rep(pair):
        rolled = pltpu.roll(pair, HD, 1)
        return jnp.concatenate([jnp.where(lane < HD, pair, rolled), jnp.where(lane < HD, rolled, pair)], axis=1)


    n_conv_tiles, n_rest_tiles = NCONV // BW, NREST // BW

    xb = x_ref[...].reshape(rows, D_MODEL).astype(BF16)
    rest_order = [R_GATE // BW, R_DU // BW, R_DT // BW, R_BETA // BW, R_DECAY // BW,
                  R_GATE // BW + 1, R_GATE // BW + 2, R_GATE // BW + 3]
    assert sorted(rest_order) == list(range(n_rest_tiles))

    def ahead(n=1):
        for _ in range(n):
            if rest_order:
                k = rest_order.pop(0)
                zr[:, k * BW:(k + 1) * BW] = _dot(xb, w_rest[:, k * BW:(k + 1) * BW])

    def proj_conv_tile(k):
        zext[:, HIST:HIST + lc, k * BW:(k + 1) * BW] = _dot(xb, w_conv[:, k * BW:(k + 1) * BW]).reshape(nseq, lc, BW)

    zext[:, h0:HIST, C_AX:C_AX + BW] = ca_out[...]
    zext[:, h0:HIST, C_BX:C_BX + 2 * BW] = cb_out[...]
    zext[:, h0:HIST, C_Q:C_Q + 3 * BW] = cc_out[...]
    proj_conv_tile(0)
    proj_conv_tile(1)
    xc = conv(C_AX)
    proj_conv_tile(2)
    xs = _silu(conv(C_BX))
    proj_conv_tile(3)
    bc = _silu(conv(C_BBC))
    bm = head_rep(bc[:, 0:LANES])
    cm = head_rep(bc[:, LANES:2 * LANES])
    proj_conv_tile(4)
    qc = _silu(conv(C_Q))
    proj_conv_tile(5)
    kc = _silu(conv(C_K))
    ahead()
    vc = _silu(conv(C_V))
    ca_out[...] = zext[:, lc + h0:lc + HIST, C_AX:C_AX + BW]
    cb_out[...] = zext[:, lc + h0:lc + HIST, C_BX:C_BX + 2 * BW]
    cc_out[...] = zext[:, lc + h0:lc + HIST, C_Q:C_Q + 3 * BW]
    ahead()

    gts = _dot(xc.astype(BF16), wg[...]) + bg[...]
    gate_r = _sigmoid(gts[:, 0:BW])
    gate_i = _sigmoid(gts[:, BW:2 * BW])
    log_a = (-RG_C * _softplus(-vec(V_RG_LAMBDA))) * gate_r
    a = jnp.exp(log_a)
    to_slabs(rgbuf, 0, a)
    to_slabs(rgbuf, 2, jnp.sqrt(-jnp.tanh(log_a) * (a * a + 1.0)) * (gate_i * xc))
    for g in range(ngroups):
        gs = slice(g * SUBLANES, (g + 1) * SUBLANES)
        h = [rg_out[gs, k * LANES:(k + 1) * LANES] for k in range(2)]
        for t in range(lc):
            idx = seq_rows(g, t)
            for k in range(2):
                h[k] = rgbuf[k, idx, :] * h[k] + rgbuf[2 + k, idx, :]
                rgbuf[2 + k, idx, :] = h[k]
        for k in range(2):
            rg_out[gs, k * LANES:(k + 1) * LANES] = h[k]
    ahead()
    mix[:, 0:BW] = (from_slabs(rgbuf, 2, 2) * _silu(rest(R_GATE))).astype(BF16)

    du = rest(R_DU)
    dub = du.astype(BF16)
    qn = qc * lax.rsqrt(head_sum(qc * qc) + 1e-6) * (HD ** -0.5)
    kn = kc * lax.rsqrt(head_sum(kc * kc) + 1e-6)
    for m in range(2 * S5N // BW):
        to_slabs(s5buf, 2 * m, _dot(dub, wb_s[:, m * BW:(m + 1) * BW]))
    ahead(2)
    dt = _softplus(rest(R_DT) + vec(V_SSD_DT_BIAS))
    da = dt * (-jnp.exp(vec(V_SSD_A_LOG)))
    xdt = xs * dt
    beta = _sigmoid(rest(R_BETA))
    gdec = (-jnp.exp(vec(V_GDN_A_LOG))) * _softplus(rest(R_DECAY) + vec(V_GDN_DT_BIAS))
    vb = vc * beta

    nsl = S5N // LANES
    a_re = [jnp.broadcast_to(tab[0:1, k * LANES:(k + 1) * LANES], (SUBLANES, LANES)) for k in range(nsl)]
    a_im = [jnp.broadcast_to(tab[1:2, k * LANES:(k + 1) * LANES], (SUBLANES, LANES)) for k in range(nsl)]
    s5_state = {}

    def s5_step(g, t):
        gs = slice(g * SUBLANES, (g + 1) * SUBLANES)
        if t == 0:
            s5_state['r'] = [s5r_out[gs, k * LANES:(k + 1) * LANES] for k in range(nsl)]
            s5_state['i'] = [s5i_out[gs, k * LANES:(k + 1) * LANES] for k in range(nsl)]
        hr, hi = s5_state['r'], s5_state['i']
        idx = seq_rows(g, t)
        for k in range(nsl):
            nr = a_re[k] * hr[k] - a_im[k] * hi[k] + s5buf[k, idx, :]
            ni = a_re[k] * hi[k] + a_im[k] * hr[k] + s5buf[nsl + k, idx, :]
            hr[k], hi[k] = nr, ni
            s5buf[k, idx, :] = nr
            s5buf[nsl + k, idx, :] = ni
        if t == lc - 1:
            for k in range(nsl):
                s5r_out[gs, k * LANES:(k + 1) * LANES] = hr[k]
                s5i_out[gs, k * LANES:(k + 1) * LANES] = hi[k]

    s5_steps = [(g, t) for g in range(ngroups) for t in range(lc)]
    n_parts = 16
    s5_done = [0]

    def s5_part():
        per = -(-len(s5_steps) // n_parts)
        for g, t in s5_steps[s5_done[0]:s5_done[0] + per]:
            s5_step(g, t)
        s5_done[0] += per

    t_c = _iota((CHUNK, BW), 0)
    s_c = _iota((CHUNK, BW), 1) & (HD - 1)
    same_unit = (t_c >> _log2(unit)) == (s_c >> _log2(unit))
    valid_incl = same_unit & (s_c <= t_c)
    valid_strict = same_unit & (s_c < t_c)
    eye_cat = jnp.where(s_c == t_c, 1.0, 0.0)
    lm = lm_ref[...]
    if units == 1:
        lm = lm[0:CHUNK]
    chunks = range(nchunks)
    sls = [slice(c * CHUNK, (c + 1) * CHUNK) for c in chunks]

    strict_f = jnp.where(s_c < t_c, 1.0, 0.0)
    acum, atot, decay, dtot, lmat, eg = [], [], [], [], [], []
    for c in chunks:
        da_c, gd_c = da[sls[c]], gdec[sls[c]]
        cs = _exact_left(lm, jnp.concatenate([da_c, gd_c, da_c * strict_f, gd_c * strict_f], axis=1))
        acum.append(cs[0:CHUNK, 0:BW])
        decay.append(cs[0:CHUNK, BW:2 * BW])
        lmat.append(jnp.where(valid_incl, jnp.exp(jnp.minimum(cs[0:CHUNK, 2 * BW:3 * BW], 0.0)), 0.0))
        eg.append(jnp.where(valid_incl, jnp.exp(jnp.minimum(cs[0:CHUNK, 3 * BW:4 * BW], 0.0)), 0.0))
        if units == 1:
            atot.append(cs[CHUNK - 1:CHUNK, 0:BW])
            dtot.append(cs[CHUNK - 1:CHUNK, BW:2 * BW])
        else:
            atot.append(cs[CHUNK:2 * CHUNK, 0:BW])
            dtot.append(cs[CHUNK:2 * CHUNK, BW:2 * BW])
    s5_part()
    ahead()

    def seq_of(c, u):
        return (c * CHUNK + u * unit) // lc

    def unit_rows(u):
        return slice(u * unit, (u + 1) * unit)

    def only_unit(v, u):
        return v if units == 1 else jnp.where((t_c >> _log2(unit)) == u, v, 0.0)

    def unit_total(tot_c, u):
        r0 = u * unit if units > 1 else 0
        return tot_c[r0:r0 + 1, :]

    def cat_rows(parts):
        return parts[0] if len(parts) == 1 else jnp.concatenate(parts, axis=0)

    cbm = [_dot_nt(cm[sls[c]].astype(BF16), bd(bm[sls[c]])) for c in chunks]
    s5_part()
    y_ssd = [_dot((cbm[c] * lmat[c]).astype(BF16), bd(xdt[sls[c]])) for c in chunks]
    s5_part()
    ahead()
    ht = [[ssd_out[seq_of(c, u)] for u in range(units)] for c in chunks]
    cme = [(cm[sls[c]] * jnp.exp(acum[c])).astype(BF16) for c in chunks]
    y_int = [cat_rows([_dot(cme[c][unit_rows(u)], bd(ht[c][u])) for u in range(units)]) for c in chunks]
    s5_part()
    wbt = [(bm[sls[c]] * jnp.exp(atot[c] - acum[c])).T.astype(BF16) for c in chunks]
    upd = [[_dot(wbt[c], only_unit(xdt[sls[c]], u).astype(BF16)) for u in range(units)] for c in chunks]
    for c in chunks:
        for u in range(units):
            ssd_out[seq_of(c, u)] = jnp.exp(unit_total(atot[c], u)) * ht[c][u] + _diag_blocks(upd[c][u], hd_mask)
    s5_part()
    ahead()

    qk_kk = [_dot_nt(jnp.concatenate([qn[sls[c]], kn[sls[c]]], axis=0).astype(BF16), bd(kn[sls[c]])) for c in chunks]
    s5_part()
    qkg = [qk_kk[c][0:CHUNK] * eg[c] for c in chunks]
    mm = [jnp.where(valid_strict, beta[sls[c]] * qk_kk[c][CHUNK:2 * CHUNK] * eg[c], 0.0) for c in chunks]
    rm = [eye_cat - mm[c] for c in chunks]
    pw = [_dot(mm[c].astype(BF16), bd(mm[c])) for c in chunks]
    s5_part()
    ahead()
    for _step in range(_log2(unit) - 2):
        pr2 = [_dot(jnp.concatenate([pw[c], rm[c]], axis=0).astype(BF16), bd(pw[c])) for c in chunks]
        pw = [pr2[c][0:CHUNK] for c in chunks]
        rm = [rm[c] + pr2[c][CHUNK:2 * CHUNK] for c in chunks]
        s5_part()
    rm = [(rm[c] + _dot(rm[c].astype(BF16), bd(pw[c]))).astype(BF16) for c in chunks]
    s5_part()
    ahead()
    edec = [jnp.exp(decay[c]) for c in chunks]
    value = [_dot(rm[c], bd(vb[sls[c]])) for c in chunks]
    kcum = [_dot(rm[c], bd(kn[sls[c]] * beta[sls[c]] * edec[c])) for c in chunks]
    s5_part()
    sq = [[gdn_out[seq_of(c, u)] for u in range(units)] for c in chunks]
    qdec = [qn[sls[c]] * edec[c] for c in chunks]
    kq = [[_dot(jnp.concatenate([kcum[c][unit_rows(u)], qdec[c][unit_rows(u)]], axis=0).astype(BF16), bd(sq[c][u]))
           for u in range(units)] for c in chunks]
    s5_part()
    ahead()
    wv = [cat_rows([value[c][unit_rows(u)] - kq[c][u][0:unit] for u in range(units)]) for c in chunks]
    o_chunks = [cat_rows([kq[c][u][unit:2 * unit] for u in range(units)]) + _dot(qkg[c].astype(BF16), bd(wv[c]))
                for c in chunks]
    kdec_t = [(kn[sls[c]] * jnp.exp(dtot[c] - decay[c])).T.astype(BF16) for c in chunks]
    upd = [[_dot(kdec_t[c], only_unit(wv[c], u).astype(BF16)) for u in range(units)] for c in chunks]
    for c in chunks:
        for u in range(units):
            gdn_out[seq_of(c, u)] = jnp.exp(unit_total(dtot[c], u)) * sq[c][u] + _diag_blocks(upd[c][u], hd_mask)
    while s5_done[0] < len(s5_steps):
        s5_part()
    ahead(n_conv_tiles + n_rest_tiles)

    y5 = vec(V_S5_D) * du
    for m in range(S5N // BW):
        y5 = y5 + _dot(from_slabs(s5buf, 2 * m, 2).astype(BF16), wc_s[m * BW:(m + 1) * BW, :])
    yb = jnp.concatenate([y_ssd[c] + y_int[c] for c in chunks], axis=0) + vec(V_SSD_D) * xs
    yb = yb * _silu(rest(R_GATE + BW))
    yb = yb * lax.rsqrt(jnp.mean(yb * yb, axis=-1, keepdims=True) + 1e-6) * vec(V_SSD_NORM)
    mix[:, BW:2 * BW] = yb.astype(BF16)
    for m in range(S5N // BW, 2 * S5N // BW):
        y5 = y5 + _dot(from_slabs(s5buf, 2 * m, 2).astype(BF16), wc_s[m * BW:(m + 1) * BW, :])
    o = jnp.concatenate(o_chunks, axis=0)
    o = o * lax.rsqrt(head_sum(o * o) * (1.0 / HD) + 1e-6) * vec(V_GDN_NORM)
    mix[:, 2 * BW:3 * BW] = (o * _silu(rest(R_GATE + 2 * BW))).astype(BF16)
    y5 = 0.5 * y5 * (1.0 + lax.erf(y5 * math.sqrt(0.5)))
    y5 = y5 * _sigmoid(_dot(y5.astype(BF16), glu_w[...]) + vec(V_GLU_B))
    mix[:, 3 * BW:4 * BW] = (y5 * _silu(rest(R_GATE + 3 * BW))).astype(BF16)

    half = rows // 2
    seq_half = nseq // 2

    def out_proj(hh):
        xh = x_ref[hh * seq_half:(hh + 1) * seq_half].reshape(half, D_MODEL)
        return ALPHA * xh + _dot(mix[hh * half:(hh + 1) * half, :], w_out[...])

    def layer_norm(hh, res):
        mu = jnp.mean(res, axis=-1, keepdims=True)
        rc = res - mu
        var = jnp.mean(rc * rc, axis=-1, keepdims=True)
        y = rc * lax.rsqrt(var + 1e-5) * v1024[V_LN_G:V_LN_G + 1, :] + v1024[V_LN_B:V_LN_B + 1, :]
        y_ref[hh * seq_half:(hh + 1) * seq_half] = y.reshape(seq_half, lc, D_MODEL)

    res0 = out_proj(0)
    res1 = out_proj(1)
    layer_norm(0, res0)
    layer_norm(1, res1)


def _chunk_masks(unit):
    t = np.arange(CHUNK)
    same = (t[:, None] // unit) == (t[None, :] // unit)
    incl = same & (t[None, :] <= t[:, None])
    return jnp.asarray(np.concatenate([incl, same], axis=0), BF16)


def _block_diag(blocks):
    *lead, n, r, c = blocks.shape
    eye = jnp.eye(n, dtype=blocks.dtype)
    out = eye[:, None, :, None] * blocks[..., :, :, None, :]
    return out.reshape(*lead, n * r, n * c)


def _pad_axis1(a, n):
    return jnp.concatenate([a, jnp.zeros((a.shape[0], n - a.shape[1]) + a.shape[2:], a.dtype)], axis=1)


def _prep_params(w_in, w_out, ln_g, ln_b, rg_conv_w, rg_conv_b, rg_gate_a_w, rg_gate_a_b, rg_gate_x_w, rg_gate_x_b,
                 rg_lambda, ssd_conv_w, ssd_conv_b, ssd_dt_bias, ssd_a_log, ssd_d, ssd_norm_w, gdn_conv_w, gdn_conv_b,
                 gdn_dt_bias, gdn_a_log, gdn_norm_w, s5_lambda_re, s5_lambda_im, s5_log_dt, s5_b_re, s5_b_im,
                 s5_c_re, s5_c_im, s5_d, s5_glu_w, s5_glu_b):
    sizes = (BW, BW, BW + 2 * SSD_GROUPS * HD, NH, BW, 3 * BW, NH, NH, BW, BW, BW)
    (a_x, a_gate, b_xbc, b_dt, b_gate, c_qkv, c_beta, c_decay, c_gate, d_u, d_gate) = np.cumsum((0,) + sizes)[:-1].tolist()

    def cols(c0, n):
        return w_in[:, :, c0:c0 + n]

    def rep(c0):
        return jnp.repeat(cols(c0, NH), HD, axis=-1)

    def head_rep(v):
        return jnp.repeat(v, HD, axis=-1)

    cw = jnp.concatenate([rg_conv_w, ssd_conv_w, gdn_conv_w], axis=-1)
    cb = jnp.concatenate([rg_conv_b, ssd_conv_b, gdn_conv_b], axis=-1)[:, None, :]
    v256 = jnp.stack([rg_lambda, head_rep(ssd_dt_bias), head_rep(ssd_a_log), head_rep(ssd_d), ssd_norm_w,
                      head_rep(gdn_dt_bias), head_rep(gdn_a_log), jnp.tile(gdn_norm_w, (1, NH)), s5_d, s5_glu_b], axis=1)
    v1024 = jnp.stack([ln_g, ln_b, s5_lambda_re.reshape(DEPTH, S5N), s5_lambda_im.reshape(DEPTH, S5N),
                       jnp.repeat(s5_log_dt, S5_STATE, axis=-1)], axis=1)
    return [
        jnp.concatenate([cols(a_x, BW), cols(b_xbc, 2 * BW), cols(c_qkv, 3 * BW)], axis=-1).astype(BF16),
        jnp.concatenate([cols(a_gate, BW), cols(b_gate, BW), cols(c_gate, BW), cols(d_gate, BW), cols(d_u, BW),
                         rep(b_dt), rep(c_beta), rep(c_decay)], axis=-1).astype(BF16),
        w_out.astype(BF16),
        _pad_axis1(v1024, 8),
        _pad_axis1(v256, 16),
        _pad_axis1(jnp.concatenate([cw, cb], axis=1), 8),
        jnp.concatenate([_block_diag(rg_gate_a_w), _block_diag(rg_gate_x_w)], axis=-1).astype(BF16),
        jnp.concatenate([rg_gate_a_b, rg_gate_x_b], axis=-1)[:, None, :],
        s5_glu_w.astype(BF16),
        _block_diag(jnp.swapaxes(s5_b_re, -1, -2)),
        _block_diag(jnp.swapaxes(s5_b_im, -1, -2)),
        _block_diag(jnp.swapaxes(s5_c_re, -1, -2)),
        _block_diag(jnp.swapaxes(s5_c_im, -1, -2)),
    ]


def _layer_call(l, x, states, params, nseq, lc):
    bsz, seqlen, _ = x.shape
    rows = nseq * lc
    assert bsz % nseq == 0 and seqlen % lc == 0 and rows % CHUNK == 0 and lc % SUBLANES == 0 and nseq % SUBLANES == 0
    assert CHUNK % min(lc, CHUNK) == 0 and lc % min(lc, CHUNK) == 0
    pitch = _scan_pitch(lc)
    grid = (bsz // nseq, seqlen // lc)

    def layer_spec(a, batched):
        shape = a.shape[1:]
        if batched:
            return pl.BlockSpec((None, nseq) + shape[1:], lambda i, j: (l, i) + (0,) * (len(shape) - 1))
        return pl.BlockSpec((None,) + shape, lambda i, j: (l,) + (0,) * len(shape), pipeline_mode=pl.Buffered(1))

    lm = _chunk_masks(min(lc, CHUNK))
    x_spec = pl.BlockSpec((nseq, lc, D_MODEL), lambda i, j: (i, j, 0))
    state_specs = [layer_spec(s, True) for s in states]
    in_specs = ([x_spec] + state_specs + [layer_spec(p, False) for p in params]
                + [pl.BlockSpec(lm.shape, lambda i, j: (0, 0), pipeline_mode=pl.Buffered(1))])
    out_state_specs = [pl.BlockSpec((nseq,) + s.shape[2:], lambda i, j, n=s.ndim - 2: (i,) + (0,) * n) for s in states]
    out_shape = [jax.ShapeDtypeStruct(x.shape, F32)] + [jax.ShapeDtypeStruct(s.shape[1:], F32) for s in states]
    scratch = [
        pltpu.VMEM((nseq, HIST + lc, NCONV), F32),
        pltpu.VMEM((rows, NREST), F32),
        pltpu.VMEM((rows, 4 * BW), BF16),
        pltpu.VMEM((S5_GROUPS * S5_GROUP, 2 * S5N), BF16),
        pltpu.VMEM((2 * S5N, S5_GROUPS * S5_GROUP), BF16),
        pltpu.VMEM((SUBLANES, S5N), F32),
        pltpu.VMEM((4, nseq * pitch, LANES), F32),
        pltpu.VMEM((2 * S5N // LANES, nseq * pitch, LANES), F32),
    ]
    outs = pl.pallas_call(
        functools.partial(_layer_kernel, nseq, lc),
        grid=grid,
        in_specs=in_specs,
        out_specs=[x_spec] + out_state_specs,
        out_shape=out_shape,
        scratch_shapes=scratch,
        compiler_params=pltpu.CompilerParams(dimension_semantics=("arbitrary", "arbitrary"),
                                             vmem_limit_bytes=VMEM_LIMIT_BYTES),
    )(x, *states, *params, lm)
    return outs[0], outs[1:]


def _states_to_kernel(conv_a, h_a, conv_b, h_b, conv_c, s_c, s5_re, s5_im):
    d, bsz = h_a.shape[0], h_a.shape[1]
    ssd = jnp.transpose(h_b, (0, 1, 4, 2, 3)).reshape(d, bsz, HD, BW)
    gdn = jnp.transpose(s_c, (0, 1, 3, 2, 4)).reshape(d, bsz, HD, BW)
    return [conv_a, conv_b, conv_c, h_a, ssd, gdn, s5_re.reshape(d, bsz, S5N), s5_im.reshape(d, bsz, S5N)]


def _states_from_kernel(per_layer):
    conv_a, conv_b, conv_c, rg, ssd, gdn, s5r, s5i = (jnp.stack(t) for t in zip(*per_layer))
    d, bsz = rg.shape[0], rg.shape[1]
    return (conv_a, rg, conv_b,
            jnp.transpose(ssd.reshape(d, bsz, HD, NH, HD), (0, 1, 3, 4, 2)),
            conv_c,
            jnp.transpose(gdn.reshape(d, bsz, HD, NH, HD), (0, 1, 3, 2, 4)),
            s5r.reshape(d, bsz, S5_GROUPS, S5_STATE), s5i.reshape(d, bsz, S5_GROUPS, S5_STATE))


def kernel(x_prompt, x_sample, cache_rglru_conv, state_rglru, cache_ssd_conv, state_ssd, cache_gdn_conv, state_gdn,
           state_s5_re, state_s5_im, w_in, w_out, ln_g, ln_b, rg_conv_w, rg_conv_b, rg_gate_a_w, rg_gate_a_b,
           rg_gate_x_w, rg_gate_x_b, rg_lambda, ssd_conv_w, ssd_conv_b, ssd_dt_bias, ssd_a_log, ssd_d, ssd_norm_w,
           gdn_conv_w, gdn_conv_b, gdn_dt_bias, gdn_a_log, gdn_norm_w, s5_lambda_re, s5_lambda_im, s5_log_dt,
           s5_b_re, s5_b_im, s5_c_re, s5_c_im, s5_d, s5_glu_w, s5_glu_b):
    params = _prep_params(w_in, w_out, ln_g, ln_b, rg_conv_w, rg_conv_b, rg_gate_a_w, rg_gate_a_b, rg_gate_x_w,
                          rg_gate_x_b, rg_lambda, ssd_conv_w, ssd_conv_b, ssd_dt_bias, ssd_a_log, ssd_d, ssd_norm_w,
                          gdn_conv_w, gdn_conv_b, gdn_dt_bias, gdn_a_log, gdn_norm_w, s5_lambda_re, s5_lambda_im,
                          s5_log_dt, s5_b_re, s5_b_im, s5_c_re, s5_c_im, s5_d, s5_glu_w, s5_glu_b)
    sample_states = _states_to_kernel(cache_rglru_conv, state_rglru, cache_ssd_conv, state_ssd, cache_gdn_conv,
                                      state_gdn, state_s5_re, state_s5_im)
    pb, pl_len = x_prompt.shape[0], x_prompt.shape[1]
    sl = x_sample.shape[1]
    prompt_states = [jnp.zeros((DEPTH, pb) + s.shape[2:], F32) for s in sample_states]
    lc_p = ROWS // pb
    assert pl_len % lc_p == 0
    yp, ys = x_prompt, x_sample
    p_new, s_new = [], []
    for l in range(DEPTH):
        yp, st_p = _layer_call(l, yp, prompt_states, params, nseq=pb, lc=lc_p)
        ys, st_s = _layer_call(l, ys, sample_states, params, nseq=SAMPLE_SEQS, lc=sl)
        p_new.append(st_p)
        s_new.append(st_s)
    return (yp, ys, *_states_from_kernel(p_new), *_states_from_kernel(s_new))
```

```python
import functools
import math

import numpy as np
import jax
import jax.numpy as jnp
from jax import lax
from jax.experimental import pallas as pl
from jax.experimental.pallas import tpu as pltpu

F32 = jnp.float32
BF16 = jnp.bfloat16

D_MODEL = 1024
DEPTH = 2
BW = 256
HD = 64
NH = BW // HD
SSD_GROUPS = 2
CONV_W = 4
HIST = 8
S5_GROUPS = 16
S5_GROUP = 16
S5_STATE = 64
S5N = S5_GROUPS * S5_STATE
LANES = 128
SUBLANES = 8
RG_C = 8.0
ALPHA = (2.0 * DEPTH) ** 0.25
CHUNK = 64
ROWS = 512
SAMPLE_SEQS = 32
VMEM_LIMIT_BYTES = 60 * 1024 * 1024

NCONV = 6 * BW
C_AX, C_BX, C_BBC, C_Q, C_K, C_V = (k * BW for k in range(6))
NREST = 8 * BW
R_GATE, R_DU, R_DT, R_BETA, R_DECAY = 0, 4 * BW, 5 * BW, 6 * BW, 7 * BW


def _dot(a, b):
    return jnp.dot(a, b, preferred_element_type=F32)


def _dot_nt(a, b):
    return lax.dot_general(a, b, (((1,), (1,)), ((), ())), preferred_element_type=F32)


def _sigmoid(x):
    return jax.nn.sigmoid(x)


def _silu(x):
    return x * jax.nn.sigmoid(x)


def _softplus(x):
    return jnp.maximum(x, 0.0) + jnp.log1p(jnp.exp(-jnp.abs(x)))


def _split_bf16(x, pieces):
    out = []
    r = x
    for k in range(pieces):
        p = r.astype(BF16)
        out.append(p)
        if k + 1 < pieces:
            r = r - p.astype(F32)
    return out


def _exact_left(mask_bf16, x):
    acc = None
    for p in _split_bf16(x, 3):
        t = _dot(mask_bf16, p)
        acc = t if acc is None else acc + t
    return acc


def _iota(shape, dim):
    return lax.broadcasted_iota(jnp.int32, shape, dim)


def _log2(n):
    k = int(round(math.log2(n)))
    assert 1 << k == n
    return k


def _bd(x, mask):
    return jnp.where(mask, jnp.tile(x, (NH, 1)), 0.0).astype(BF16)


def _diag_blocks(full, mask):
    fm = jnp.where(mask, full, 0.0)
    return fm[0:HD] + fm[HD:2 * HD] + fm[2 * HD:3 * HD] + fm[3 * HD:4 * HD]


def _scan_pitch(lc):
    return lc if (lc // SUBLANES) % 2 == 1 else lc + SUBLANES


_PARAM_NAMES = ('w_conv', 'w_rest', 'w_out', 'wg', 's5v', 'bre', 'bim', 'cre', 'cim', 'glu_w',
                'ln_g', 'ln_b', 'rg_conv_w', 'ssd_conv_w', 'gdn_conv_w', 'rg_conv_b', 'ssd_conv_b', 'gdn_conv_b',
                'rg_gate_a_b', 'rg_gate_x_b', 'rg_lambda', 'ssd_dt_bias', 'ssd_a_log', 'ssd_d', 'ssd_norm_w',
                'gdn_dt_bias', 'gdn_a_log', 'gdn_norm_w', 's5_d', 's5_glu_b')
_LAYER_BLOCKS = frozenset(_PARAM_NAMES[:10])
_N_STATES = 8
S5V_LRE, S5V_LIM, S5V_LOGDT = range(3)


def _layer_kernel(l, nseq, lc, zero_init, *refs):
    n_in = 0 if zero_init else _N_STATES
    x_ref = refs[0]
    state_in = refs[1:1 + n_in]
    prm = dict(zip(_PARAM_NAMES, refs[1 + n_in:1 + n_in + len(_PARAM_NAMES)]))
    rest_refs = refs[1 + n_in + len(_PARAM_NAMES):]
    lm_ref, y_ref = rest_refs[0], rest_refs[1]
    state_out = rest_refs[2:2 + _N_STATES]
    ca_out, cb_out, cc_out, rg_out, ssd_out, gdn_out, s5r_out, s5i_out = state_out
    zext, zr, mix, wb_s, wc_s, tab, rgbuf, s5buf = rest_refs[2 + _N_STATES:]
    w_conv, w_rest, w_out, wg, glu_w = (prm[k] for k in ('w_conv', 'w_rest', 'w_out', 'wg', 'glu_w'))
    s5v, bre, bim, cre, cim = (prm[k] for k in ('s5v', 'bre', 'bim', 'cre', 'cim'))

    rows = nseq * lc
    pitch = _scan_pitch(lc)
    ngroups = nseq // SUBLANES
    unit = min(lc, CHUNK)
    units = CHUNK // unit
    nchunks = rows // CHUNK
    first_call_step = (pl.program_id(0) == 0) & (pl.program_id(1) == 0)

    def vec(name):
        return prm[name][l:l + 1, :]

    def head_vec(name):
        r = vec(name)
        return jnp.concatenate([jnp.broadcast_to(r[:, h:h + 1], (1, HD)) for h in range(NH)], axis=1)

    @pl.when(first_call_step)
    def _():
        lr = s5v[S5V_LRE:S5V_LRE + 1, :]
        li = s5v[S5V_LIM:S5V_LIM + 1, :]
        dt = jnp.exp(s5v[S5V_LOGDT:S5V_LOGDT + 1, :])
        mag = jnp.exp(lr * dt)
        ang = li * dt
        ar = mag * jnp.cos(ang)
        ai = mag * jnp.sin(ang)
        den = lr * lr + li * li
        fr = ((ar - 1.0) * lr + ai * li) / den
        fi = (ai * lr - (ar - 1.0) * li) / den
        wb_s[:, 0:S5N] = (fr * bre[...] - fi * bim[...]).astype(BF16)
        wb_s[:, S5N:2 * S5N] = (fr * bim[...] + fi * bre[...]).astype(BF16)
        wc_s[0:S5N, :] = cre[...].astype(BF16)
        wc_s[S5N:2 * S5N, :] = (-cim[...]).astype(BF16)
        tab[0:1, :] = ar
        tab[1:2, :] = ai

    @pl.when(pl.program_id(1) == 0)
    def _():
        for k, out in enumerate(state_out):
            out[...] = jnp.zeros(out.shape, F32) if zero_init else state_in[k][...]

    hd_mask = (_iota((BW, BW), 0) >> _log2(HD)) == (_iota((BW, BW), 1) >> _log2(HD))
    bd = functools.partial(_bd, mask=hd_mask)

    h0 = HIST - (CONV_W - 1)

    conv_src = {C_AX: ('rg', 0), C_BX: ('ssd', 0), C_BBC: ('ssd', BW), C_Q: ('gdn', 0), C_K: ('gdn', BW),
                C_V: ('gdn', 2 * BW)}

    def conv(c0):
        branch, p0 = conv_src[c0]
        wts, bias = prm[branch + '_conv_w'], prm[branch + '_conv_b']
        acc = bias[l:l + 1, p0:p0 + BW]
        for tap in range(CONV_W):
            acc = acc + wts[l, tap:tap + 1, p0:p0 + BW] * zext[:, h0 + tap:h0 + tap + lc, c0:c0 + BW]
        return acc.reshape(rows, BW)

    def rest(c0):
        return zr[:, c0:c0 + BW]

    def to_slabs(buf, first, val):
        for k in range(val.shape[1] // LANES):
            piece = val[:, k * LANES:(k + 1) * LANES]
            if pitch == lc:
                buf[first + k] = piece
            else:
                for q in range(nseq):
                    buf[first + k, q * pitch:q * pitch + lc, :] = piece[q * lc:(q + 1) * lc]

    def from_slabs(buf, first, n):
        cols = []
        for k in range(n):
            if pitch == lc:
                cols.append(buf[first + k])
            else:
                cols.append(jnp.concatenate([buf[first + k, q * pitch:q * pitch + lc, :] for q in range(nseq)], axis=0))
        return cols[0] if n == 1 else jnp.concatenate(cols, axis=1)

    def seq_rows(g, t):
        return pl.ds(g * SUBLANES * pitch + t, SUBLANES, stride=pitch)

    ones_bd = hd_mask.astype(BF16)

    def head_sum(v):
        p1, p2 = _split_bf16(v, 2)
        return _dot(p1, ones_bd) + _dot(p2, ones_bd)

    lane = _iota((rows, LANES), 1)

    def head_rep(pair):
        rolled = pltpu.roll(pair, HD, 1)
        return jnp.concatenate([jnp.where(lane < HD, pair, rolled), jnp.where(lane < HD, rolled, pair)], axis=1)


    n_conv_tiles, n_rest_tiles = NCONV // BW, NREST // BW

    xb = x_ref[...].reshape(rows, D_MODEL).astype(BF16)
    rest_order = [R_GATE // BW, R_DU // BW, R_DT // BW, R_BETA // BW, R_DECAY // BW,
                  R_GATE // BW + 1, R_GATE // BW + 2, R_GATE // BW + 3]
    assert sorted(rest_order) == list(range(n_rest_tiles))

    def ahead(n=1):
        for _ in range(n):
            if rest_order:
                k = rest_order.pop(0)
                zr[:, k * BW:(k + 1) * BW] = _dot(xb, w_rest[:, k * BW:(k + 1) * BW])

    def proj_conv_tile(k):
        zext[:, HIST:HIST + lc, k * BW:(k + 1) * BW] = _dot(xb, w_conv[:, k * BW:(k + 1) * BW]).reshape(nseq, lc, BW)

    zext[:, h0:HIST, C_AX:C_AX + BW] = ca_out[...]
    zext[:, h0:HIST, C_BX:C_BX + 2 * BW] = cb_out[...]
    zext[:, h0:HIST, C_Q:C_Q + 3 * BW] = cc_out[...]
    proj_conv_tile(0)
    proj_conv_tile(1)
    xc = conv(C_AX)
    proj_conv_tile(2)
    xs = _silu(conv(C_BX))
    proj_conv_tile(3)
    bc = _silu(conv(C_BBC))
    bm = head_rep(bc[:, 0:LANES])
    cm = head_rep(bc[:, LANES:2 * LANES])
    proj_conv_tile(4)
    qc = _silu(conv(C_Q))
    proj_conv_tile(5)
    kc = _silu(conv(C_K))
    ahead()
    vc = _silu(conv(C_V))
    ca_out[...] = zext[:, lc + h0:lc + HIST, C_AX:C_AX + BW]
    cb_out[...] = zext[:, lc + h0:lc + HIST, C_BX:C_BX + 2 * BW]
    cc_out[...] = zext[:, lc + h0:lc + HIST, C_Q:C_Q + 3 * BW]
    ahead()

    gts = _dot(xc.astype(BF16), wg[...])
    gate_r = _sigmoid(gts[:, 0:BW] + vec('rg_gate_a_b'))
    gate_i = _sigmoid(gts[:, BW:2 * BW] + vec('rg_gate_x_b'))
    log_a = (-RG_C * _softplus(-vec('rg_lambda'))) * gate_r
    a = jnp.exp(log_a)
    to_slabs(rgbuf, 0, a)
    to_slabs(rgbuf, 2, jnp.sqrt(-jnp.tanh(log_a) * (a * a + 1.0)) * (gate_i * xc))
    for g in range(ngroups):
        gs = slice(g * SUBLANES, (g + 1) * SUBLANES)
        h = [rg_out[gs, k * LANES:(k + 1) * LANES] for k in range(2)]
        for t in range(lc):
            idx = seq_rows(g, t)
            for k in range(2):
                h[k] = rgbuf[k, idx, :] * h[k] + rgbuf[2 + k, idx, :]
                rgbuf[2 + k, idx, :] = h[k]
        for k in range(2):
            rg_out[gs, k * LANES:(k + 1) * LANES] = h[k]
    ahead()
    mix[:, 0:BW] = (from_slabs(rgbuf, 2, 2) * _silu(rest(R_GATE))).astype(BF16)

    du = rest(R_DU)
    dub = du.astype(BF16)
    qn = qc * lax.rsqrt(head_sum(qc * qc) + 1e-6) * (HD ** -0.5)
    kn = kc * lax.rsqrt(head_sum(kc * kc) + 1e-6)
    for m in range(2 * S5N // BW):
        to_slabs(s5buf, 2 * m, _dot(dub, wb_s[:, m * BW:(m + 1) * BW]))
    ahead(2)
    dt = _softplus(rest(R_DT) + head_vec('ssd_dt_bias'))
    da = dt * (-jnp.exp(head_vec('ssd_a_log')))
    xdt = xs * dt
    beta = _sigmoid(rest(R_BETA))
    gdec = (-jnp.exp(head_vec('gdn_a_log'))) * _softplus(rest(R_DECAY) + head_vec('gdn_dt_bias'))
    vb = vc * beta

    nsl = S5N // LANES
    a_re = [jnp.broadcast_to(tab[0:1, k * LANES:(k + 1) * LANES], (SUBLANES, LANES)) for k in range(nsl)]
    a_im = [jnp.broadcast_to(tab[1:2, k * LANES:(k + 1) * LANES], (SUBLANES, LANES)) for k in range(nsl)]
    s5_state = {}

    def s5_step(g, t):
        gs = slice(g * SUBLANES, (g + 1) * SUBLANES)
        if t == 0:
            s5_state['r'] = [s5r_out[gs, k * LANES:(k + 1) * LANES] for k in range(nsl)]
            s5_state['i'] = [s5i_out[gs, k * LANES:(k + 1) * LANES] for k in range(nsl)]
        hr, hi = s5_state['r'], s5_state['i']
        idx = seq_rows(g, t)
        for k in range(nsl):
            nr = a_re[k] * hr[k] - a_im[k] * hi[k] + s5buf[k, idx, :]
            ni = a_re[k] * hi[k] + a_im[k] * hr[k] + s5buf[nsl + k, idx, :]
            hr[k], hi[k] = nr, ni
            s5buf[k, idx, :] = nr
            s5buf[nsl + k, idx, :] = ni
        if t == lc - 1:
            for k in range(nsl):
                s5r_out[gs, k * LANES:(k + 1) * LANES] = hr[k]
                s5i_out[gs, k * LANES:(k + 1) * LANES] = hi[k]

    s5_steps = [(g, t) for g in range(ngroups) for t in range(lc)]
    n_parts = 16
    s5_done = [0]

    def s5_part():
        per = -(-len(s5_steps) // n_parts)
        for g, t in s5_steps[s5_done[0]:s5_done[0] + per]:
            s5_step(g, t)
        s5_done[0] += per

    t_c = _iota((CHUNK, BW), 0)
    s_c = _iota((CHUNK, BW), 1) & (HD - 1)
    same_unit = (t_c >> _log2(unit)) == (s_c >> _log2(unit))
    valid_incl = same_unit & (s_c <= t_c)
    valid_strict = same_unit & (s_c < t_c)
    eye_cat = jnp.where(s_c == t_c, 1.0, 0.0)
    lm = lm_ref[...]
    if units == 1:
        lm = lm[0:CHUNK]
    chunks = range(nchunks)
    sls = [slice(c * CHUNK, (c + 1) * CHUNK) for c in chunks]

    strict_f = jnp.where(s_c < t_c, 1.0, 0.0)
    acum, atot, decay, dtot, lmat, eg = [], [], [], [], [], []
    for c in chunks:
        da_c, gd_c = da[sls[c]], gdec[sls[c]]
        cs = _exact_left(lm, jnp.concatenate([da_c, gd_c, da_c * strict_f, gd_c * strict_f], axis=1))
        acum.append(cs[0:CHUNK, 0:BW])
        decay.append(cs[0:CHUNK, BW:2 * BW])
        lmat.append(jnp.where(valid_incl, jnp.exp(jnp.minimum(cs[0:CHUNK, 2 * BW:3 * BW], 0.0)), 0.0))
        eg.append(jnp.where(valid_incl, jnp.exp(jnp.minimum(cs[0:CHUNK, 3 * BW:4 * BW], 0.0)), 0.0))
        if units == 1:
            atot.append(cs[CHUNK - 1:CHUNK, 0:BW])
            dtot.append(cs[CHUNK - 1:CHUNK, BW:2 * BW])
        else:
            atot.append(cs[CHUNK:2 * CHUNK, 0:BW])
            dtot.append(cs[CHUNK:2 * CHUNK, BW:2 * BW])
    s5_part()
    ahead()

    def seq_of(c, u):
        return (c * CHUNK + u * unit) // lc

    def unit_rows(u):
        return slice(u * unit, (u + 1) * unit)

    def only_unit(v, u):
        return v if units == 1 else jnp.where((t_c >> _log2(unit)) == u, v, 0.0)

    def unit_total(tot_c, u):
        r0 = u * unit if units > 1 else 0
        return tot_c[r0:r0 + 1, :]

    def cat_rows(parts):
        return parts[0] if len(parts) == 1 else jnp.concatenate(parts, axis=0)

    cbm = [_dot_nt(cm[sls[c]].astype(BF16), bd(bm[sls[c]])) for c in chunks]
    s5_part()
    y_ssd = [_dot((cbm[c] * lmat[c]).astype(BF16), bd(xdt[sls[c]])) for c in chunks]
    s5_part()
    ahead()
    ht = [[ssd_out[seq_of(c, u)] for u in range(units)] for c in chunks]
    cme = [(cm[sls[c]] * jnp.exp(acum[c])).astype(BF16) for c in chunks]
    y_int = [cat_rows([_dot(cme[c][unit_rows(u)], bd(ht[c][u])) for u in range(units)]) for c in chunks]
    s5_part()
    wbt = [(bm[sls[c]] * jnp.exp(atot[c] - acum[c])).T.astype(BF16) for c in chunks]
    upd = [[_dot(wbt[c], only_unit(xdt[sls[c]], u).astype(BF16)) for u in range(units)] for c in chunks]
    for c in chunks:
        for u in range(units):
            ssd_out[seq_of(c, u)] = jnp.exp(unit_total(atot[c], u)) * ht[c][u] + _diag_blocks(upd[c][u], hd_mask)
    s5_part()
    ahead()

    qk_kk = [_dot_nt(jnp.concatenate([qn[sls[c]], kn[sls[c]]], axis=0).astype(BF16), bd(kn[sls[c]])) for c in chunks]
    s5_part()
    qkg = [qk_kk[c][0:CHUNK] * eg[c] for c in chunks]
    mm = [jnp.where(valid_strict, beta[sls[c]] * qk_kk[c][CHUNK:2 * CHUNK] * eg[c], 0.0) for c in chunks]
    rm = [eye_cat - mm[c] for c in chunks]
    pw = [_dot(mm[c].astype(BF16), bd(mm[c])) for c in chunks]
    s5_part()
    ahead()
    for _step in range(_log2(unit) - 2):
        pr2 = [_dot(jnp.concatenate([pw[c], rm[c]], axis=0).astype(BF16), bd(pw[c])) for c in chunks]
        pw = [pr2[c][0:CHUNK] for c in chunks]
        rm = [rm[c] + pr2[c][CHUNK:2 * CHUNK] for c in chunks]
        s5_part()
    rm = [(rm[c] + _dot(rm[c].astype(BF16), bd(pw[c]))).astype(BF16) for c in chunks]
    s5_part()
    ahead()
    edec = [jnp.exp(decay[c]) for c in chunks]
    value = [_dot(rm[c], bd(vb[sls[c]])) for c in chunks]
    kcum = [_dot(rm[c], bd(kn[sls[c]] * beta[sls[c]] * edec[c])) for c in chunks]
    s5_part()
    sq = [[gdn_out[seq_of(c, u)] for u in range(units)] for c in chunks]
    qdec = [qn[sls[c]] * edec[c] for c in chunks]
    kq = [[_dot(jnp.concatenate([kcum[c][unit_rows(u)], qdec[c][unit_rows(u)]], axis=0).astype(BF16), bd(sq[c][u]))
           for u in range(units)] for c in chunks]
    s5_part()
    ahead()
    wv = [cat_rows([value[c][unit_rows(u)] - kq[c][u][0:unit] for u in range(units)]) for c in chunks]
    o_chunks = [cat_rows([kq[c][u][unit:2 * unit] for u in range(units)]) + _dot(qkg[c].astype(BF16), bd(wv[c]))
                for c in chunks]
    kdec_t = [(kn[sls[c]] * jnp.exp(dtot[c] - decay[c])).T.astype(BF16) for c in chunks]
    upd = [[_dot(kdec_t[c], only_unit(wv[c], u).astype(BF16)) for u in range(units)] for c in chunks]
    for c in chunks:
        for u in range(units):
            gdn_out[seq_of(c, u)] = jnp.exp(unit_total(dtot[c], u)) * sq[c][u] + _diag_blocks(upd[c][u], hd_mask)
    while s5_done[0] < len(s5_steps):
        s5_part()
    ahead(n_conv_tiles + n_rest_tiles)

    y5 = vec('s5_d') * du
    for m in range(S5N // BW):
        y5 = y5 + _dot(from_slabs(s5buf, 2 * m, 2).astype(BF16), wc_s[m * BW:(m + 1) * BW, :])
    yb = jnp.concatenate([y_ssd[c] + y_int[c] for c in chunks], axis=0) + head_vec('ssd_d') * xs
    yb = yb * _silu(rest(R_GATE + BW))
    yb = yb * lax.rsqrt(jnp.mean(yb * yb, axis=-1, keepdims=True) + 1e-6) * vec('ssd_norm_w')
    mix[:, BW:2 * BW] = yb.astype(BF16)
    for m in range(S5N // BW, 2 * S5N // BW):
        y5 = y5 + _dot(from_slabs(s5buf, 2 * m, 2).astype(BF16), wc_s[m * BW:(m + 1) * BW, :])
    o = jnp.concatenate(o_chunks, axis=0)
    gdn_norm = jnp.concatenate([vec('gdn_norm_w')] * NH, axis=1)
    o = o * lax.rsqrt(head_sum(o * o) * (1.0 / HD) + 1e-6) * gdn_norm
    mix[:, 2 * BW:3 * BW] = (o * _silu(rest(R_GATE + 2 * BW))).astype(BF16)
    y5 = 0.5 * y5 * (1.0 + lax.erf(y5 * math.sqrt(0.5)))
    y5 = y5 * _sigmoid(_dot(y5.astype(BF16), glu_w[...]) + vec('s5_glu_b'))
    mix[:, 3 * BW:4 * BW] = (y5 * _silu(rest(R_GATE + 3 * BW))).astype(BF16)

    half = rows // 2
    seq_half = nseq // 2

    def out_proj(hh):
        xh = x_ref[hh * seq_half:(hh + 1) * seq_half].reshape(half, D_MODEL)
        return ALPHA * xh + _dot(mix[hh * half:(hh + 1) * half, :], w_out[...])

    def layer_norm(hh, res):
        mu = jnp.mean(res, axis=-1, keepdims=True)
        rc = res - mu
        var = jnp.mean(rc * rc, axis=-1, keepdims=True)
        y = rc * lax.rsqrt(var + 1e-5) * vec('ln_g') + vec('ln_b')
        y_ref[hh * seq_half:(hh + 1) * seq_half] = y.reshape(seq_half, lc, D_MODEL)

    res0 = out_proj(0)
    res1 = out_proj(1)
    layer_norm(0, res0)
    layer_norm(1, res1)


def _chunk_masks(unit):
    t = np.arange(CHUNK)
    same = (t[:, None] // unit) == (t[None, :] // unit)
    incl = same & (t[None, :] <= t[:, None])
    return jnp.asarray(np.concatenate([incl, same], axis=0), BF16)


def _block_diag(blocks):
    *lead, n, r, c = blocks.shape
    eye = jnp.eye(n, dtype=blocks.dtype)
    out = eye[:, None, :, None] * blocks[..., :, :, None, :]
    return out.reshape(*lead, n * r, n * c)


def _prep_params(w_in, w_out, rg_gate_a_w, rg_gate_x_w, s5_lambda_re, s5_lambda_im, s5_log_dt, s5_b_re, s5_b_im,
                 s5_c_re, s5_c_im, s5_glu_w, **small):
    sizes = (BW, BW, BW + 2 * SSD_GROUPS * HD, NH, BW, 3 * BW, NH, NH, BW, BW, BW)
    (a_x, a_gate, b_xbc, b_dt, b_gate, c_qkv, c_beta, c_decay, c_gate, d_u, d_gate) = np.cumsum((0,) + sizes)[:-1].tolist()
    w_t = jnp.swapaxes(w_in, 1, 2)

    def rows_of(c0, n):
        return w_t[:, c0:c0 + n, :]

    def rep(c0):
        return jnp.repeat(rows_of(c0, NH), HD, axis=1)

    out = dict(small)
    out.update(
        w_conv=jnp.swapaxes(jnp.concatenate([rows_of(a_x, BW), rows_of(b_xbc, 2 * BW), rows_of(c_qkv, 3 * BW)],
                                            axis=1).astype(BF16), 1, 2),
        w_rest=jnp.swapaxes(jnp.concatenate([rows_of(a_gate, BW), rows_of(b_gate, BW), rows_of(c_gate, BW),
                                             rows_of(d_gate, BW), rows_of(d_u, BW), rep(b_dt), rep(c_beta),
                                             rep(c_decay)], axis=1).astype(BF16), 1, 2),
        w_out=w_out.astype(BF16),
        wg=jnp.concatenate([_block_diag(rg_gate_a_w), _block_diag(rg_gate_x_w)], axis=-1).astype(BF16),
        s5v=jnp.stack([s5_lambda_re.reshape(DEPTH, S5N), s5_lambda_im.reshape(DEPTH, S5N),
                       jnp.repeat(s5_log_dt, S5_STATE, axis=-1)], axis=1),
        bre=_block_diag(jnp.swapaxes(s5_b_re, -1, -2)),
        bim=_block_diag(jnp.swapaxes(s5_b_im, -1, -2)),
        cre=_block_diag(jnp.swapaxes(s5_c_re, -1, -2)),
        cim=_block_diag(jnp.swapaxes(s5_c_im, -1, -2)),
        glu_w=s5_glu_w.astype(BF16),
    )
    return [out[k] for k in _PARAM_NAMES]


def _layer_call(l, x, states, params, nseq, lc):
    bsz, seqlen, _ = x.shape
    rows = nseq * lc
    assert bsz % nseq == 0 and seqlen % lc == 0 and rows % CHUNK == 0 and lc % SUBLANES == 0 and nseq % SUBLANES == 0
    assert CHUNK % min(lc, CHUNK) == 0 and lc % min(lc, CHUNK) == 0
    pitch = _scan_pitch(lc)
    grid = (bsz // nseq, seqlen // lc)
    state_shapes = [(CONV_W - 1, BW), (CONV_W - 1, 2 * BW), (CONV_W - 1, 3 * BW), (BW,), (HD, BW), (HD, BW), (S5N,), (S5N,)]
    assert len(state_shapes) == _N_STATES

    def param_spec(name, a):
        if name in _LAYER_BLOCKS:
            return pl.BlockSpec((None,) + a.shape[1:], lambda i, j: (l,) + (0,) * (a.ndim - 1), pipeline_mode=pl.Buffered(1))
        return pl.BlockSpec(a.shape, lambda i, j: (0,) * a.ndim, pipeline_mode=pl.Buffered(1))

    lm = _chunk_masks(min(lc, CHUNK))
    x_spec = pl.BlockSpec((nseq, lc, D_MODEL), lambda i, j: (i, j, 0))
    state_args = [] if states is None else list(states)
    state_specs = [pl.BlockSpec((None, nseq) + s, lambda i, j, n=len(s): (l, i) + (0,) * n) for s in state_shapes]
    in_specs = ([x_spec] + (state_specs if state_args else [])
                + [param_spec(name, a) for name, a in zip(_PARAM_NAMES, params)]
                + [pl.BlockSpec(lm.shape, lambda i, j: (0, 0), pipeline_mode=pl.Buffered(1))])
    out_state_specs = [pl.BlockSpec((nseq,) + s, lambda i, j, n=len(s): (i,) + (0,) * n) for s in state_shapes]
    out_shape = [jax.ShapeDtypeStruct(x.shape, F32)] + [jax.ShapeDtypeStruct((bsz,) + s, F32) for s in state_shapes]
    scratch = [
        pltpu.VMEM((nseq, HIST + lc, NCONV), F32),
        pltpu.VMEM((rows, NREST), F32),
        pltpu.VMEM((rows, 4 * BW), BF16),
        pltpu.VMEM((S5_GROUPS * S5_GROUP, 2 * S5N), BF16),
        pltpu.VMEM((2 * S5N, S5_GROUPS * S5_GROUP), BF16),
        pltpu.VMEM((SUBLANES, S5N), F32),
        pltpu.VMEM((4, nseq * pitch, LANES), F32),
        pltpu.VMEM((2 * S5N // LANES, nseq * pitch, LANES), F32),
    ]
    outs = pl.pallas_call(
        functools.partial(_layer_kernel, l, nseq, lc, not state_args),
        grid=grid,
        in_specs=in_specs,
        out_specs=[x_spec] + out_state_specs,
        out_shape=out_shape,
        scratch_shapes=scratch,
        compiler_params=pltpu.CompilerParams(dimension_semantics=("arbitrary", "arbitrary"),
                                             vmem_limit_bytes=VMEM_LIMIT_BYTES),
    )(x, *state_args, *params, lm)
    return outs[0], outs[1:]


def _states_to_kernel(conv_a, h_a, conv_b, h_b, conv_c, s_c, s5_re, s5_im):
    d, bsz = h_a.shape[0], h_a.shape[1]
    ssd = jnp.transpose(h_b, (0, 1, 4, 2, 3)).reshape(d, bsz, HD, BW)
    gdn = jnp.transpose(s_c, (0, 1, 3, 2, 4)).reshape(d, bsz, HD, BW)
    return [conv_a, conv_b, conv_c, h_a, ssd, gdn, s5_re.reshape(d, bsz, S5N), s5_im.reshape(d, bsz, S5N)]


def _states_from_kernel(per_layer):
    conv_a, conv_b, conv_c, rg, ssd, gdn, s5r, s5i = (jnp.stack(t) for t in zip(*per_layer))
    d, bsz = rg.shape[0], rg.shape[1]
    return (conv_a, rg, conv_b,
            jnp.transpose(ssd.reshape(d, bsz, HD, NH, HD), (0, 1, 3, 4, 2)),
            conv_c,
            jnp.transpose(gdn.reshape(d, bsz, HD, NH, HD), (0, 1, 3, 2, 4)),
            s5r.reshape(d, bsz, S5_GROUPS, S5_STATE), s5i.reshape(d, bsz, S5_GROUPS, S5_STATE))


def kernel(x_prompt, x_sample, cache_rglru_conv, state_rglru, cache_ssd_conv, state_ssd, cache_gdn_conv, state_gdn,
           state_s5_re, state_s5_im, w_in, w_out, ln_g, ln_b, rg_conv_w, rg_conv_b, rg_gate_a_w, rg_gate_a_b,
           rg_gate_x_w, rg_gate_x_b, rg_lambda, ssd_conv_w, ssd_conv_b, ssd_dt_bias, ssd_a_log, ssd_d, ssd_norm_w,
           gdn_conv_w, gdn_conv_b, gdn_dt_bias, gdn_a_log, gdn_norm_w, s5_lambda_re, s5_lambda_im, s5_log_dt,
           s5_b_re, s5_b_im, s5_c_re, s5_c_im, s5_d, s5_glu_w, s5_glu_b):
    params = _prep_params(w_in, w_out, rg_gate_a_w, rg_gate_x_w, s5_lambda_re, s5_lambda_im, s5_log_dt, s5_b_re,
                          s5_b_im, s5_c_re, s5_c_im, s5_glu_w,
                          ln_g=ln_g, ln_b=ln_b, rg_conv_w=rg_conv_w, ssd_conv_w=ssd_conv_w, gdn_conv_w=gdn_conv_w,
                          rg_conv_b=rg_conv_b, ssd_conv_b=ssd_conv_b, gdn_conv_b=gdn_conv_b, rg_gate_a_b=rg_gate_a_b,
                          rg_gate_x_b=rg_gate_x_b, rg_lambda=rg_lambda, ssd_dt_bias=ssd_dt_bias, ssd_a_log=ssd_a_log,
                          ssd_d=ssd_d, ssd_norm_w=ssd_norm_w, gdn_dt_bias=gdn_dt_bias, gdn_a_log=gdn_a_log,
                          gdn_norm_w=gdn_norm_w, s5_d=s5_d, s5_glu_b=s5_glu_b)
    sample_states = _states_to_kernel(cache_rglru_conv, state_rglru, cache_ssd_conv, state_ssd, cache_gdn_conv,
                                      state_gdn, state_s5_re, state_s5_im)
    pb, pl_len = x_prompt.shape[0], x_prompt.shape[1]
    sl = x_sample.shape[1]
    lc_p = ROWS // pb
    assert pl_len % lc_p == 0
    yp, ys = x_prompt, x_sample
    p_new, s_new = [], []
    for l in range(DEPTH):
        yp, st_p = _layer_call(l, yp, None, params, nseq=pb, lc=lc_p)
        ys, st_s = _layer_call(l, ys, sample_states, params, nseq=SAMPLE_SEQS, lc=sl)
        p_new.append(st_p)
        s_new.append(st_s)
    return (yp, ys, *_states_from_kernel(p_new), *_states_from_kernel(s_new))
```

```python
import functools
import math

import numpy as np
import jax
import jax.numpy as jnp
from jax import lax
from jax.experimental import pallas as pl
from jax.experimental.pallas import tpu as pltpu

F32 = jnp.float32
BF16 = jnp.bfloat16

D_MODEL = 1024
DEPTH = 2
BW = 256
HD = 64
NH = BW // HD
SSD_GROUPS = 2
CONV_W = 4
HIST = 8
S5_GROUPS = 16
S5_GROUP = 16
S5_STATE = 64
S5N = S5_GROUPS * S5_STATE
LANES = 128
SUBLANES = 8
RG_C = 8.0
ALPHA = (2.0 * DEPTH) ** 0.25
CHUNK = 64
ROWS = 512
SAMPLE_SEQS = 32
VMEM_LIMIT_BYTES = 60 * 1024 * 1024

NCONV = 6 * BW
C_AX, C_BX, C_BBC, C_Q, C_K, C_V = (k * BW for k in range(6))
NREST = 6 * BW
R_GATE, R_DU, R_HEADS = 0, 4 * BW, 5 * BW
H_DT, H_BETA, H_DECAY = 0, NH, 2 * NH


def _dot(a, b):
    return jnp.dot(a, b, preferred_element_type=F32)


def _dot_nt(a, b):
    return lax.dot_general(a, b, (((1,), (1,)), ((), ())), preferred_element_type=F32)


def _sigmoid(x):
    return 0.5 * jnp.tanh(0.5 * x) + 0.5


def _silu(x):
    hx = 0.5 * x
    return hx * jnp.tanh(hx) + hx


def _softplus(x):
    return jnp.maximum(x, 0.0) + jnp.log1p(jnp.exp(-jnp.abs(x)))


def _split_bf16(x, pieces):
    out = []
    r = x
    for k in range(pieces):
        p = r.astype(BF16)
        out.append(p)
        if k + 1 < pieces:
            r = r - p.astype(F32)
    return out


def _exact_left(mask_bf16, x):
    acc = None
    for p in _split_bf16(x, 3):
        t = _dot(mask_bf16, p)
        acc = t if acc is None else acc + t
    return acc


def _iota(shape, dim):
    return lax.broadcasted_iota(jnp.int32, shape, dim)


def _log2(n):
    k = int(round(math.log2(n)))
    assert 1 << k == n
    return k


def _bd(x, mask):
    return jnp.where(mask, jnp.tile(x, (NH, 1)), 0.0).astype(BF16)


def _diag_blocks(full, mask):
    fm = jnp.where(mask, full, 0.0)
    return fm[0:HD] + fm[HD:2 * HD] + fm[2 * HD:3 * HD] + fm[3 * HD:4 * HD]


def _scan_pitch(lc):
    return lc if (lc // SUBLANES) % 2 == 1 else lc + SUBLANES


_PARAM_NAMES = ('w_conv', 'w_rest', 'w_out', 'wg', 's5v', 'bre', 'bim', 'cre', 'cim', 'glu_w',
                'ln_g', 'ln_b', 'rg_conv_w', 'ssd_conv_w', 'gdn_conv_w', 'rg_conv_b', 'ssd_conv_b', 'gdn_conv_b',
                'rg_gate_a_b', 'rg_gate_x_b', 'rg_lambda', 'ssd_dt_bias', 'ssd_a_log', 'ssd_d', 'ssd_norm_w',
                'gdn_dt_bias', 'gdn_a_log', 'gdn_norm_w', 's5_d', 's5_glu_b')
_LAYER_BLOCKS = frozenset(_PARAM_NAMES[:10])
_N_STATES = 8
S5V_LRE, S5V_LIM, S5V_LOGDT = range(3)


def _layer_kernel(l, nseq, lc, zero_init, *refs):
    n_in = 0 if zero_init else _N_STATES
    x_ref = refs[0]
    state_in = refs[1:1 + n_in]
    prm = dict(zip(_PARAM_NAMES, refs[1 + n_in:1 + n_in + len(_PARAM_NAMES)]))
    rest_refs = refs[1 + n_in + len(_PARAM_NAMES):]
    lm_ref, y_ref = rest_refs[0], rest_refs[1]
    state_out = rest_refs[2:2 + _N_STATES]
    ca_out, cb_out, cc_out, rg_out, ssd_out, gdn_out, s5r_out, s5i_out = state_out
    zext, zr, mix, wb_s, wc_s, tab, rgbuf, s5buf = rest_refs[2 + _N_STATES:]
    w_conv, w_rest, w_out, wg, glu_w = (prm[k] for k in ('w_conv', 'w_rest', 'w_out', 'wg', 'glu_w'))
    s5v, bre, bim, cre, cim = (prm[k] for k in ('s5v', 'bre', 'bim', 'cre', 'cim'))

    rows = nseq * lc
    pitch = _scan_pitch(lc)
    ngroups = nseq // SUBLANES
    unit = min(lc, CHUNK)
    units = CHUNK // unit
    nchunks = rows // CHUNK
    first_call_step = (pl.program_id(0) == 0) & (pl.program_id(1) == 0)

    def vec(name):
        return prm[name][l:l + 1, :]

    def head_vec(name):
        r = vec(name)
        return jnp.concatenate([jnp.broadcast_to(r[:, h:h + 1], (1, HD)) for h in range(NH)], axis=1)

    @pl.when(first_call_step)
    def _():
        lr = s5v[S5V_LRE:S5V_LRE + 1, :]
        li = s5v[S5V_LIM:S5V_LIM + 1, :]
        dt = jnp.exp(s5v[S5V_LOGDT:S5V_LOGDT + 1, :])
        mag = jnp.exp(lr * dt)
        ang = li * dt
        ar = mag * jnp.cos(ang)
        ai = mag * jnp.sin(ang)
        den = lr * lr + li * li
        fr = ((ar - 1.0) * lr + ai * li) / den
        fi = (ai * lr - (ar - 1.0) * li) / den
        wb_s[:, 0:S5N] = (fr * bre[...] - fi * bim[...]).astype(BF16)
        wb_s[:, S5N:2 * S5N] = (fr * bim[...] + fi * bre[...]).astype(BF16)
        wc_s[0:S5N, :] = cre[...].astype(BF16)
        wc_s[S5N:2 * S5N, :] = (-cim[...]).astype(BF16)
        tab[0:1, :] = ar
        tab[1:2, :] = ai

    @pl.when(pl.program_id(1) == 0)
    def _():
        for k, out in enumerate(state_out):
            out[...] = jnp.zeros(out.shape, F32) if zero_init else state_in[k][...]

    hd_mask = (_iota((BW, BW), 0) >> _log2(HD)) == (_iota((BW, BW), 1) >> _log2(HD))
    bd = functools.partial(_bd, mask=hd_mask)

    h0 = HIST - (CONV_W - 1)

    conv_src = {C_AX: ('rg', 0), C_BX: ('ssd', 0), C_BBC: ('ssd', BW), C_Q: ('gdn', 0), C_K: ('gdn', BW),
                C_V: ('gdn', 2 * BW)}

    def conv(c0):
        branch, p0 = conv_src[c0]
        wts, bias = prm[branch + '_conv_w'], prm[branch + '_conv_b']
        ext = zext[:, :, c0:c0 + BW]
        acc = bias[l:l + 1, p0:p0 + BW] + wts[l, CONV_W - 1:CONV_W, p0:p0 + BW] * ext[:, HIST:, :]
        for back in range(1, CONV_W):
            tap = CONV_W - 1 - back
            acc = acc + wts[l, tap:tap + 1, p0:p0 + BW] * pltpu.roll(ext, back, 1)[:, HIST:, :]
        return acc.reshape(rows, BW)

    def rest(c0):
        return zr[:, c0:c0 + BW]

    def to_slabs(buf, first, val):
        for k in range(val.shape[1] // LANES):
            piece = val[:, k * LANES:(k + 1) * LANES]
            if pitch == lc:
                buf[first + k] = piece
            else:
                for q in range(nseq):
                    buf[first + k, q * pitch:q * pitch + lc, :] = piece[q * lc:(q + 1) * lc]

    def from_slabs(buf, first, n):
        cols = []
        for k in range(n):
            if pitch == lc:
                cols.append(buf[first + k])
            else:
                cols.append(jnp.concatenate([buf[first + k, q * pitch:q * pitch + lc, :] for q in range(nseq)], axis=0))
        return cols[0] if n == 1 else jnp.concatenate(cols, axis=1)

    def seq_rows(g, t):
        return pl.ds(g * SUBLANES * pitch + t, SUBLANES, stride=pitch)

    ones_bd = hd_mask.astype(BF16)

    def head_sum(v):
        return _dot(v.astype(BF16), ones_bd)

    lane = _iota((rows, LANES), 1)

    def head_rep(pair):
        rolled = pltpu.roll(pair, HD, 1)
        return jnp.concatenate([jnp.where(lane < HD, pair, rolled), jnp.where(lane < HD, rolled, pair)], axis=1)


    n_conv_tiles, n_rest_tiles = NCONV // BW, NREST // BW

    xb = x_ref[...].reshape(rows, D_MODEL).astype(BF16)
    rest_order = [R_GATE // BW, R_DU // BW, R_HEADS // BW, R_GATE // BW + 1, R_GATE // BW + 2, R_GATE // BW + 3]
    assert sorted(rest_order) == list(range(n_rest_tiles))

    def ahead(n=1):
        for _ in range(n):
            if rest_order:
                k = rest_order.pop(0)
                zr[:, k * BW:(k + 1) * BW] = _dot(xb, w_rest[:, k * BW:(k + 1) * BW])

    def proj_conv_tile(k):
        zext[:, HIST:HIST + lc, k * BW:(k + 1) * BW] = _dot(xb, w_conv[:, k * BW:(k + 1) * BW]).reshape(nseq, lc, BW)

    zext[:, h0:HIST, C_AX:C_AX + BW] = ca_out[...]
    zext[:, h0:HIST, C_BX:C_BX + 2 * BW] = cb_out[...]
    zext[:, h0:HIST, C_Q:C_Q + 3 * BW] = cc_out[...]
    proj_conv_tile(0)
    proj_conv_tile(1)
    xc = conv(C_AX)
    proj_conv_tile(2)
    xs = _silu(conv(C_BX))
    proj_conv_tile(3)
    bc = _silu(conv(C_BBC))
    bm = head_rep(bc[:, 0:LANES])
    cm = head_rep(bc[:, LANES:2 * LANES])
    proj_conv_tile(4)
    qc = _silu(conv(C_Q))
    proj_conv_tile(5)
    kc = _silu(conv(C_K))
    ahead()
    vc = _silu(conv(C_V))
    ca_out[...] = zext[:, lc + h0:lc + HIST, C_AX:C_AX + BW]
    cb_out[...] = zext[:, lc + h0:lc + HIST, C_BX:C_BX + 2 * BW]
    cc_out[...] = zext[:, lc + h0:lc + HIST, C_Q:C_Q + 3 * BW]
    ahead()

    gts = _dot(xc.astype(BF16), wg[...])
    gate_r = _sigmoid(gts[:, 0:BW] + vec('rg_gate_a_b'))
    gate_i = _sigmoid(gts[:, BW:2 * BW] + vec('rg_gate_x_b'))
    log_a = (-RG_C * _softplus(-vec('rg_lambda'))) * gate_r
    a = jnp.exp(log_a)
    to_slabs(rgbuf, 0, a)
    to_slabs(rgbuf, 2, jnp.sqrt(-jnp.tanh(log_a) * (a * a + 1.0)) * (gate_i * xc))
    for g in range(ngroups):
        gs = slice(g * SUBLANES, (g + 1) * SUBLANES)
        h = [rg_out[gs, k * LANES:(k + 1) * LANES] for k in range(2)]
        for t in range(lc):
            idx = seq_rows(g, t)
            for k in range(2):
                h[k] = rgbuf[k, idx, :] * h[k] + rgbuf[2 + k, idx, :]
                rgbuf[2 + k, idx, :] = h[k]
        for k in range(2):
            rg_out[gs, k * LANES:(k + 1) * LANES] = h[k]
    ahead()
    mix[:, 0:BW] = (from_slabs(rgbuf, 2, 2) * _silu(rest(R_GATE))).astype(BF16)

    du = rest(R_DU)
    dub = du.astype(BF16)
    qn = qc * lax.rsqrt(head_sum(qc * qc) + 1e-6) * (HD ** -0.5)
    kn = kc * lax.rsqrt(head_sum(kc * kc) + 1e-6)
    for m in range(2 * S5N // BW):
        to_slabs(s5buf, 2 * m, _dot(dub, wb_s[:, m * BW:(m + 1) * BW]))
    ahead()
    hl = _iota((1, LANES), 1)

    def head_lanes(name_at):
        out = jnp.zeros((1, LANES), F32)
        for name, off in name_at:
            r = vec(name)
            for h in range(NH):
                out = jnp.where(hl == off + h, r[:, h:h + 1], out)
        return out

    def spread(tile, off):
        return jnp.concatenate([jnp.broadcast_to(tile[:, off + h:off + h + 1], (rows, HD)) for h in range(NH)], axis=1)

    narrow = zr[:, R_HEADS:R_HEADS + LANES]
    sp = _softplus(narrow + head_lanes((('ssd_dt_bias', H_DT), ('gdn_dt_bias', H_DECAY))))
    rate = sp * (-jnp.exp(head_lanes((('ssd_a_log', H_DT), ('gdn_a_log', H_DECAY)))))
    dt = spread(sp, H_DT)
    da = spread(rate, H_DT)
    gdec = spread(rate, H_DECAY)
    beta = spread(_sigmoid(narrow), H_BETA)
    xdt = xs * dt
    vb = vc * beta

    nsl = S5N // LANES
    a_re = [jnp.broadcast_to(tab[0:1, k * LANES:(k + 1) * LANES], (SUBLANES, LANES)) for k in range(nsl)]
    a_im = [jnp.broadcast_to(tab[1:2, k * LANES:(k + 1) * LANES], (SUBLANES, LANES)) for k in range(nsl)]
    s5_state = {}

    def s5_step(g, t):
        gs = slice(g * SUBLANES, (g + 1) * SUBLANES)
        if t == 0:
            s5_state['r'] = [s5r_out[gs, k * LANES:(k + 1) * LANES] for k in range(nsl)]
            s5_state['i'] = [s5i_out[gs, k * LANES:(k + 1) * LANES] for k in range(nsl)]
        hr, hi = s5_state['r'], s5_state['i']
        idx = seq_rows(g, t)
        for k in range(nsl):
            nr = a_re[k] * hr[k] - a_im[k] * hi[k] + s5buf[k, idx, :]
            ni = a_re[k] * hi[k] + a_im[k] * hr[k] + s5buf[nsl + k, idx, :]
            hr[k], hi[k] = nr, ni
            s5buf[k, idx, :] = nr
            s5buf[nsl + k, idx, :] = ni
        if t == lc - 1:
            for k in range(nsl):
                s5r_out[gs, k * LANES:(k + 1) * LANES] = hr[k]
                s5i_out[gs, k * LANES:(k + 1) * LANES] = hi[k]

    s5_steps = [(g, t) for g in range(ngroups) for t in range(lc)]
    n_parts = 16
    s5_done = [0]

    def s5_part():
        per = -(-len(s5_steps) // n_parts)
        for g, t in s5_steps[s5_done[0]:s5_done[0] + per]:
            s5_step(g, t)
        s5_done[0] += per

    t_c = _iota((CHUNK, BW), 0)
    s_c = _iota((CHUNK, BW), 1) & (HD - 1)
    same_unit = (t_c >> _log2(unit)) == (s_c >> _log2(unit))
    valid_incl = same_unit & (s_c <= t_c)
    valid_strict = same_unit & (s_c < t_c)
    eye_cat = jnp.where(s_c == t_c, 1.0, 0.0)
    lm = lm_ref[...]
    if units == 1:
        lm = lm[0:CHUNK]
    chunks = range(nchunks)
    sls = [slice(c * CHUNK, (c + 1) * CHUNK) for c in chunks]

    strict_f = jnp.where(s_c < t_c, 1.0, 0.0)
    acum, atot, decay, dtot, lmat, eg = [], [], [], [], [], []
    for c in chunks:
        da_c, gd_c = da[sls[c]], gdec[sls[c]]
        cs = _exact_left(lm, jnp.concatenate([da_c, gd_c, da_c * strict_f, gd_c * strict_f], axis=1))
        acum.append(cs[0:CHUNK, 0:BW])
        decay.append(cs[0:CHUNK, BW:2 * BW])
        lmat.append(jnp.where(valid_incl, jnp.exp(jnp.minimum(cs[0:CHUNK, 2 * BW:3 * BW], 0.0)), 0.0))
        eg.append(jnp.where(valid_incl, jnp.exp(jnp.minimum(cs[0:CHUNK, 3 * BW:4 * BW], 0.0)), 0.0))
        if units == 1:
            atot.append(cs[CHUNK - 1:CHUNK, 0:BW])
            dtot.append(cs[CHUNK - 1:CHUNK, BW:2 * BW])
        else:
            atot.append(cs[CHUNK:2 * CHUNK, 0:BW])
            dtot.append(cs[CHUNK:2 * CHUNK, BW:2 * BW])
    s5_part()
    ahead()

    def seq_of(c, u):
        return (c * CHUNK + u * unit) // lc

    def unit_rows(u):
        return slice(u * unit, (u + 1) * unit)

    def only_unit(v, u):
        return v if units == 1 else jnp.where((t_c >> _log2(unit)) == u, v, 0.0)

    def unit_total(tot_c, u):
        r0 = u * unit if units > 1 else 0
        return tot_c[r0:r0 + 1, :]

    def cat_rows(parts):
        return parts[0] if len(parts) == 1 else jnp.concatenate(parts, axis=0)

    cbm = [_dot_nt(cm[sls[c]].astype(BF16), bd(bm[sls[c]])) for c in chunks]
    s5_part()
    y_ssd = [_dot((cbm[c] * lmat[c]).astype(BF16), bd(xdt[sls[c]])) for c in chunks]
    s5_part()
    ahead()
    ht = [[ssd_out[seq_of(c, u)] for u in range(units)] for c in chunks]
    cme = [(cm[sls[c]] * jnp.exp(acum[c])).astype(BF16) for c in chunks]
    y_int = [cat_rows([_dot(cme[c][unit_rows(u)], bd(ht[c][u])) for u in range(units)]) for c in chunks]
    s5_part()
    wbt = [(bm[sls[c]] * jnp.exp(atot[c] - acum[c])).T.astype(BF16) for c in chunks]
    upd = [[_dot(wbt[c], only_unit(xdt[sls[c]], u).astype(BF16)) for u in range(units)] for c in chunks]
    for c in chunks:
        for u in range(units):
            ssd_out[seq_of(c, u)] = jnp.exp(unit_total(atot[c], u)) * ht[c][u] + _diag_blocks(upd[c][u], hd_mask)
    s5_part()
    ahead()

    qk_kk = [_dot_nt(jnp.concatenate([qn[sls[c]], kn[sls[c]]], axis=0).astype(BF16), bd(kn[sls[c]])) for c in chunks]
    s5_part()
    qkg = [qk_kk[c][0:CHUNK] * eg[c] for c in chunks]
    mm = [jnp.where(valid_strict, beta[sls[c]] * qk_kk[c][CHUNK:2 * CHUNK] * eg[c], 0.0) for c in chunks]
    rm = [eye_cat - mm[c] for c in chunks]
    pw = [_dot(mm[c].astype(BF16), bd(mm[c])) for c in chunks]
    s5_part()
    ahead()
    for _step in range(_log2(unit) - 2):
        pr2 = [_dot(jnp.concatenate([pw[c], rm[c]], axis=0).astype(BF16), bd(pw[c])) for c in chunks]
        pw = [pr2[c][0:CHUNK] for c in chunks]
        rm = [rm[c] + pr2[c][CHUNK:2 * CHUNK] for c in chunks]
        s5_part()
    rm = [(rm[c] + _dot(rm[c].astype(BF16), bd(pw[c]))).astype(BF16) for c in chunks]
    s5_part()
    ahead()
    edec = [jnp.exp(decay[c]) for c in chunks]
    value = [_dot(rm[c], bd(vb[sls[c]])) for c in chunks]
    kcum = [_dot(rm[c], bd(kn[sls[c]] * beta[sls[c]] * edec[c])) for c in chunks]
    s5_part()
    sq = [[gdn_out[seq_of(c, u)] for u in range(units)] for c in chunks]
    qdec = [qn[sls[c]] * edec[c] for c in chunks]
    kq = [[_dot(jnp.concatenate([kcum[c][unit_rows(u)], qdec[c][unit_rows(u)]], axis=0).astype(BF16), bd(sq[c][u]))
           for u in range(units)] for c in chunks]
    s5_part()
    ahead()
    wv = [cat_rows([value[c][unit_rows(u)] - kq[c][u][0:unit] for u in range(units)]) for c in chunks]
    o_chunks = [cat_rows([kq[c][u][unit:2 * unit] for u in range(units)]) + _dot(qkg[c].astype(BF16), bd(wv[c]))
                for c in chunks]
    kdec_t = [(kn[sls[c]] * jnp.exp(dtot[c] - decay[c])).T.astype(BF16) for c in chunks]
    upd = [[_dot(kdec_t[c], only_unit(wv[c], u).astype(BF16)) for u in range(units)] for c in chunks]
    for c in chunks:
        for u in range(units):
            gdn_out[seq_of(c, u)] = jnp.exp(unit_total(dtot[c], u)) * sq[c][u] + _diag_blocks(upd[c][u], hd_mask)
    while s5_done[0] < len(s5_steps):
        s5_part()
    ahead(n_conv_tiles + n_rest_tiles)

    y5 = vec('s5_d') * du
    for m in range(S5N // BW):
        y5 = y5 + _dot(from_slabs(s5buf, 2 * m, 2).astype(BF16), wc_s[m * BW:(m + 1) * BW, :])
    yb = jnp.concatenate([y_ssd[c] + y_int[c] for c in chunks], axis=0) + head_vec('ssd_d') * xs
    yb = yb * _silu(rest(R_GATE + BW))
    yb = yb * lax.rsqrt(jnp.mean(yb * yb, axis=-1, keepdims=True) + 1e-6) * vec('ssd_norm_w')
    mix[:, BW:2 * BW] = yb.astype(BF16)
    for m in range(S5N // BW, 2 * S5N // BW):
        y5 = y5 + _dot(from_slabs(s5buf, 2 * m, 2).astype(BF16), wc_s[m * BW:(m + 1) * BW, :])
    o = jnp.concatenate(o_chunks, axis=0)
    gdn_norm = jnp.concatenate([vec('gdn_norm_w')] * NH, axis=1)
    o = o * lax.rsqrt(head_sum(o * o) * (1.0 / HD) + 1e-6) * gdn_norm
    mix[:, 2 * BW:3 * BW] = (o * _silu(rest(R_GATE + 2 * BW))).astype(BF16)
    y5 = 0.5 * y5 * (1.0 + lax.erf(y5 * math.sqrt(0.5)))
    y5 = y5 * _sigmoid(_dot(y5.astype(BF16), glu_w[...]) + vec('s5_glu_b'))
    mix[:, 3 * BW:4 * BW] = (y5 * _silu(rest(R_GATE + 3 * BW))).astype(BF16)

    half = rows // 2
    seq_half = nseq // 2

    def out_proj(hh):
        xh = x_ref[hh * seq_half:(hh + 1) * seq_half].reshape(half, D_MODEL)
        return ALPHA * xh + _dot(mix[hh * half:(hh + 1) * half, :], w_out[...])

    def layer_norm(hh, res):
        mu = jnp.mean(res, axis=-1, keepdims=True)
        rc = res - mu
        var = jnp.mean(rc * rc, axis=-1, keepdims=True)
        y = rc * lax.rsqrt(var + 1e-5) * vec('ln_g') + vec('ln_b')
        y_ref[hh * seq_half:(hh + 1) * seq_half] = y.reshape(seq_half, lc, D_MODEL)

    res0 = out_proj(0)
    res1 = out_proj(1)
    layer_norm(0, res0)
    layer_norm(1, res1)


def _chunk_masks(unit):
    t = np.arange(CHUNK)
    same = (t[:, None] // unit) == (t[None, :] // unit)
    incl = same & (t[None, :] <= t[:, None])
    return jnp.asarray(np.concatenate([incl, same], axis=0), BF16)


def _block_diag(blocks):
    *lead, n, r, c = blocks.shape
    eye = jnp.eye(n, dtype=blocks.dtype)
    out = eye[:, None, :, None] * blocks[..., :, :, None, :]
    return out.reshape(*lead, n * r, n * c)


def _prep_params(w_in, w_out, rg_gate_a_w, rg_gate_x_w, s5_lambda_re, s5_lambda_im, s5_log_dt, s5_b_re, s5_b_im,
                 s5_c_re, s5_c_im, s5_glu_w, **small):
    sizes = (BW, BW, BW + 2 * SSD_GROUPS * HD, NH, BW, 3 * BW, NH, NH, BW, BW, BW)
    (a_x, a_gate, b_xbc, b_dt, b_gate, c_qkv, c_beta, c_decay, c_gate, d_u, d_gate) = np.cumsum((0,) + sizes)[:-1].tolist()
    assert (H_DT, H_BETA, H_DECAY) == (0, NH, 2 * NH)

    def cols(c0, n):
        return w_in[:, :, c0:c0 + n]

    out = dict(small)
    out.update(
        w_conv=jnp.concatenate([cols(a_x, BW), cols(b_xbc, 2 * BW), cols(c_qkv, 3 * BW)], axis=-1).astype(BF16),
        w_rest=jnp.concatenate([cols(a_gate, BW), cols(b_gate, BW), cols(c_gate, BW), cols(d_gate, BW), cols(d_u, BW),
                                cols(b_dt, NH), cols(c_beta, NH), cols(c_decay, NH),
                                jnp.zeros((DEPTH, D_MODEL, BW - 3 * NH), F32)], axis=-1).astype(BF16),
        w_out=w_out.astype(BF16),
        wg=jnp.concatenate([_block_diag(rg_gate_a_w), _block_diag(rg_gate_x_w)], axis=-1).astype(BF16),
        s5v=jnp.stack([s5_lambda_re.reshape(DEPTH, S5N), s5_lambda_im.reshape(DEPTH, S5N),
                       jnp.repeat(s5_log_dt, S5_STATE, axis=-1)], axis=1),
        bre=_block_diag(jnp.swapaxes(s5_b_re, -1, -2)),
        bim=_block_diag(jnp.swapaxes(s5_b_im, -1, -2)),
        cre=_block_diag(jnp.swapaxes(s5_c_re, -1, -2)),
        cim=_block_diag(jnp.swapaxes(s5_c_im, -1, -2)),
        glu_w=s5_glu_w.astype(BF16),
    )
    return [out[k] for k in _PARAM_NAMES]


def _layer_call(l, x, states, params, nseq, lc):
    bsz, seqlen, _ = x.shape
    rows = nseq * lc
    assert bsz % nseq == 0 and seqlen % lc == 0 and rows % CHUNK == 0 and lc % SUBLANES == 0 and nseq % SUBLANES == 0
    assert CHUNK % min(lc, CHUNK) == 0 and lc % min(lc, CHUNK) == 0
    pitch = _scan_pitch(lc)
    grid = (bsz // nseq, seqlen // lc)
    state_shapes = [(CONV_W - 1, BW), (CONV_W - 1, 2 * BW), (CONV_W - 1, 3 * BW), (BW,), (HD, BW), (HD, BW), (S5N,), (S5N,)]
    assert len(state_shapes) == _N_STATES

    def param_spec(name, a):
        if name in _LAYER_BLOCKS:
            return pl.BlockSpec((None,) + a.shape[1:], lambda i, j: (l,) + (0,) * (a.ndim - 1), pipeline_mode=pl.Buffered(1))
        return pl.BlockSpec(a.shape, lambda i, j: (0,) * a.ndim, pipeline_mode=pl.Buffered(1))

    lm = _chunk_masks(min(lc, CHUNK))
    x_spec = pl.BlockSpec((nseq, lc, D_MODEL), lambda i, j: (i, j, 0))
    state_args = [] if states is None else list(states)
    state_specs = [pl.BlockSpec((None, nseq) + s, lambda i, j, n=len(s): (l, i) + (0,) * n) for s in state_shapes]
    in_specs = ([x_spec] + (state_specs if state_args else [])
                + [param_spec(name, a) for name, a in zip(_PARAM_NAMES, params)]
                + [pl.BlockSpec(lm.shape, lambda i, j: (0, 0), pipeline_mode=pl.Buffered(1))])
    out_state_specs = [pl.BlockSpec((nseq,) + s, lambda i, j, n=len(s): (i,) + (0,) * n) for s in state_shapes]
    out_shape = [jax.ShapeDtypeStruct(x.shape, F32)] + [jax.ShapeDtypeStruct((bsz,) + s, F32) for s in state_shapes]
    scratch = [
        pltpu.VMEM((nseq, HIST + lc, NCONV), F32),
        pltpu.VMEM((rows, NREST), F32),
        pltpu.VMEM((rows, 4 * BW), BF16),
        pltpu.VMEM((S5_GROUPS * S5_GROUP, 2 * S5N), BF16),
        pltpu.VMEM((2 * S5N, S5_GROUPS * S5_GROUP), BF16),
        pltpu.VMEM((SUBLANES, S5N), F32),
        pltpu.VMEM((4, nseq * pitch, LANES), F32),
        pltpu.VMEM((2 * S5N // LANES, nseq * pitch, LANES), F32),
    ]
    outs = pl.pallas_call(
        functools.partial(_layer_kernel, l, nseq, lc, not state_args),
        grid=grid,
        in_specs=in_specs,
        out_specs=[x_spec] + out_state_specs,
        out_shape=out_shape,
        scratch_shapes=scratch,
        compiler_params=pltpu.CompilerParams(dimension_semantics=("arbitrary", "arbitrary"),
                                             vmem_limit_bytes=VMEM_LIMIT_BYTES),
    )(x, *state_args, *params, lm)
    return outs[0], outs[1:]


def _states_to_kernel(conv_a, h_a, conv_b, h_b, conv_c, s_c, s5_re, s5_im):
    d, bsz = h_a.shape[0], h_a.shape[1]
    ssd = jnp.transpose(h_b, (0, 1, 4, 2, 3)).reshape(d, bsz, HD, BW)
    gdn = jnp.transpose(s_c, (0, 1, 3, 2, 4)).reshape(d, bsz, HD, BW)
    return [conv_a, conv_b, conv_c, h_a, ssd, gdn, s5_re.reshape(d, bsz, S5N), s5_im.reshape(d, bsz, S5N)]


def _states_from_kernel(per_layer):
    conv_a, conv_b, conv_c, rg, ssd, gdn, s5r, s5i = (jnp.stack(t) for t in zip(*per_layer))
    d, bsz = rg.shape[0], rg.shape[1]
    return (conv_a, rg, conv_b,
            jnp.transpose(ssd.reshape(d, bsz, HD, NH, HD), (0, 1, 3, 4, 2)),
            conv_c,
            jnp.transpose(gdn.reshape(d, bsz, HD, NH, HD), (0, 1, 3, 2, 4)),
            s5r.reshape(d, bsz, S5_GROUPS, S5_STATE), s5i.reshape(d, bsz, S5_GROUPS, S5_STATE))


def kernel(x_prompt, x_sample, cache_rglru_conv, state_rglru, cache_ssd_conv, state_ssd, cache_gdn_conv, state_gdn,
           state_s5_re, state_s5_im, w_in, w_out, ln_g, ln_b, rg_conv_w, rg_conv_b, rg_gate_a_w, rg_gate_a_b,
           rg_gate_x_w, rg_gate_x_b, rg_lambda, ssd_conv_w, ssd_conv_b, ssd_dt_bias, ssd_a_log, ssd_d, ssd_norm_w,
           gdn_conv_w, gdn_conv_b, gdn_dt_bias, gdn_a_log, gdn_norm_w, s5_lambda_re, s5_lambda_im, s5_log_dt,
           s5_b_re, s5_b_im, s5_c_re, s5_c_im, s5_d, s5_glu_w, s5_glu_b):
    params = _prep_params(w_in, w_out, rg_gate_a_w, rg_gate_x_w, s5_lambda_re, s5_lambda_im, s5_log_dt, s5_b_re,
                          s5_b_im, s5_c_re, s5_c_im, s5_glu_w,
                          ln_g=ln_g, ln_b=ln_b, rg_conv_w=rg_conv_w, ssd_conv_w=ssd_conv_w, gdn_conv_w=gdn_conv_w,
                          rg_conv_b=rg_conv_b, ssd_conv_b=ssd_conv_b, gdn_conv_b=gdn_conv_b, rg_gate_a_b=rg_gate_a_b,
                          rg_gate_x_b=rg_gate_x_b, rg_lambda=rg_lambda, ssd_dt_bias=ssd_dt_bias, ssd_a_log=ssd_a_log,
                          ssd_d=ssd_d, ssd_norm_w=ssd_norm_w, gdn_dt_bias=gdn_dt_bias, gdn_a_log=gdn_a_log,
                          gdn_norm_w=gdn_norm_w, s5_d=s5_d, s5_glu_b=s5_glu_b)
    sample_states = _states_to_kernel(cache_rglru_conv, state_rglru, cache_ssd_conv, state_ssd, cache_gdn_conv,
                                      state_gdn, state_s5_re, state_s5_im)
    pb, pl_len = x_prompt.shape[0], x_prompt.shape[1]
    sl = x_sample.shape[1]
    lc_p = ROWS // pb
    assert pl_len % lc_p == 0
    yp, ys = x_prompt, x_sample
    p_new, s_new = [], []
    for l in range(DEPTH):
        yp, st_p = _layer_call(l, yp, None, params, nseq=pb, lc=lc_p)
        ys, st_s = _layer_call(l, ys, sample_states, params, nseq=SAMPLE_SEQS, lc=sl)
        p_new.append(st_p)
        s_new.append(st_s)
    return (yp, ys, *_states_from_kernel(p_new), *_states_from_kernel(s_new))
```

```python
import functools
import math

import numpy as np
import jax
import jax.numpy as jnp
from jax import lax
from jax.experimental import pallas as pl
from jax.experimental.pallas import tpu as pltpu

F32 = jnp.float32
BF16 = jnp.bfloat16

D_MODEL = 1024
DEPTH = 2
BW = 256
HD = 64
NH = BW // HD
SSD_GROUPS = 2
CONV_W = 4
HIST = 8
S5_GROUPS = 16
S5_GROUP = 16
S5_STATE = 64
S5N = S5_GROUPS * S5_STATE
LANES = 128
SUBLANES = 8
RG_C = 8.0
ALPHA = (2.0 * DEPTH) ** 0.25
CHUNK = 64
ROWS = 512
SAMPLE_SEQS = 32
VMEM_LIMIT_BYTES = 60 * 1024 * 1024

NCONV = 6 * BW
C_AX, C_BX, C_BBC, C_Q, C_K, C_V = (k * BW for k in range(6))
NREST = 6 * BW
R_GATE, R_DU, R_HEADS = 0, 4 * BW, 5 * BW
H_DT, H_BETA, H_DECAY = 0, NH, 2 * NH


def _dot(a, b):
    return jnp.dot(a, b, preferred_element_type=F32)


def _dot_nt(a, b):
    return lax.dot_general(a, b, (((1,), (1,)), ((), ())), preferred_element_type=F32)


def _sigmoid(x):
    return 0.5 * jnp.tanh(0.5 * x) + 0.5


def _silu(x):
    hx = 0.5 * x
    return hx * jnp.tanh(hx) + hx


def _softplus(x):
    return jnp.maximum(x, 0.0) + jnp.log1p(jnp.exp(-jnp.abs(x)))


def _split_bf16(x, pieces):
    out = []
    r = x
    for k in range(pieces):
        p = r.astype(BF16)
        out.append(p)
        if k + 1 < pieces:
            r = r - p.astype(F32)
    return out


def _exact_left(mask_bf16, x):
    acc = None
    for p in _split_bf16(x, 3):
        t = _dot(mask_bf16, p)
        acc = t if acc is None else acc + t
    return acc


def _iota(shape, dim):
    return lax.broadcasted_iota(jnp.int32, shape, dim)


def _log2(n):
    k = int(round(math.log2(n)))
    assert 1 << k == n
    return k


def _bd(x, mask):
    return jnp.where(mask, jnp.tile(x, (NH, 1)), 0.0).astype(BF16)


def _diag_blocks(full, mask):
    fm = jnp.where(mask, full, 0.0)
    return fm[0:HD] + fm[HD:2 * HD] + fm[2 * HD:3 * HD] + fm[3 * HD:4 * HD]


def _scan_pitch(lc):
    return lc if (lc // SUBLANES) % 2 == 1 else lc + SUBLANES


_PARAM_NAMES = ('w_conv', 'w_rest', 'w_out', 'wg', 's5v', 'bre', 'bim', 'cre', 'cim', 'glu_w',
                'ln_g', 'ln_b', 'rg_conv_w', 'ssd_conv_w', 'gdn_conv_w', 'rg_conv_b', 'ssd_conv_b', 'gdn_conv_b',
                'rg_gate_a_b', 'rg_gate_x_b', 'rg_lambda', 'ssd_dt_bias', 'ssd_a_log', 'ssd_d', 'ssd_norm_w',
                'gdn_dt_bias', 'gdn_a_log', 'gdn_norm_w', 's5_d', 's5_glu_b')
_LAYER_BLOCKS = frozenset(_PARAM_NAMES[:10])
_N_STATES = 8
S5V_LRE, S5V_LIM, S5V_LOGDT = range(3)


def _layer_kernel(l, nseq, lc, zero_init, *refs):
    n_in = 0 if zero_init else _N_STATES
    x_ref = refs[0]
    state_in = refs[1:1 + n_in]
    prm = dict(zip(_PARAM_NAMES, refs[1 + n_in:1 + n_in + len(_PARAM_NAMES)]))
    rest_refs = refs[1 + n_in + len(_PARAM_NAMES):]
    lm_ref, y_ref = rest_refs[0], rest_refs[1]
    state_out = rest_refs[2:2 + _N_STATES]
    ca_out, cb_out, cc_out, rg_out, ssd_out, gdn_out, s5r_out, s5i_out = state_out
    zext, zr, mix, wb_s, wc_s, tab, rgbuf, s5buf, dubuf, ybuf = rest_refs[2 + _N_STATES:]
    w_conv, w_rest, w_out, wg, glu_w = (prm[k] for k in ('w_conv', 'w_rest', 'w_out', 'wg', 'glu_w'))
    s5v, bre, bim, cre, cim = (prm[k] for k in ('s5v', 'bre', 'bim', 'cre', 'cim'))

    rows = nseq * lc
    pitch = _scan_pitch(lc)
    ngroups = nseq // SUBLANES
    unit = min(lc, CHUNK)
    units = CHUNK // unit
    nchunks = rows // CHUNK
    first_call_step = (pl.program_id(0) == 0) & (pl.program_id(1) == 0)

    def vec(name):
        return prm[name][l:l + 1, :]

    def head_vec(name):
        r = vec(name)
        return jnp.concatenate([jnp.broadcast_to(r[:, h:h + 1], (1, HD)) for h in range(NH)], axis=1)

    @pl.when(first_call_step)
    def _():
        lr = s5v[S5V_LRE:S5V_LRE + 1, :]
        li = s5v[S5V_LIM:S5V_LIM + 1, :]
        dt = jnp.exp(s5v[S5V_LOGDT:S5V_LOGDT + 1, :])
        mag = jnp.exp(lr * dt)
        ang = li * dt
        ar = mag * jnp.cos(ang)
        ai = mag * jnp.sin(ang)
        den = lr * lr + li * li
        fr = ((ar - 1.0) * lr + ai * li) / den
        fi = (ai * lr - (ar - 1.0) * li) / den
        wb_s[:, 0:S5N] = (fr * bre[...] - fi * bim[...]).astype(BF16)
        wb_s[:, S5N:2 * S5N] = (fr * bim[...] + fi * bre[...]).astype(BF16)
        wc_s[0:S5N, :] = cre[...].astype(BF16)
        wc_s[S5N:2 * S5N, :] = (-cim[...]).astype(BF16)
        tab[0:1, :] = ar
        tab[1:2, :] = ai

    @pl.when(pl.program_id(1) == 0)
    def _():
        for k, out in enumerate(state_out):
            out[...] = jnp.zeros(out.shape, F32) if zero_init else state_in[k][...]

    hd_mask = (_iota((BW, BW), 0) >> _log2(HD)) == (_iota((BW, BW), 1) >> _log2(HD))
    bd = functools.partial(_bd, mask=hd_mask)

    h0 = HIST - (CONV_W - 1)

    conv_src = {C_AX: ('rg', 0), C_BX: ('ssd', 0), C_BBC: ('ssd', BW), C_Q: ('gdn', 0), C_K: ('gdn', BW),
                C_V: ('gdn', 2 * BW)}

    def conv(c0):
        branch, p0 = conv_src[c0]
        wts, bias = prm[branch + '_conv_w'], prm[branch + '_conv_b']
        ext = zext[:, :, c0:c0 + BW]
        acc = bias[l:l + 1, p0:p0 + BW] + wts[l, CONV_W - 1:CONV_W, p0:p0 + BW] * ext[:, HIST:, :]
        for back in range(1, CONV_W):
            tap = CONV_W - 1 - back
            acc = acc + wts[l, tap:tap + 1, p0:p0 + BW] * pltpu.roll(ext, back, 1)[:, HIST:, :]
        return acc.reshape(rows, BW)

    def rest(c0):
        return zr[:, c0:c0 + BW]

    def to_slabs(buf, first, val):
        for k in range(val.shape[1] // LANES):
            piece = val[:, k * LANES:(k + 1) * LANES]
            if pitch == lc:
                buf[first + k] = piece
            else:
                for q in range(nseq):
                    buf[first + k, q * pitch:q * pitch + lc, :] = piece[q * lc:(q + 1) * lc]

    def from_slabs(buf, first, n):
        cols = []
        for k in range(n):
            if pitch == lc:
                cols.append(buf[first + k])
            else:
                cols.append(jnp.concatenate([buf[first + k, q * pitch:q * pitch + lc, :] for q in range(nseq)], axis=0))
        return cols[0] if n == 1 else jnp.concatenate(cols, axis=1)

    def seq_rows(g, t):
        return pl.ds(g * SUBLANES * pitch + t, SUBLANES, stride=pitch)

    ones_bd = hd_mask.astype(BF16)

    def head_sum(v):
        return _dot(v.astype(BF16), ones_bd)

    lane = _iota((rows, LANES), 1)

    def head_rep(pair):
        rolled = pltpu.roll(pair, HD, 1)
        return jnp.concatenate([jnp.where(lane < HD, pair, rolled), jnp.where(lane < HD, rolled, pair)], axis=1)


    n_conv_tiles, n_rest_tiles = NCONV // BW, NREST // BW

    xb = x_ref[...].reshape(rows, D_MODEL).astype(BF16)
    rest_order = [R_GATE // BW, R_DU // BW, R_HEADS // BW, R_GATE // BW + 1, R_GATE // BW + 2, R_GATE // BW + 3]
    assert sorted(rest_order) == list(range(n_rest_tiles))

    def ahead(n=1):
        for _ in range(n):
            if rest_order:
                k = rest_order.pop(0)
                tile = _dot(xb, w_rest[:, k * BW:(k + 1) * BW])
                if k == R_DU // BW:
                    to_slabs(dubuf, 0, tile)
                else:
                    zr[:, k * BW:(k + 1) * BW] = tile

    def proj_conv_tile(k):
        zext[:, HIST:HIST + lc, k * BW:(k + 1) * BW] = _dot(xb, w_conv[:, k * BW:(k + 1) * BW]).reshape(nseq, lc, BW)

    zext[:, h0:HIST, C_AX:C_AX + BW] = ca_out[...]
    zext[:, h0:HIST, C_BX:C_BX + 2 * BW] = cb_out[...]
    zext[:, h0:HIST, C_Q:C_Q + 3 * BW] = cc_out[...]
    proj_conv_tile(0)
    proj_conv_tile(1)
    xc = conv(C_AX)
    proj_conv_tile(2)
    xs = _silu(conv(C_BX))
    proj_conv_tile(3)
    bc = _silu(conv(C_BBC))
    bm = head_rep(bc[:, 0:LANES])
    cm = head_rep(bc[:, LANES:2 * LANES])
    proj_conv_tile(4)
    qc = _silu(conv(C_Q))
    proj_conv_tile(5)
    kc = _silu(conv(C_K))
    ahead()
    vc = _silu(conv(C_V))
    ca_out[...] = zext[:, lc + h0:lc + HIST, C_AX:C_AX + BW]
    cb_out[...] = zext[:, lc + h0:lc + HIST, C_BX:C_BX + 2 * BW]
    cc_out[...] = zext[:, lc + h0:lc + HIST, C_Q:C_Q + 3 * BW]
    ahead()

    gts = _dot(xc.astype(BF16), wg[...])
    gate_r = _sigmoid(gts[:, 0:BW] + vec('rg_gate_a_b'))
    gate_i = _sigmoid(gts[:, BW:2 * BW] + vec('rg_gate_x_b'))
    log_a = (-RG_C * _softplus(-vec('rg_lambda'))) * gate_r
    a = jnp.exp(log_a)
    to_slabs(rgbuf, 0, a)
    to_slabs(rgbuf, 2, jnp.sqrt(-jnp.tanh(log_a) * (a * a + 1.0)) * (gate_i * xc))
    for g in range(ngroups):
        gs = slice(g * SUBLANES, (g + 1) * SUBLANES)
        h = [rg_out[gs, k * LANES:(k + 1) * LANES] for k in range(2)]
        for t in range(lc):
            idx = seq_rows(g, t)
            for k in range(2):
                h[k] = rgbuf[k, idx, :] * h[k] + rgbuf[2 + k, idx, :]
                rgbuf[2 + k, idx, :] = h[k]
        for k in range(2):
            rg_out[gs, k * LANES:(k + 1) * LANES] = h[k]
    ahead()
    mix[:, 0:BW] = (from_slabs(rgbuf, 2, 2) * _silu(rest(R_GATE))).astype(BF16)

    du = jnp.concatenate([jnp.concatenate([dubuf[k, seq_rows(g, t), :] for k in range(BW // LANES)], axis=1)
                          for g in range(ngroups) for t in range(lc)], axis=0)
    dub = du.astype(BF16)
    qn = qc * lax.rsqrt(head_sum(qc * qc) + 1e-6) * (HD ** -0.5)
    kn = kc * lax.rsqrt(head_sum(kc * kc) + 1e-6)
    for m in range(2 * S5N // BW):
        bu = _dot(dub, wb_s[:, m * BW:(m + 1) * BW])
        s5buf[2 * m] = bu[:, 0:LANES]
        s5buf[2 * m + 1] = bu[:, LANES:2 * LANES]
    ahead()
    hl = _iota((1, LANES), 1)

    def head_lanes(name_at):
        out = jnp.zeros((1, LANES), F32)
        for name, off in name_at:
            r = vec(name)
            for h in range(NH):
                out = jnp.where(hl == off + h, r[:, h:h + 1], out)
        return out

    def spread(tile, off):
        return jnp.concatenate([jnp.broadcast_to(tile[:, off + h:off + h + 1], (rows, HD)) for h in range(NH)], axis=1)

    narrow = zr[:, R_HEADS:R_HEADS + LANES]
    sp = _softplus(narrow + head_lanes((('ssd_dt_bias', H_DT), ('gdn_dt_bias', H_DECAY))))
    rate = sp * (-jnp.exp(head_lanes((('ssd_a_log', H_DT), ('gdn_a_log', H_DECAY)))))
    dt = spread(sp, H_DT)
    da = spread(rate, H_DT)
    gdec = spread(rate, H_DECAY)
    beta = spread(_sigmoid(narrow), H_BETA)
    xdt = xs * dt
    vb = vc * beta

    nsl = S5N // LANES
    a_re = [jnp.broadcast_to(tab[0:1, k * LANES:(k + 1) * LANES], (SUBLANES, LANES)) for k in range(nsl)]
    a_im = [jnp.broadcast_to(tab[1:2, k * LANES:(k + 1) * LANES], (SUBLANES, LANES)) for k in range(nsl)]
    s5_state = {}

    def s5_step(g, t):
        gs = slice(g * SUBLANES, (g + 1) * SUBLANES)
        if t == 0:
            s5_state['r'] = [s5r_out[gs, k * LANES:(k + 1) * LANES] for k in range(nsl)]
            s5_state['i'] = [s5i_out[gs, k * LANES:(k + 1) * LANES] for k in range(nsl)]
        hr, hi = s5_state['r'], s5_state['i']
        r0 = (g * lc + t) * SUBLANES
        idx = slice(r0, r0 + SUBLANES)
        for k in range(nsl):
            nr = a_re[k] * hr[k] - a_im[k] * hi[k] + s5buf[k, idx, :]
            ni = a_re[k] * hi[k] + a_im[k] * hr[k] + s5buf[nsl + k, idx, :]
            hr[k], hi[k] = nr, ni
            s5buf[k, idx, :] = nr
            s5buf[nsl + k, idx, :] = ni
        if t == lc - 1:
            for k in range(nsl):
                s5r_out[gs, k * LANES:(k + 1) * LANES] = hr[k]
                s5i_out[gs, k * LANES:(k + 1) * LANES] = hi[k]

    s5_steps = [(g, t) for g in range(ngroups) for t in range(lc)]
    n_parts = 16
    s5_done = [0]

    def s5_part():
        per = -(-len(s5_steps) // n_parts)
        for g, t in s5_steps[s5_done[0]:s5_done[0] + per]:
            s5_step(g, t)
        s5_done[0] += per

    t_c = _iota((CHUNK, BW), 0)
    s_c = _iota((CHUNK, BW), 1) & (HD - 1)
    same_unit = (t_c >> _log2(unit)) == (s_c >> _log2(unit))
    valid_incl = same_unit & (s_c <= t_c)
    valid_strict = same_unit & (s_c < t_c)
    eye_cat = jnp.where(s_c == t_c, 1.0, 0.0)
    lm = lm_ref[...]
    if units == 1:
        lm = lm[0:CHUNK]
    chunks = range(nchunks)
    sls = [slice(c * CHUNK, (c + 1) * CHUNK) for c in chunks]

    strict_f = jnp.where(s_c < t_c, 1.0, 0.0)
    acum, atot, decay, dtot, lmat, eg = [], [], [], [], [], []
    for c in chunks:
        da_c, gd_c = da[sls[c]], gdec[sls[c]]
        cs = _exact_left(lm, jnp.concatenate([da_c, gd_c, da_c * strict_f, gd_c * strict_f], axis=1))
        acum.append(cs[0:CHUNK, 0:BW])
        decay.append(cs[0:CHUNK, BW:2 * BW])
        lmat.append(jnp.where(valid_incl, jnp.exp(jnp.minimum(cs[0:CHUNK, 2 * BW:3 * BW], 0.0)), 0.0))
        eg.append(jnp.where(valid_incl, jnp.exp(jnp.minimum(cs[0:CHUNK, 3 * BW:4 * BW], 0.0)), 0.0))
        if units == 1:
            atot.append(cs[CHUNK - 1:CHUNK, 0:BW])
            dtot.append(cs[CHUNK - 1:CHUNK, BW:2 * BW])
        else:
            atot.append(cs[CHUNK:2 * CHUNK, 0:BW])
            dtot.append(cs[CHUNK:2 * CHUNK, BW:2 * BW])
    s5_part()
    ahead()

    def seq_of(c, u):
        return (c * CHUNK + u * unit) // lc

    def unit_rows(u):
        return slice(u * unit, (u + 1) * unit)

    def only_unit(v, u):
        return v if units == 1 else jnp.where((t_c >> _log2(unit)) == u, v, 0.0)

    def unit_total(tot_c, u):
        r0 = u * unit if units > 1 else 0
        return tot_c[r0:r0 + 1, :]

    def cat_rows(parts):
        return parts[0] if len(parts) == 1 else jnp.concatenate(parts, axis=0)

    cbm = [_dot_nt(cm[sls[c]].astype(BF16), bd(bm[sls[c]])) for c in chunks]
    s5_part()
    y_ssd = [_dot((cbm[c] * lmat[c]).astype(BF16), bd(xdt[sls[c]])) for c in chunks]
    s5_part()
    ahead()
    ht = [[ssd_out[seq_of(c, u)] for u in range(units)] for c in chunks]
    cme = [(cm[sls[c]] * jnp.exp(acum[c])).astype(BF16) for c in chunks]
    y_int = [cat_rows([_dot(cme[c][unit_rows(u)], bd(ht[c][u])) for u in range(units)]) for c in chunks]
    s5_part()
    wbt = [(bm[sls[c]] * jnp.exp(atot[c] - acum[c])).T.astype(BF16) for c in chunks]
    upd = [[_dot(wbt[c], only_unit(xdt[sls[c]], u).astype(BF16)) for u in range(units)] for c in chunks]
    for c in chunks:
        for u in range(units):
            ssd_out[seq_of(c, u)] = jnp.exp(unit_total(atot[c], u)) * ht[c][u] + _diag_blocks(upd[c][u], hd_mask)
    s5_part()
    ahead()

    qk_kk = [_dot_nt(jnp.concatenate([qn[sls[c]], kn[sls[c]]], axis=0).astype(BF16), bd(kn[sls[c]])) for c in chunks]
    s5_part()
    qkg = [qk_kk[c][0:CHUNK] * eg[c] for c in chunks]
    mm = [jnp.where(valid_strict, beta[sls[c]] * qk_kk[c][CHUNK:2 * CHUNK] * eg[c], 0.0) for c in chunks]
    rm = [eye_cat - mm[c] for c in chunks]
    pw = [_dot(mm[c].astype(BF16), bd(mm[c])) for c in chunks]
    s5_part()
    ahead()
    for _step in range(_log2(unit) - 2):
        pr2 = [_dot(jnp.concatenate([pw[c], rm[c]], axis=0).astype(BF16), bd(pw[c])) for c in chunks]
        pw = [pr2[c][0:CHUNK] for c in chunks]
        rm = [rm[c] + pr2[c][CHUNK:2 * CHUNK] for c in chunks]
        s5_part()
    rm = [(rm[c] + _dot(rm[c].astype(BF16), bd(pw[c]))).astype(BF16) for c in chunks]
    s5_part()
    ahead()
    edec = [jnp.exp(decay[c]) for c in chunks]
    value = [_dot(rm[c], bd(vb[sls[c]])) for c in chunks]
    kcum = [_dot(rm[c], bd(kn[sls[c]] * beta[sls[c]] * edec[c])) for c in chunks]
    s5_part()
    sq = [[gdn_out[seq_of(c, u)] for u in range(units)] for c in chunks]
    qdec = [qn[sls[c]] * edec[c] for c in chunks]
    kq = [[_dot(jnp.concatenate([kcum[c][unit_rows(u)], qdec[c][unit_rows(u)]], axis=0).astype(BF16), bd(sq[c][u]))
           for u in range(units)] for c in chunks]
    s5_part()
    ahead()
    wv = [cat_rows([value[c][unit_rows(u)] - kq[c][u][0:unit] for u in range(units)]) for c in chunks]
    o_chunks = [cat_rows([kq[c][u][unit:2 * unit] for u in range(units)]) + _dot(qkg[c].astype(BF16), bd(wv[c]))
                for c in chunks]
    kdec_t = [(kn[sls[c]] * jnp.exp(dtot[c] - decay[c])).T.astype(BF16) for c in chunks]
    upd = [[_dot(kdec_t[c], only_unit(wv[c], u).astype(BF16)) for u in range(units)] for c in chunks]
    for c in chunks:
        for u in range(units):
            gdn_out[seq_of(c, u)] = jnp.exp(unit_total(dtot[c], u)) * sq[c][u] + _diag_blocks(upd[c][u], hd_mask)
    while s5_done[0] < len(s5_steps):
        s5_part()
    ahead(n_conv_tiles + n_rest_tiles)

    def s5_states(m):
        return jnp.concatenate([s5buf[2 * m], s5buf[2 * m + 1]], axis=1).astype(BF16)

    y5 = vec('s5_d') * du
    for m in range(S5N // BW):
        y5 = y5 + _dot(s5_states(m), wc_s[m * BW:(m + 1) * BW, :])
    yb = jnp.concatenate([y_ssd[c] + y_int[c] for c in chunks], axis=0) + head_vec('ssd_d') * xs
    yb = yb * _silu(rest(R_GATE + BW))
    yb = yb * lax.rsqrt(jnp.mean(yb * yb, axis=-1, keepdims=True) + 1e-6) * vec('ssd_norm_w')
    mix[:, BW:2 * BW] = yb.astype(BF16)
    for m in range(S5N // BW, 2 * S5N // BW):
        y5 = y5 + _dot(s5_states(m), wc_s[m * BW:(m + 1) * BW, :])
    o = jnp.concatenate(o_chunks, axis=0)
    gdn_norm = jnp.concatenate([vec('gdn_norm_w')] * NH, axis=1)
    o = o * lax.rsqrt(head_sum(o * o) * (1.0 / HD) + 1e-6) * gdn_norm
    mix[:, 2 * BW:3 * BW] = (o * _silu(rest(R_GATE + 2 * BW))).astype(BF16)
    y5 = 0.5 * y5 * (1.0 + lax.erf(y5 * math.sqrt(0.5)))
    y5 = y5 * _sigmoid(_dot(y5.astype(BF16), glu_w[...]) + vec('s5_glu_b'))
    for k in range(BW // LANES):
        ybuf[k] = y5[:, k * LANES:(k + 1) * LANES]
    y5 = jnp.concatenate(
        [jnp.concatenate([ybuf[k, pl.ds(g * lc * SUBLANES + s, lc, stride=SUBLANES), :] for k in range(BW // LANES)],
                         axis=1) for g in range(ngroups) for s in range(SUBLANES)], axis=0)
    mix[:, 3 * BW:4 * BW] = (y5 * _silu(rest(R_GATE + 3 * BW))).astype(BF16)

    half = rows // 2
    seq_half = nseq // 2

    def out_proj(hh):
        xh = x_ref[hh * seq_half:(hh + 1) * seq_half].reshape(half, D_MODEL)
        return ALPHA * xh + _dot(mix[hh * half:(hh + 1) * half, :], w_out[...])

    def layer_norm(hh, res):
        mu = jnp.mean(res, axis=-1, keepdims=True)
        rc = res - mu
        var = jnp.mean(rc * rc, axis=-1, keepdims=True)
        y = rc * lax.rsqrt(var + 1e-5) * vec('ln_g') + vec('ln_b')
        y_ref[hh * seq_half:(hh + 1) * seq_half] = y.reshape(seq_half, lc, D_MODEL)

    res0 = out_proj(0)
    res1 = out_proj(1)
    layer_norm(0, res0)
    layer_norm(1, res1)


def _chunk_masks(unit):
    t = np.arange(CHUNK)
    same = (t[:, None] // unit) == (t[None, :] // unit)
    incl = same & (t[None, :] <= t[:, None])
    return jnp.asarray(np.concatenate([incl, same], axis=0), BF16)


def _block_diag(blocks):
    *lead, n, r, c = blocks.shape
    eye = jnp.eye(n, dtype=blocks.dtype)
    out = eye[:, None, :, None] * blocks[..., :, :, None, :]
    return out.reshape(*lead, n * r, n * c)


def _prep_params(w_in, w_out, rg_gate_a_w, rg_gate_x_w, s5_lambda_re, s5_lambda_im, s5_log_dt, s5_b_re, s5_b_im,
                 s5_c_re, s5_c_im, s5_glu_w, **small):
    sizes = (BW, BW, BW + 2 * SSD_GROUPS * HD, NH, BW, 3 * BW, NH, NH, BW, BW, BW)
    (a_x, a_gate, b_xbc, b_dt, b_gate, c_qkv, c_beta, c_decay, c_gate, d_u, d_gate) = np.cumsum((0,) + sizes)[:-1].tolist()
    assert (H_DT, H_BETA, H_DECAY) == (0, NH, 2 * NH)

    def cols(c0, n):
        return w_in[:, :, c0:c0 + n]

    out = dict(small)
    out.update(
        w_conv=jnp.concatenate([cols(a_x, BW), cols(b_xbc, 2 * BW), cols(c_qkv, 3 * BW)], axis=-1).astype(BF16),
        w_rest=jnp.concatenate([cols(a_gate, BW), cols(b_gate, BW), cols(c_gate, BW), cols(d_gate, BW), cols(d_u, BW),
                                cols(b_dt, NH), cols(c_beta, NH), cols(c_decay, NH),
                                jnp.zeros((DEPTH, D_MODEL, BW - 3 * NH), F32)], axis=-1).astype(BF16),
        w_out=w_out.astype(BF16),
        wg=jnp.concatenate([_block_diag(rg_gate_a_w), _block_diag(rg_gate_x_w)], axis=-1).astype(BF16),
        s5v=jnp.stack([s5_lambda_re.reshape(DEPTH, S5N), s5_lambda_im.reshape(DEPTH, S5N),
                       jnp.repeat(s5_log_dt, S5_STATE, axis=-1)], axis=1),
        bre=_block_diag(jnp.swapaxes(s5_b_re, -1, -2)),
        bim=_block_diag(jnp.swapaxes(s5_b_im, -1, -2)),
        cre=_block_diag(jnp.swapaxes(s5_c_re, -1, -2)),
        cim=_block_diag(jnp.swapaxes(s5_c_im, -1, -2)),
        glu_w=s5_glu_w.astype(BF16),
    )
    return [out[k] for k in _PARAM_NAMES]


def _layer_call(l, x, states, params, nseq, lc):
    bsz, seqlen, _ = x.shape
    rows = nseq * lc
    assert bsz % nseq == 0 and seqlen % lc == 0 and rows % CHUNK == 0 and lc % SUBLANES == 0 and nseq % SUBLANES == 0
    assert CHUNK % min(lc, CHUNK) == 0 and lc % min(lc, CHUNK) == 0
    pitch = _scan_pitch(lc)
    grid = (bsz // nseq, seqlen // lc)
    state_shapes = [(CONV_W - 1, BW), (CONV_W - 1, 2 * BW), (CONV_W - 1, 3 * BW), (BW,), (HD, BW), (HD, BW), (S5N,), (S5N,)]
    assert len(state_shapes) == _N_STATES

    def param_spec(name, a):
        if name in _LAYER_BLOCKS:
            return pl.BlockSpec((None,) + a.shape[1:], lambda i, j: (l,) + (0,) * (a.ndim - 1), pipeline_mode=pl.Buffered(1))
        return pl.BlockSpec(a.shape, lambda i, j: (0,) * a.ndim, pipeline_mode=pl.Buffered(1))

    lm = _chunk_masks(min(lc, CHUNK))
    x_spec = pl.BlockSpec((nseq, lc, D_MODEL), lambda i, j: (i, j, 0))
    state_args = [] if states is None else list(states)
    state_specs = [pl.BlockSpec((None, nseq) + s, lambda i, j, n=len(s): (l, i) + (0,) * n) for s in state_shapes]
    in_specs = ([x_spec] + (state_specs if state_args else [])
                + [param_spec(name, a) for name, a in zip(_PARAM_NAMES, params)]
                + [pl.BlockSpec(lm.shape, lambda i, j: (0, 0), pipeline_mode=pl.Buffered(1))])
    out_state_specs = [pl.BlockSpec((nseq,) + s, lambda i, j, n=len(s): (i,) + (0,) * n) for s in state_shapes]
    out_shape = [jax.ShapeDtypeStruct(x.shape, F32)] + [jax.ShapeDtypeStruct((bsz,) + s, F32) for s in state_shapes]
    scratch = [
        pltpu.VMEM((nseq, HIST + lc, NCONV), F32),
        pltpu.VMEM((rows, NREST), F32),
        pltpu.VMEM((rows, 4 * BW), BF16),
        pltpu.VMEM((S5_GROUPS * S5_GROUP, 2 * S5N), BF16),
        pltpu.VMEM((2 * S5N, S5_GROUPS * S5_GROUP), BF16),
        pltpu.VMEM((SUBLANES, S5N), F32),
        pltpu.VMEM((4, nseq * pitch, LANES), F32),
        pltpu.VMEM((2 * S5N // LANES, rows, LANES), F32),
        pltpu.VMEM((BW // LANES, nseq * pitch, LANES), F32),
        pltpu.VMEM((BW // LANES, rows, LANES), F32),
    ]
    outs = pl.pallas_call(
        functools.partial(_layer_kernel, l, nseq, lc, not state_args),
        grid=grid,
        in_specs=in_specs,
        out_specs=[x_spec] + out_state_specs,
        out_shape=out_shape,
        scratch_shapes=scratch,
        compiler_params=pltpu.CompilerParams(dimension_semantics=("arbitrary", "arbitrary"),
                                             vmem_limit_bytes=VMEM_LIMIT_BYTES),
    )(x, *state_args, *params, lm)
    return outs[0], outs[1:]


def _states_to_kernel(conv_a, h_a, conv_b, h_b, conv_c, s_c, s5_re, s5_im):
    d, bsz = h_a.shape[0], h_a.shape[1]
    ssd = jnp.transpose(h_b, (0, 1, 4, 2, 3)).reshape(d, bsz, HD, BW)
    gdn = jnp.transpose(s_c, (0, 1, 3, 2, 4)).reshape(d, bsz, HD, BW)
    return [conv_a, conv_b, conv_c, h_a, ssd, gdn, s5_re.reshape(d, bsz, S5N), s5_im.reshape(d, bsz, S5N)]


def _states_from_kernel(per_layer):
    conv_a, conv_b, conv_c, rg, ssd, gdn, s5r, s5i = (jnp.stack(t) for t in zip(*per_layer))
    d, bsz = rg.shape[0], rg.shape[1]
    return (conv_a, rg, conv_b,
            jnp.transpose(ssd.reshape(d, bsz, HD, NH, HD), (0, 1, 3, 4, 2)),
            conv_c,
            jnp.transpose(gdn.reshape(d, bsz, HD, NH, HD), (0, 1, 3, 2, 4)),
            s5r.reshape(d, bsz, S5_GROUPS, S5_STATE), s5i.reshape(d, bsz, S5_GROUPS, S5_STATE))


def kernel(x_prompt, x_sample, cache_rglru_conv, state_rglru, cache_ssd_conv, state_ssd, cache_gdn_conv, state_gdn,
           state_s5_re, state_s5_im, w_in, w_out, ln_g, ln_b, rg_conv_w, rg_conv_b, rg_gate_a_w, rg_gate_a_b,
           rg_gate_x_w, rg_gate_x_b, rg_lambda, ssd_conv_w, ssd_conv_b, ssd_dt_bias, ssd_a_log, ssd_d, ssd_norm_w,
           gdn_conv_w, gdn_conv_b, gdn_dt_bias, gdn_a_log, gdn_norm_w, s5_lambda_re, s5_lambda_im, s5_log_dt,
           s5_b_re, s5_b_im, s5_c_re, s5_c_im, s5_d, s5_glu_w, s5_glu_b):
    params = _prep_params(w_in, w_out, rg_gate_a_w, rg_gate_x_w, s5_lambda_re, s5_lambda_im, s5_log_dt, s5_b_re,
                          s5_b_im, s5_c_re, s5_c_im, s5_glu_w,
                          ln_g=ln_g, ln_b=ln_b, rg_conv_w=rg_conv_w, ssd_conv_w=ssd_conv_w, gdn_conv_w=gdn_conv_w,
                          rg_conv_b=rg_conv_b, ssd_conv_b=ssd_conv_b, gdn_conv_b=gdn_conv_b, rg_gate_a_b=rg_gate_a_b,
                          rg_gate_x_b=rg_gate_x_b, rg_lambda=rg_lambda, ssd_dt_bias=ssd_dt_bias, ssd_a_log=ssd_a_log,
                          ssd_d=ssd_d, ssd_norm_w=ssd_norm_w, gdn_dt_bias=gdn_dt_bias, gdn_a_log=gdn_a_log,
                          gdn_norm_w=gdn_norm_w, s5_d=s5_d, s5_glu_b=s5_glu_b)
    sample_states = _states_to_kernel(cache_rglru_conv, state_rglru, cache_ssd_conv, state_ssd, cache_gdn_conv,
                                      state_gdn, state_s5_re, state_s5_im)
    pb, pl_len = x_prompt.shape[0], x_prompt.shape[1]
    sl = x_sample.shape[1]
    lc_p = ROWS // pb
    assert pl_len % lc_p == 0
    yp, ys = x_prompt, x_sample
    p_new, s_new = [], []
    for l in range(DEPTH):
        yp, st_p = _layer_call(l, yp, None, params, nseq=pb, lc=lc_p)
        ys, st_s = _layer_call(l, ys, sample_states, params, nseq=SAMPLE_SEQS, lc=sl)
        p_new.append(st_p)
        s_new.append(st_s)
    return (yp, ys, *_states_from_kernel(p_new), *_states_from_kernel(s_new))
```

```python
import functools
import math

import numpy as np
import jax
import jax.numpy as jnp
from jax import lax
from jax.experimental import pallas as pl
from jax.experimental.pallas import tpu as pltpu

F32 = jnp.float32
BF16 = jnp.bfloat16

D_MODEL = 1024
DEPTH = 2
BW = 256
HD = 64
NH = BW // HD
SSD_GROUPS = 2
CONV_W = 4
HIST = 8
S5_GROUPS = 16
S5_GROUP = 16
S5_STATE = 64
S5N = S5_GROUPS * S5_STATE
LANES = 128
SUBLANES = 8
RG_C = 8.0
ALPHA = (2.0 * DEPTH) ** 0.25
CHUNK = 64
ROWS = 512
SAMPLE_SEQS = 32
VMEM_LIMIT_BYTES = 60 * 1024 * 1024

NCONV = 6 * BW
C_AX, C_BX, C_BBC, C_Q, C_K, C_V = (k * BW for k in range(6))
NREST = 6 * BW
R_GATE, R_DU, R_HEADS = 0, 4 * BW, 5 * BW
H_DT, H_BETA, H_DECAY = 0, NH, 2 * NH


def _dot(a, b):
    return jnp.dot(a, b, preferred_element_type=F32)


def _dot_nt(a, b):
    return lax.dot_general(a, b, (((1,), (1,)), ((), ())), preferred_element_type=F32)


def _sigmoid(x):
    return 0.5 * jnp.tanh(0.5 * x) + 0.5


def _silu(x):
    hx = 0.5 * x
    return hx * jnp.tanh(hx) + hx


def _softplus(x):
    return jnp.maximum(x, 0.0) + jnp.log1p(jnp.exp(-jnp.abs(x)))


def _split_bf16(x, pieces):
    out = []
    r = x
    for k in range(pieces):
        p = r.astype(BF16)
        out.append(p)
        if k + 1 < pieces:
            r = r - p.astype(F32)
    return out


def _exact_left(mask_bf16, x):
    acc = None
    for p in _split_bf16(x, 3):
        t = _dot(mask_bf16, p)
        acc = t if acc is None else acc + t
    return acc


def _iota(shape, dim):
    return lax.broadcasted_iota(jnp.int32, shape, dim)


def _log2(n):
    k = int(round(math.log2(n)))
    assert 1 << k == n
    return k


def _bd(x, mask):
    return jnp.where(mask, jnp.tile(x, (NH, 1)), 0.0).astype(BF16)


def _diag_blocks(full, mask):
    fm = jnp.where(mask, full, 0.0)
    return fm[0:HD] + fm[HD:2 * HD] + fm[2 * HD:3 * HD] + fm[3 * HD:4 * HD]


def _scan_pitch(lc):
    return lc if (lc // SUBLANES) % 2 == 1 else lc + SUBLANES


_PARAM_NAMES = ('w_conv', 'w_rest', 'w_out', 'wg', 's5v', 'bre', 'bim', 'cre', 'cim', 'glu_w',
                'ln_g', 'ln_b', 'rg_conv_w', 'ssd_conv_w', 'gdn_conv_w', 'rg_conv_b', 'ssd_conv_b', 'gdn_conv_b',
                'rg_gate_a_b', 'rg_gate_x_b', 'rg_lambda', 'ssd_dt_bias', 'ssd_a_log', 'ssd_d', 'ssd_norm_w',
                'gdn_dt_bias', 'gdn_a_log', 'gdn_norm_w', 's5_d', 's5_glu_b')
_LAYER_BLOCKS = frozenset(_PARAM_NAMES[:10])
_N_STATES = 8
S5V_LRE, S5V_LIM, S5V_LOGDT = range(3)


def _layer_kernel(l, nseq, lc, zero_init, *refs):
    n_in = 0 if zero_init else _N_STATES
    x_ref = refs[0]
    state_in = refs[1:1 + n_in]
    prm = dict(zip(_PARAM_NAMES, refs[1 + n_in:1 + n_in + len(_PARAM_NAMES)]))
    rest_refs = refs[1 + n_in + len(_PARAM_NAMES):]
    lm_ref, y_ref = rest_refs[0], rest_refs[1]
    state_out = rest_refs[2:2 + _N_STATES]
    ca_out, cb_out, cc_out, rg_out, ssd_out, gdn_out, s5r_out, s5i_out = state_out
    zext, zr, mix, wb_s, wc_s, tab, rgbuf, s5buf, dubuf, ybuf = rest_refs[2 + _N_STATES:]
    w_conv, w_rest, w_out, wg, glu_w = (prm[k] for k in ('w_conv', 'w_rest', 'w_out', 'wg', 'glu_w'))
    s5v, bre, bim, cre, cim = (prm[k] for k in ('s5v', 'bre', 'bim', 'cre', 'cim'))

    rows = nseq * lc
    pitch = _scan_pitch(lc)
    ngroups = nseq // SUBLANES
    unit = min(lc, CHUNK)
    units = CHUNK // unit
    nchunks = rows // CHUNK
    first_call_step = (pl.program_id(0) == 0) & (pl.program_id(1) == 0)

    def vec(name):
        return prm[name][l:l + 1, :]

    def head_vec(name):
        r = vec(name)
        return jnp.concatenate([jnp.broadcast_to(r[:, h:h + 1], (1, HD)) for h in range(NH)], axis=1)

    @pl.when(first_call_step)
    def _():
        lr = s5v[S5V_LRE:S5V_LRE + 1, :]
        li = s5v[S5V_LIM:S5V_LIM + 1, :]
        dt = jnp.exp(s5v[S5V_LOGDT:S5V_LOGDT + 1, :])
        mag = jnp.exp(lr * dt)
        ang = li * dt
        ar = mag * jnp.cos(ang)
        ai = mag * jnp.sin(ang)
        den = lr * lr + li * li
        fr = ((ar - 1.0) * lr + ai * li) / den
        fi = (ai * lr - (ar - 1.0) * li) / den
        def group_diag(blocks, rows_per_group, lanes_per_group):
            t = jnp.concatenate([blocks] * S5_GROUPS, axis=1)
            same = ((_iota(t.shape, 0) >> _log2(rows_per_group)) == (_iota(t.shape, 1) >> _log2(lanes_per_group)))
            return jnp.where(same, t, 0.0)

        b_re, b_im = (group_diag(r[...], S5_GROUP, S5_STATE) for r in (bre, bim))
        wb_s[:, 0:S5N] = (fr * b_re - fi * b_im).astype(BF16)
        wb_s[:, S5N:2 * S5N] = (fr * b_im + fi * b_re).astype(BF16)
        wc_s[0:S5N, :] = group_diag(cre[...], S5_STATE, S5_GROUP).astype(BF16)
        wc_s[S5N:2 * S5N, :] = (-group_diag(cim[...], S5_STATE, S5_GROUP)).astype(BF16)
        tab[0:1, :] = ar
        tab[1:2, :] = ai

    @pl.when(pl.program_id(1) == 0)
    def _():
        for k, out in enumerate(state_out):
            out[...] = jnp.zeros(out.shape, F32) if zero_init else state_in[k][...]

    hd_mask = (_iota((BW, BW), 0) >> _log2(HD)) == (_iota((BW, BW), 1) >> _log2(HD))
    bd = functools.partial(_bd, mask=hd_mask)

    h0 = HIST - (CONV_W - 1)

    conv_src = {C_AX: ('rg', 0), C_BX: ('ssd', 0), C_BBC: ('ssd', BW), C_Q: ('gdn', 0), C_K: ('gdn', BW),
                C_V: ('gdn', 2 * BW)}

    def conv(c0):
        branch, p0 = conv_src[c0]
        wts, bias = prm[branch + '_conv_w'], prm[branch + '_conv_b']
        ext = zext[:, :, c0:c0 + BW]
        acc = bias[l:l + 1, p0:p0 + BW] + wts[l, CONV_W - 1:CONV_W, p0:p0 + BW] * ext[:, HIST:, :]
        for back in range(1, CONV_W):
            tap = CONV_W - 1 - back
            acc = acc + wts[l, tap:tap + 1, p0:p0 + BW] * pltpu.roll(ext, back, 1)[:, HIST:, :]
        return acc.reshape(rows, BW)

    def rest(c0):
        return zr[:, c0:c0 + BW]

    def to_slabs(buf, first, val):
        for k in range(val.shape[1] // LANES):
            piece = val[:, k * LANES:(k + 1) * LANES]
            if pitch == lc:
                buf[first + k] = piece
            else:
                for q in range(nseq):
                    buf[first + k, q * pitch:q * pitch + lc, :] = piece[q * lc:(q + 1) * lc]

    def from_slabs(buf, first, n):
        cols = []
        for k in range(n):
            if pitch == lc:
                cols.append(buf[first + k])
            else:
                cols.append(jnp.concatenate([buf[first + k, q * pitch:q * pitch + lc, :] for q in range(nseq)], axis=0))
        return cols[0] if n == 1 else jnp.concatenate(cols, axis=1)

    def seq_rows(g, t):
        return pl.ds(g * SUBLANES * pitch + t, SUBLANES, stride=pitch)

    ones_bd = hd_mask.astype(BF16)

    def head_sum(v):
        return _dot(v.astype(BF16), ones_bd)

    lane = _iota((rows, LANES), 1)

    def head_rep(pair):
        rolled = pltpu.roll(pair, HD, 1)
        return jnp.concatenate([jnp.where(lane < HD, pair, rolled), jnp.where(lane < HD, rolled, pair)], axis=1)


    n_conv_tiles, n_rest_tiles = NCONV // BW, NREST // BW

    xb = x_ref[...].reshape(rows, D_MODEL).astype(BF16)
    rest_order = [R_GATE // BW, R_DU // BW, R_HEADS // BW, R_GATE // BW + 1, R_GATE // BW + 2, R_GATE // BW + 3]
    assert sorted(rest_order) == list(range(n_rest_tiles))

    def ahead(n=1):
        for _ in range(n):
            if rest_order:
                k = rest_order.pop(0)
                tile = _dot(xb, w_rest[:, k * BW:(k + 1) * BW])
                if k == R_DU // BW:
                    to_slabs(dubuf, 0, tile)
                else:
                    zr[:, k * BW:(k + 1) * BW] = tile

    def proj_conv_tile(k):
        zext[:, HIST:HIST + lc, k * BW:(k + 1) * BW] = _dot(xb, w_conv[:, k * BW:(k + 1) * BW]).reshape(nseq, lc, BW)

    zext[:, h0:HIST, C_AX:C_AX + BW] = ca_out[...]
    zext[:, h0:HIST, C_BX:C_BX + 2 * BW] = cb_out[...]
    zext[:, h0:HIST, C_Q:C_Q + 3 * BW] = cc_out[...]
    proj_conv_tile(0)
    proj_conv_tile(1)
    xc = conv(C_AX)
    proj_conv_tile(2)
    xs = _silu(conv(C_BX))
    proj_conv_tile(3)
    bc = _silu(conv(C_BBC))
    bm = head_rep(bc[:, 0:LANES])
    cm = head_rep(bc[:, LANES:2 * LANES])
    proj_conv_tile(4)
    qc = _silu(conv(C_Q))
    proj_conv_tile(5)
    kc = _silu(conv(C_K))
    ahead()
    vc = _silu(conv(C_V))
    ca_out[...] = zext[:, lc + h0:lc + HIST, C_AX:C_AX + BW]
    cb_out[...] = zext[:, lc + h0:lc + HIST, C_BX:C_BX + 2 * BW]
    cc_out[...] = zext[:, lc + h0:lc + HIST, C_Q:C_Q + 3 * BW]
    ahead()

    gts = _dot(xc.astype(BF16), wg[...])
    gate_r = _sigmoid(gts[:, 0:BW] + vec('rg_gate_a_b'))
    gate_i = _sigmoid(gts[:, BW:2 * BW] + vec('rg_gate_x_b'))
    log_a = (-RG_C * _softplus(-vec('rg_lambda'))) * gate_r
    a = jnp.exp(log_a)
    to_slabs(rgbuf, 0, a)
    to_slabs(rgbuf, 2, jnp.sqrt(-jnp.tanh(log_a) * (a * a + 1.0)) * (gate_i * xc))
    for g in range(ngroups):
        gs = slice(g * SUBLANES, (g + 1) * SUBLANES)
        h = [rg_out[gs, k * LANES:(k + 1) * LANES] for k in range(2)]
        for t in range(lc):
            idx = seq_rows(g, t)
            for k in range(2):
                h[k] = rgbuf[k, idx, :] * h[k] + rgbuf[2 + k, idx, :]
                rgbuf[2 + k, idx, :] = h[k]
        for k in range(2):
            rg_out[gs, k * LANES:(k + 1) * LANES] = h[k]
    ahead()
    mix[:, 0:BW] = (from_slabs(rgbuf, 2, 2) * _silu(rest(R_GATE))).astype(BF16)

    du = jnp.concatenate([jnp.concatenate([dubuf[k, seq_rows(g, t), :] for k in range(BW // LANES)], axis=1)
                          for g in range(ngroups) for t in range(lc)], axis=0)
    dub = du.astype(BF16)
    qn = qc * lax.rsqrt(head_sum(qc * qc) + 1e-6) * (HD ** -0.5)
    kn = kc * lax.rsqrt(head_sum(kc * kc) + 1e-6)
    for m in range(2 * S5N // BW):
        bu = _dot(dub, wb_s[:, m * BW:(m + 1) * BW])
        s5buf[2 * m] = bu[:, 0:LANES]
        s5buf[2 * m + 1] = bu[:, LANES:2 * LANES]
    ahead()
    hl = _iota((1, LANES), 1)

    def head_lanes(name_at):
        out = jnp.zeros((1, LANES), F32)
        for name, off in name_at:
            r = vec(name)
            for h in range(NH):
                out = jnp.where(hl == off + h, r[:, h:h + 1], out)
        return out

    def spread(tile, off):
        return jnp.concatenate([jnp.broadcast_to(tile[:, off + h:off + h + 1], (tile.shape[0], HD))
                                for h in range(NH)], axis=1)

    narrow = zr[:, R_HEADS:R_HEADS + LANES]
    sp = _softplus(narrow + head_lanes((('ssd_dt_bias', H_DT), ('gdn_dt_bias', H_DECAY))))
    rate = sp * (-jnp.exp(head_lanes((('ssd_a_log', H_DT), ('gdn_a_log', H_DECAY)))))
    dt = spread(sp, H_DT)
    da = spread(rate, H_DT)
    gdec = spread(rate, H_DECAY)
    beta = spread(_sigmoid(narrow), H_BETA)
    xdt = xs * dt
    vb = vc * beta

    nsl = S5N // LANES
    a_re = [jnp.broadcast_to(tab[0:1, k * LANES:(k + 1) * LANES], (SUBLANES, LANES)) for k in range(nsl)]
    a_im = [jnp.broadcast_to(tab[1:2, k * LANES:(k + 1) * LANES], (SUBLANES, LANES)) for k in range(nsl)]
    s5_state = {}

    def s5_step(g, t):
        gs = slice(g * SUBLANES, (g + 1) * SUBLANES)
        if t == 0:
            s5_state['r'] = [s5r_out[gs, k * LANES:(k + 1) * LANES] for k in range(nsl)]
            s5_state['i'] = [s5i_out[gs, k * LANES:(k + 1) * LANES] for k in range(nsl)]
        hr, hi = s5_state['r'], s5_state['i']
        r0 = (g * lc + t) * SUBLANES
        idx = slice(r0, r0 + SUBLANES)
        for k in range(nsl):
            nr = a_re[k] * hr[k] - a_im[k] * hi[k] + s5buf[k, idx, :]
            ni = a_re[k] * hi[k] + a_im[k] * hr[k] + s5buf[nsl + k, idx, :]
            hr[k], hi[k] = nr, ni
            s5buf[k, idx, :] = nr
            s5buf[nsl + k, idx, :] = ni
        if t == lc - 1:
            for k in range(nsl):
                s5r_out[gs, k * LANES:(k + 1) * LANES] = hr[k]
                s5i_out[gs, k * LANES:(k + 1) * LANES] = hi[k]

    s5_steps = [(g, t) for g in range(ngroups) for t in range(lc)]
    n_parts = 16
    s5_done = [0]

    def s5_part():
        per = -(-len(s5_steps) // n_parts)
        for g, t in s5_steps[s5_done[0]:s5_done[0] + per]:
            s5_step(g, t)
        s5_done[0] += per

    t_c = _iota((CHUNK, BW), 0)
    s_c = _iota((CHUNK, BW), 1) & (HD - 1)
    same_unit = (t_c >> _log2(unit)) == (s_c >> _log2(unit))
    valid_incl = same_unit & (s_c <= t_c)
    valid_strict = same_unit & (s_c < t_c)
    eye_cat = jnp.where(s_c == t_c, 1.0, 0.0)
    lm = lm_ref[...]
    if units == 1:
        lm = lm[0:CHUNK]
    chunks = range(nchunks)
    sls = [slice(c * CHUNK, (c + 1) * CHUNK) for c in chunks]

    strict_f = jnp.where(s_c < t_c, 1.0, 0.0)
    acum, atot, decay, dtot, lmat, eg = [], [], [], [], [], []
    for c in chunks:
        da_c, gd_c = da[sls[c]], gdec[sls[c]]
        cs = _exact_left(lm, jnp.concatenate([rate[sls[c]], da_c * strict_f, gd_c * strict_f], axis=1))
        acum.append(spread(cs[0:CHUNK, 0:LANES], H_DT))
        decay.append(spread(cs[0:CHUNK, 0:LANES], H_DECAY))
        lmat.append(jnp.where(valid_incl, jnp.exp(jnp.minimum(cs[0:CHUNK, LANES:LANES + BW], 0.0)), 0.0))
        eg.append(jnp.where(valid_incl, jnp.exp(jnp.minimum(cs[0:CHUNK, LANES + BW:LANES + 2 * BW], 0.0)), 0.0))
        tot = cs[CHUNK - 1:CHUNK, 0:LANES] if units == 1 else cs[CHUNK:2 * CHUNK, 0:LANES]
        atot.append(spread(tot, H_DT))
        dtot.append(spread(tot, H_DECAY))
    s5_part()
    ahead()

    def seq_of(c, u):
        return (c * CHUNK + u * unit) // lc

    def unit_rows(u):
        return slice(u * unit, (u + 1) * unit)

    def only_unit(v, u):
        return v if units == 1 else jnp.where((t_c >> _log2(unit)) == u, v, 0.0)

    def unit_total(tot_c, u):
        r0 = u * unit if units > 1 else 0
        return tot_c[r0:r0 + 1, :]

    def cat_rows(parts):
        return parts[0] if len(parts) == 1 else jnp.concatenate(parts, axis=0)

    cbm = [_dot_nt(cm[sls[c]].astype(BF16), bd(bm[sls[c]])) for c in chunks]
    s5_part()
    y_ssd = [_dot((cbm[c] * lmat[c]).astype(BF16), bd(xdt[sls[c]])) for c in chunks]
    s5_part()
    ahead()
    ht = [[ssd_out[seq_of(c, u)] for u in range(units)] for c in chunks]
    cme = [(cm[sls[c]] * jnp.exp(acum[c])).astype(BF16) for c in chunks]
    y_int = [cat_rows([_dot(cme[c][unit_rows(u)], bd(ht[c][u])) for u in range(units)]) for c in chunks]
    s5_part()
    wbt = [(bm[sls[c]] * jnp.exp(atot[c] - acum[c])).T.astype(BF16) for c in chunks]
    upd = [[_dot(wbt[c], only_unit(xdt[sls[c]], u).astype(BF16)) for u in range(units)] for c in chunks]
    for c in chunks:
        for u in range(units):
            ssd_out[seq_of(c, u)] = jnp.exp(unit_total(atot[c], u)) * ht[c][u] + _diag_blocks(upd[c][u], hd_mask)
    s5_part()
    ahead()

    qk_kk = [_dot_nt(jnp.concatenate([qn[sls[c]], kn[sls[c]]], axis=0).astype(BF16), bd(kn[sls[c]])) for c in chunks]
    s5_part()
    qkg = [qk_kk[c][0:CHUNK] * eg[c] for c in chunks]
    mm = [jnp.where(valid_strict, beta[sls[c]] * qk_kk[c][CHUNK:2 * CHUNK] * eg[c], 0.0) for c in chunks]
    rm = [eye_cat - mm[c] for c in chunks]
    pw = [_dot(mm[c].astype(BF16), bd(mm[c])) for c in chunks]
    s5_part()
    ahead()
    for _step in range(_log2(unit) - 2):
        pr2 = [_dot(jnp.concatenate([pw[c], rm[c]], axis=0).astype(BF16), bd(pw[c])) for c in chunks]
        pw = [pr2[c][0:CHUNK] for c in chunks]
        rm = [rm[c] + pr2[c][CHUNK:2 * CHUNK] for c in chunks]
        s5_part()
    rm = [(rm[c] + _dot(rm[c].astype(BF16), bd(pw[c]))).astype(BF16) for c in chunks]
    s5_part()
    ahead()
    edec = [jnp.exp(decay[c]) for c in chunks]
    value = [_dot(rm[c], bd(vb[sls[c]])) for c in chunks]
    kcum = [_dot(rm[c], bd(kn[sls[c]] * beta[sls[c]] * edec[c])) for c in chunks]
    s5_part()
    sq = [[gdn_out[seq_of(c, u)] for u in range(units)] for c in chunks]
    qdec = [qn[sls[c]] * edec[c] for c in chunks]
    kq = [[_dot(jnp.concatenate([kcum[c][unit_rows(u)], qdec[c][unit_rows(u)]], axis=0).astype(BF16), bd(sq[c][u]))
           for u in range(units)] for c in chunks]
    s5_part()
    ahead()
    wv = [cat_rows([value[c][unit_rows(u)] - kq[c][u][0:unit] for u in range(units)]) for c in chunks]
    o_chunks = [cat_rows([kq[c][u][unit:2 * unit] for u in range(units)]) + _dot(qkg[c].astype(BF16), bd(wv[c]))
                for c in chunks]
    kdec_t = [(kn[sls[c]] * jnp.exp(dtot[c] - decay[c])).T.astype(BF16) for c in chunks]
    upd = [[_dot(kdec_t[c], only_unit(wv[c], u).astype(BF16)) for u in range(units)] for c in chunks]
    for c in chunks:
        for u in range(units):
            gdn_out[seq_of(c, u)] = jnp.exp(unit_total(dtot[c], u)) * sq[c][u] + _diag_blocks(upd[c][u], hd_mask)
    while s5_done[0] < len(s5_steps):
        s5_part()
    ahead(n_conv_tiles + n_rest_tiles)

    def s5_states(m):
        return jnp.concatenate([s5buf[2 * m], s5buf[2 * m + 1]], axis=1).astype(BF16)

    y5 = vec('s5_d') * du
    for m in range(S5N // BW):
        y5 = y5 + _dot(s5_states(m), wc_s[m * BW:(m + 1) * BW, :])
    yb = jnp.concatenate([y_ssd[c] + y_int[c] for c in chunks], axis=0) + head_vec('ssd_d') * xs
    yb = yb * _silu(rest(R_GATE + BW))
    yb = yb * lax.rsqrt(jnp.mean(yb * yb, axis=-1, keepdims=True) + 1e-6) * vec('ssd_norm_w')
    mix[:, BW:2 * BW] = yb.astype(BF16)
    for m in range(S5N // BW, 2 * S5N // BW):
        y5 = y5 + _dot(s5_states(m), wc_s[m * BW:(m + 1) * BW, :])
    o = jnp.concatenate(o_chunks, axis=0)
    gdn_norm = jnp.concatenate([vec('gdn_norm_w')] * NH, axis=1)
    o = o * lax.rsqrt(head_sum(o * o) * (1.0 / HD) + 1e-6) * gdn_norm
    mix[:, 2 * BW:3 * BW] = (o * _silu(rest(R_GATE + 2 * BW))).astype(BF16)
    y5 = 0.5 * y5 * (1.0 + lax.erf(y5 * math.sqrt(0.5)))
    y5 = y5 * _sigmoid(_dot(y5.astype(BF16), glu_w[...]) + vec('s5_glu_b'))
    for k in range(BW // LANES):
        ybuf[k] = y5[:, k * LANES:(k + 1) * LANES]
    y5 = jnp.concatenate(
        [jnp.concatenate([ybuf[k, pl.ds(g * lc * SUBLANES + s, lc, stride=SUBLANES), :] for k in range(BW // LANES)],
                         axis=1) for g in range(ngroups) for s in range(SUBLANES)], axis=0)
    mix[:, 3 * BW:4 * BW] = (y5 * _silu(rest(R_GATE + 3 * BW))).astype(BF16)

    half = rows // 2
    seq_half = nseq // 2

    def out_proj(hh):
        xh = x_ref[hh * seq_half:(hh + 1) * seq_half].reshape(half, D_MODEL)
        return ALPHA * xh + _dot(mix[hh * half:(hh + 1) * half, :], w_out[...])

    def layer_norm(hh, res):
        mu = jnp.mean(res, axis=-1, keepdims=True)
        rc = res - mu
        var = jnp.mean(rc * rc, axis=-1, keepdims=True)
        y = rc * lax.rsqrt(var + 1e-5) * vec('ln_g') + vec('ln_b')
        y_ref[hh * seq_half:(hh + 1) * seq_half] = y.reshape(seq_half, lc, D_MODEL)

    res0 = out_proj(0)
    res1 = out_proj(1)
    layer_norm(0, res0)
    layer_norm(1, res1)


def _chunk_masks(unit):
    t = np.arange(CHUNK)
    same = (t[:, None] // unit) == (t[None, :] // unit)
    incl = same & (t[None, :] <= t[:, None])
    return jnp.asarray(np.concatenate([incl, same], axis=0), BF16)


def _block_diag(blocks):
    *lead, n, r, c = blocks.shape
    eye = jnp.eye(n, dtype=blocks.dtype)
    out = eye[:, None, :, None] * blocks[..., :, :, None, :]
    return out.reshape(*lead, n * r, n * c)


def _prep_params(w_in, w_out, rg_gate_a_w, rg_gate_x_w, s5_lambda_re, s5_lambda_im, s5_log_dt, s5_b_re, s5_b_im,
                 s5_c_re, s5_c_im, s5_glu_w, **small):
    sizes = (BW, BW, BW + 2 * SSD_GROUPS * HD, NH, BW, 3 * BW, NH, NH, BW, BW, BW)
    (a_x, a_gate, b_xbc, b_dt, b_gate, c_qkv, c_beta, c_decay, c_gate, d_u, d_gate) = np.cumsum((0,) + sizes)[:-1].tolist()
    assert (H_DT, H_BETA, H_DECAY) == (0, NH, 2 * NH)

    def cols(c0, n):
        return w_in[:, :, c0:c0 + n]

    out = dict(small)
    out.update(
        w_conv=jnp.concatenate([cols(a_x, BW), cols(b_xbc, 2 * BW), cols(c_qkv, 3 * BW)], axis=-1).astype(BF16),
        w_rest=jnp.concatenate([cols(a_gate, BW), cols(b_gate, BW), cols(c_gate, BW), cols(d_gate, BW), cols(d_u, BW),
                                cols(b_dt, NH), cols(c_beta, NH), cols(c_decay, NH),
                                jnp.zeros((DEPTH, D_MODEL, BW - 3 * NH), F32)], axis=-1).astype(BF16),
        w_out=w_out.astype(BF16),
        wg=jnp.concatenate([_block_diag(rg_gate_a_w), _block_diag(rg_gate_x_w)], axis=-1).astype(BF16),
        s5v=jnp.stack([s5_lambda_re.reshape(DEPTH, S5N), s5_lambda_im.reshape(DEPTH, S5N),
                       jnp.repeat(s5_log_dt, S5_STATE, axis=-1)], axis=1),
        bre=jnp.swapaxes(s5_b_re, -1, -2).reshape(DEPTH, S5_GROUPS * S5_GROUP, S5_STATE),
        bim=jnp.swapaxes(s5_b_im, -1, -2).reshape(DEPTH, S5_GROUPS * S5_GROUP, S5_STATE),
        cre=jnp.swapaxes(s5_c_re, -1, -2).reshape(DEPTH, S5N, S5_GROUP),
        cim=jnp.swapaxes(s5_c_im, -1, -2).reshape(DEPTH, S5N, S5_GROUP),
        glu_w=s5_glu_w.astype(BF16),
    )
    return [out[k] for k in _PARAM_NAMES]


def _layer_call(l, x, states, params, nseq, lc):
    bsz, seqlen, _ = x.shape
    rows = nseq * lc
    assert bsz % nseq == 0 and seqlen % lc == 0 and rows % CHUNK == 0 and lc % SUBLANES == 0 and nseq % SUBLANES == 0
    assert CHUNK % min(lc, CHUNK) == 0 and lc % min(lc, CHUNK) == 0
    pitch = _scan_pitch(lc)
    grid = (bsz // nseq, seqlen // lc)
    state_shapes = [(CONV_W - 1, BW), (CONV_W - 1, 2 * BW), (CONV_W - 1, 3 * BW), (BW,), (HD, BW), (HD, BW), (S5N,), (S5N,)]
    assert len(state_shapes) == _N_STATES

    def param_spec(name, a):
        if name in _LAYER_BLOCKS:
            return pl.BlockSpec((None,) + a.shape[1:], lambda i, j: (l,) + (0,) * (a.ndim - 1), pipeline_mode=pl.Buffered(1))
        return pl.BlockSpec(a.shape, lambda i, j: (0,) * a.ndim, pipeline_mode=pl.Buffered(1))

    lm = _chunk_masks(min(lc, CHUNK))
    x_spec = pl.BlockSpec((nseq, lc, D_MODEL), lambda i, j: (i, j, 0))
    state_args = [] if states is None else list(states)
    state_specs = [pl.BlockSpec((None, nseq) + s, lambda i, j, n=len(s): (l, i) + (0,) * n) for s in state_shapes]
    in_specs = ([x_spec] + (state_specs if state_args else [])
                + [param_spec(name, a) for name, a in zip(_PARAM_NAMES, params)]
                + [pl.BlockSpec(lm.shape, lambda i, j: (0, 0), pipeline_mode=pl.Buffered(1))])
    out_state_specs = [pl.BlockSpec((nseq,) + s, lambda i, j, n=len(s): (i,) + (0,) * n) for s in state_shapes]
    out_shape = [jax.ShapeDtypeStruct(x.shape, F32)] + [jax.ShapeDtypeStruct((bsz,) + s, F32) for s in state_shapes]
    scratch = [
        pltpu.VMEM((nseq, HIST + lc, NCONV), F32),
        pltpu.VMEM((rows, NREST), F32),
        pltpu.VMEM((rows, 4 * BW), BF16),
        pltpu.VMEM((S5_GROUPS * S5_GROUP, 2 * S5N), BF16),
        pltpu.VMEM((2 * S5N, S5_GROUPS * S5_GROUP), BF16),
        pltpu.VMEM((SUBLANES, S5N), F32),
        pltpu.VMEM((4, nseq * pitch, LANES), F32),
        pltpu.VMEM((2 * S5N // LANES, rows, LANES), F32),
        pltpu.VMEM((BW // LANES, nseq * pitch, LANES), F32),
        pltpu.VMEM((BW // LANES, rows, LANES), F32),
    ]
    outs = pl.pallas_call(
        functools.partial(_layer_kernel, l, nseq, lc, not state_args),
        grid=grid,
        in_specs=in_specs,
        out_specs=[x_spec] + out_state_specs,
        out_shape=out_shape,
        scratch_shapes=scratch,
        compiler_params=pltpu.CompilerParams(dimension_semantics=("arbitrary", "arbitrary"),
                                             vmem_limit_bytes=VMEM_LIMIT_BYTES),
    )(x, *state_args, *params, lm)
    return outs[0], outs[1:]


def _states_to_kernel(conv_a, h_a, conv_b, h_b, conv_c, s_c, s5_re, s5_im):
    d, bsz = h_a.shape[0], h_a.shape[1]
    ssd = jnp.transpose(h_b, (0, 1, 4, 2, 3)).reshape(d, bsz, HD, BW)
    gdn = jnp.transpose(s_c, (0, 1, 3, 2, 4)).reshape(d, bsz, HD, BW)
    return [conv_a, conv_b, conv_c, h_a, ssd, gdn, s5_re.reshape(d, bsz, S5N), s5_im.reshape(d, bsz, S5N)]


def _states_from_kernel(per_layer):
    conv_a, conv_b, conv_c, rg, ssd, gdn, s5r, s5i = (jnp.stack(t) for t in zip(*per_layer))
    d, bsz = rg.shape[0], rg.shape[1]
    return (conv_a, rg, conv_b,
            jnp.transpose(ssd.reshape(d, bsz, HD, NH, HD), (0, 1, 3, 4, 2)),
            conv_c,
            jnp.transpose(gdn.reshape(d, bsz, HD, NH, HD), (0, 1, 3, 2, 4)),
            s5r.reshape(d, bsz, S5_GROUPS, S5_STATE), s5i.reshape(d, bsz, S5_GROUPS, S5_STATE))


def kernel(x_prompt, x_sample, cache_rglru_conv, state_rglru, cache_ssd_conv, state_ssd, cache_gdn_conv, state_gdn,
           state_s5_re, state_s5_im, w_in, w_out, ln_g, ln_b, rg_conv_w, rg_conv_b, rg_gate_a_w, rg_gate_a_b,
           rg_gate_x_w, rg_gate_x_b, rg_lambda, ssd_conv_w, ssd_conv_b, ssd_dt_bias, ssd_a_log, ssd_d, ssd_norm_w,
           gdn_conv_w, gdn_conv_b, gdn_dt_bias, gdn_a_log, gdn_norm_w, s5_lambda_re, s5_lambda_im, s5_log_dt,
           s5_b_re, s5_b_im, s5_c_re, s5_c_im, s5_d, s5_glu_w, s5_glu_b):
    params = _prep_params(w_in, w_out, rg_gate_a_w, rg_gate_x_w, s5_lambda_re, s5_lambda_im, s5_log_dt, s5_b_re,
                          s5_b_im, s5_c_re, s5_c_im, s5_glu_w,
                          ln_g=ln_g, ln_b=ln_b, rg_conv_w=rg_conv_w, ssd_conv_w=ssd_conv_w, gdn_conv_w=gdn_conv_w,
                          rg_conv_b=rg_conv_b, ssd_conv_b=ssd_conv_b, gdn_conv_b=gdn_conv_b, rg_gate_a_b=rg_gate_a_b,
                          rg_gate_x_b=rg_gate_x_b, rg_lambda=rg_lambda, ssd_dt_bias=ssd_dt_bias, ssd_a_log=ssd_a_log,
                          ssd_d=ssd_d, ssd_norm_w=ssd_norm_w, gdn_dt_bias=gdn_dt_bias, gdn_a_log=gdn_a_log,
                          gdn_norm_w=gdn_norm_w, s5_d=s5_d, s5_glu_b=s5_glu_b)
    sample_states = _states_to_kernel(cache_rglru_conv, state_rglru, cache_ssd_conv, state_ssd, cache_gdn_conv,
                                      state_gdn, state_s5_re, state_s5_im)
    pb, pl_len = x_prompt.shape[0], x_prompt.shape[1]
    sl = x_sample.shape[1]
    lc_p = ROWS // pb
    assert pl_len % lc_p == 0
    yp, ys = x_prompt, x_sample
    p_new, s_new = [], []
    for l in range(DEPTH):
        yp, st_p = _layer_call(l, yp, None, params, nseq=pb, lc=lc_p)
        ys, st_s = _layer_call(l, ys, sample_states, params, nseq=SAMPLE_SEQS, lc=sl)
        p_new.append(st_p)
        s_new.append(st_s)
    return (yp, ys, *_states_from_kernel(p_new), *_states_from_kernel(s_new))
```

```python
import functools
import math

import numpy as np
import jax
import jax.numpy as jnp
from jax import lax
from jax.experimental import pallas as pl
from jax.experimental.pallas import tpu as pltpu

F32 = jnp.float32
BF16 = jnp.bfloat16

D_MODEL = 1024
DEPTH = 2
BW = 256
HD = 64
NH = BW // HD
SSD_GROUPS = 2
CONV_W = 4
HIST = 8
S5_GROUPS = 16
S5_GROUP = 16
S5_STATE = 64
S5N = S5_GROUPS * S5_STATE
LANES = 128
SUBLANES = 8
RG_C = 8.0
ALPHA = (2.0 * DEPTH) ** 0.25
CHUNK = 64
ROWS = 512
SAMPLE_SEQS = 32
VMEM_LIMIT_BYTES = 60 * 1024 * 1024

NCONV = 6 * BW
C_AX, C_BX, C_BBC, C_Q, C_K, C_V = (k * BW for k in range(6))
NREST = 6 * BW
R_GATE, R_DU, R_HEADS = 0, 4 * BW, 5 * BW
H_DT, H_BETA, H_DECAY = 0, NH, 2 * NH


def _dot(a, b):
    return jnp.dot(a, b, preferred_element_type=F32)


def _dot_nt(a, b):
    return lax.dot_general(a, b, (((1,), (1,)), ((), ())), preferred_element_type=F32)


def _sigmoid(x):
    return 0.5 * jnp.tanh(0.5 * x) + 0.5


def _silu(x):
    hx = 0.5 * x
    return hx * jnp.tanh(hx) + hx


def _softplus(x):
    return jnp.maximum(x, 0.0) + jnp.log1p(jnp.exp(-jnp.abs(x)))


def _split_bf16(x, pieces):
    out = []
    r = x
    for k in range(pieces):
        p = r.astype(BF16)
        out.append(p)
        if k + 1 < pieces:
            r = r - p.astype(F32)
    return out


def _exact_left(mask_bf16, x):
    acc = None
    for p in _split_bf16(x, 3):
        t = _dot(mask_bf16, p)
        acc = t if acc is None else acc + t
    return acc


def _iota(shape, dim):
    return lax.broadcasted_iota(jnp.int32, shape, dim)


def _log2(n):
    k = int(round(math.log2(n)))
    assert 1 << k == n
    return k


def _bd(x, mask):
    return jnp.where(mask, jnp.tile(x, (NH, 1)), 0.0).astype(BF16)


def _diag_blocks(full, mask):
    fm = jnp.where(mask, full, 0.0)
    return fm[0:HD] + fm[HD:2 * HD] + fm[2 * HD:3 * HD] + fm[3 * HD:4 * HD]


def _scan_pitch(lc):
    return lc if (lc // SUBLANES) % 2 == 1 else lc + SUBLANES


_PARAM_NAMES = ('w_in', 'w_out', 'wg', 's5v', 'bre', 'bim', 'cre', 'cim', 'glu_w',
                'ln_g', 'ln_b', 'rg_conv_w', 'ssd_conv_w', 'gdn_conv_w', 'rg_conv_b', 'ssd_conv_b', 'gdn_conv_b',
                'rg_gate_a_b', 'rg_gate_x_b', 'rg_lambda', 'ssd_dt_bias', 'ssd_a_log', 'ssd_d', 'ssd_norm_w',
                'gdn_dt_bias', 'gdn_a_log', 'gdn_norm_w', 's5_d', 's5_glu_b')
_LAYER_BLOCKS = frozenset(_PARAM_NAMES[:9])
_N_STATES = 8
S5V_LRE, S5V_LIM, S5V_LOGDT = range(3)

_IN_COLS = {}
_c0 = 0
for _name, _w in (('a_x', BW), ('a_gate', BW), ('b_xbc', BW + 2 * SSD_GROUPS * HD), ('b_dt', NH), ('b_gate', BW),
                  ('c_qkv', 3 * BW), ('c_beta', NH), ('c_decay', NH), ('c_gate', BW), ('d_u', BW), ('d_gate', BW)):
    _IN_COLS[_name] = (_c0, _w)
    _c0 += _w
IN_COLS = _c0
_CONV_FROM = (('a_x', C_AX), ('b_xbc', C_BX), ('c_qkv', C_Q))
_REST_FROM = (('a_gate', R_GATE), ('b_gate', R_GATE + BW), ('c_gate', R_GATE + 2 * BW), ('d_gate', R_GATE + 3 * BW),
              ('d_u', R_DU), ('b_dt', R_HEADS + H_DT), ('c_beta', R_HEADS + H_BETA), ('c_decay', R_HEADS + H_DECAY))


def _layer_kernel(l, nseq, lc, zero_init, *refs):
    n_in = 0 if zero_init else _N_STATES
    x_ref = refs[0]
    state_in = refs[1:1 + n_in]
    prm = dict(zip(_PARAM_NAMES, refs[1 + n_in:1 + n_in + len(_PARAM_NAMES)]))
    rest_refs = refs[1 + n_in + len(_PARAM_NAMES):]
    lm_ref, y_ref = rest_refs[0], rest_refs[1]
    state_out = rest_refs[2:2 + _N_STATES]
    ca_out, cb_out, cc_out, rg_out, ssd_out, gdn_out, s5r_out, s5i_out = state_out
    zext, zr, mix, wb_s, wc_s, tab, rgbuf, s5buf, dubuf, ybuf, w_conv, w_rest = rest_refs[2 + _N_STATES:]
    w_in, w_out, wg, glu_w = (prm[k] for k in ('w_in', 'w_out', 'wg', 'glu_w'))
    s5v, bre, bim, cre, cim = (prm[k] for k in ('s5v', 'bre', 'bim', 'cre', 'cim'))

    rows = nseq * lc
    pitch = _scan_pitch(lc)
    ngroups = nseq // SUBLANES
    unit = min(lc, CHUNK)
    units = CHUNK // unit
    nchunks = rows // CHUNK
    first_call_step = (pl.program_id(0) == 0) & (pl.program_id(1) == 0)

    def vec(name):
        return prm[name][l:l + 1, :]

    def head_vec(name):
        r = vec(name)
        return jnp.concatenate([jnp.broadcast_to(r[:, h:h + 1], (1, HD)) for h in range(NH)], axis=1)

    @pl.when(first_call_step)
    def _():
        for name, dst in _CONV_FROM:
            c0, width = _IN_COLS[name]
            w_conv[:, dst:dst + width] = w_in[:, c0:c0 + width]
        used = R_HEADS + 3 * NH
        w_rest[:, used:NREST] = jnp.zeros((D_MODEL, NREST - used), BF16)
        for name, dst in _REST_FROM:
            c0, width = _IN_COLS[name]
            w_rest[:, dst:dst + width] = w_in[:, c0:c0 + width]
        lr = s5v[S5V_LRE:S5V_LRE + 1, :]
        li = s5v[S5V_LIM:S5V_LIM + 1, :]
        dt = jnp.exp(s5v[S5V_LOGDT:S5V_LOGDT + 1, :])
        mag = jnp.exp(lr * dt)
        ang = li * dt
        ar = mag * jnp.cos(ang)
        ai = mag * jnp.sin(ang)
        den = lr * lr + li * li
        fr = ((ar - 1.0) * lr + ai * li) / den
        fi = (ai * lr - (ar - 1.0) * li) / den
        def group_diag(blocks, rows_per_group, lanes_per_group):
            t = jnp.concatenate([blocks] * S5_GROUPS, axis=1)
            same = ((_iota(t.shape, 0) >> _log2(rows_per_group)) == (_iota(t.shape, 1) >> _log2(lanes_per_group)))
            return jnp.where(same, t, 0.0)

        b_re, b_im = (group_diag(r[...], S5_GROUP, S5_STATE) for r in (bre, bim))
        wb_s[:, 0:S5N] = (fr * b_re - fi * b_im).astype(BF16)
        wb_s[:, S5N:2 * S5N] = (fr * b_im + fi * b_re).astype(BF16)
        wc_s[0:S5N, :] = group_diag(cre[...], S5_STATE, S5_GROUP).astype(BF16)
        wc_s[S5N:2 * S5N, :] = (-group_diag(cim[...], S5_STATE, S5_GROUP)).astype(BF16)
        tab[0:1, :] = ar
        tab[1:2, :] = ai

    @pl.when(pl.program_id(1) == 0)
    def _():
        for k, out in enumerate(state_out):
            out[...] = jnp.zeros(out.shape, F32) if zero_init else state_in[k][...]

    hd_mask = (_iota((BW, BW), 0) >> _log2(HD)) == (_iota((BW, BW), 1) >> _log2(HD))
    bd = functools.partial(_bd, mask=hd_mask)

    h0 = HIST - (CONV_W - 1)

    conv_src = {C_AX: ('rg', 0), C_BX: ('ssd', 0), C_BBC: ('ssd', BW), C_Q: ('gdn', 0), C_K: ('gdn', BW),
                C_V: ('gdn', 2 * BW)}

    def conv(c0):
        branch, p0 = conv_src[c0]
        wts, bias = prm[branch + '_conv_w'], prm[branch + '_conv_b']
        ext = zext[:, :, c0:c0 + BW]
        acc = bias[l:l + 1, p0:p0 + BW] + wts[l, CONV_W - 1:CONV_W, p0:p0 + BW] * ext[:, HIST:, :]
        for back in range(1, CONV_W):
            tap = CONV_W - 1 - back
            acc = acc + wts[l, tap:tap + 1, p0:p0 + BW] * pltpu.roll(ext, back, 1)[:, HIST:, :]
        return acc.reshape(rows, BW)

    def rest(c0):
        return zr[:, c0:c0 + BW]

    def to_slabs(buf, first, val):
        for k in range(val.shape[1] // LANES):
            piece = val[:, k * LANES:(k + 1) * LANES]
            if pitch == lc:
                buf[first + k] = piece
            else:
                for q in range(nseq):
                    buf[first + k, q * pitch:q * pitch + lc, :] = piece[q * lc:(q + 1) * lc]

    def from_slabs(buf, first, n):
        cols = []
        for k in range(n):
            if pitch == lc:
                cols.append(buf[first + k])
            else:
                cols.append(jnp.concatenate([buf[first + k, q * pitch:q * pitch + lc, :] for q in range(nseq)], axis=0))
        return cols[0] if n == 1 else jnp.concatenate(cols, axis=1)

    def seq_rows(g, t):
        return pl.ds(g * SUBLANES * pitch + t, SUBLANES, stride=pitch)

    ones_bd = hd_mask.astype(BF16)

    def head_sum(v):
        return _dot(v.astype(BF16), ones_bd)

    lane = _iota((rows, LANES), 1)

    def head_rep(pair):
        rolled = pltpu.roll(pair, HD, 1)
        return jnp.concatenate([jnp.where(lane < HD, pair, rolled), jnp.where(lane < HD, rolled, pair)], axis=1)


    n_conv_tiles, n_rest_tiles = NCONV // BW, NREST // BW

    xb = x_ref[...].reshape(rows, D_MODEL).astype(BF16)
    rest_order = [R_GATE // BW, R_DU // BW, R_HEADS // BW, R_GATE // BW + 1, R_GATE // BW + 2, R_GATE // BW + 3]
    assert sorted(rest_order) == list(range(n_rest_tiles))

    def ahead(n=1):
        for _ in range(n):
            if rest_order:
                k = rest_order.pop(0)
                tile = _dot(xb, w_rest[:, k * BW:(k + 1) * BW])
                if k == R_DU // BW:
                    to_slabs(dubuf, 0, tile)
                else:
                    zr[:, k * BW:(k + 1) * BW] = tile

    def proj_conv_tile(k):
        zext[:, HIST:HIST + lc, k * BW:(k + 1) * BW] = _dot(xb, w_conv[:, k * BW:(k + 1) * BW]).reshape(nseq, lc, BW)

    zext[:, h0:HIST, C_AX:C_AX + BW] = ca_out[...]
    zext[:, h0:HIST, C_BX:C_BX + 2 * BW] = cb_out[...]
    zext[:, h0:HIST, C_Q:C_Q + 3 * BW] = cc_out[...]
    proj_conv_tile(0)
    proj_conv_tile(1)
    xc = conv(C_AX)
    proj_conv_tile(2)
    xs = _silu(conv(C_BX))
    proj_conv_tile(3)
    bc = _silu(conv(C_BBC))
    bm = head_rep(bc[:, 0:LANES])
    cm = head_rep(bc[:, LANES:2 * LANES])
    proj_conv_tile(4)
    qc = _silu(conv(C_Q))
    proj_conv_tile(5)
    kc = _silu(conv(C_K))
    ahead()
    vc = _silu(conv(C_V))
    ca_out[...] = zext[:, lc + h0:lc + HIST, C_AX:C_AX + BW]
    cb_out[...] = zext[:, lc + h0:lc + HIST, C_BX:C_BX + 2 * BW]
    cc_out[...] = zext[:, lc + h0:lc + HIST, C_Q:C_Q + 3 * BW]
    ahead()

    gts = _dot(xc.astype(BF16), wg[...])
    gate_r = _sigmoid(gts[:, 0:BW] + vec('rg_gate_a_b'))
    gate_i = _sigmoid(gts[:, BW:2 * BW] + vec('rg_gate_x_b'))
    log_a = (-RG_C * _softplus(-vec('rg_lambda'))) * gate_r
    a = jnp.exp(log_a)
    to_slabs(rgbuf, 0, a)
    to_slabs(rgbuf, 2, jnp.sqrt(-jnp.tanh(log_a) * (a * a + 1.0)) * (gate_i * xc))
    for g in range(ngroups):
        gs = slice(g * SUBLANES, (g + 1) * SUBLANES)
        h = [rg_out[gs, k * LANES:(k + 1) * LANES] for k in range(2)]
        for t in range(lc):
            idx = seq_rows(g, t)
            for k in range(2):
                h[k] = rgbuf[k, idx, :] * h[k] + rgbuf[2 + k, idx, :]
                rgbuf[2 + k, idx, :] = h[k]
        for k in range(2):
            rg_out[gs, k * LANES:(k + 1) * LANES] = h[k]
    ahead()
    mix[:, 0:BW] = (from_slabs(rgbuf, 2, 2) * _silu(rest(R_GATE))).astype(BF16)

    du = jnp.concatenate([jnp.concatenate([dubuf[k, seq_rows(g, t), :] for k in range(BW // LANES)], axis=1)
                          for g in range(ngroups) for t in range(lc)], axis=0)
    dub = du.astype(BF16)
    qn = qc * lax.rsqrt(head_sum(qc * qc) + 1e-6) * (HD ** -0.5)
    kn = kc * lax.rsqrt(head_sum(kc * kc) + 1e-6)
    for m in range(2 * S5N // BW):
        bu = _dot(dub, wb_s[:, m * BW:(m + 1) * BW])
        s5buf[2 * m] = bu[:, 0:LANES]
        s5buf[2 * m + 1] = bu[:, LANES:2 * LANES]
    ahead()
    hl = _iota((1, LANES), 1)

    def head_lanes(name_at):
        out = jnp.zeros((1, LANES), F32)
        for name, off in name_at:
            r = vec(name)
            for h in range(NH):
                out = jnp.where(hl == off + h, r[:, h:h + 1], out)
        return out

    def spread(tile, off):
        return jnp.concatenate([jnp.broadcast_to(tile[:, off + h:off + h + 1], (tile.shape[0], HD))
                                for h in range(NH)], axis=1)

    narrow = zr[:, R_HEADS:R_HEADS + LANES]
    sp = _softplus(narrow + head_lanes((('ssd_dt_bias', H_DT), ('gdn_dt_bias', H_DECAY))))
    rate = sp * (-jnp.exp(head_lanes((('ssd_a_log', H_DT), ('gdn_a_log', H_DECAY)))))
    dt = spread(sp, H_DT)
    da = spread(rate, H_DT)
    gdec = spread(rate, H_DECAY)
    beta = spread(_sigmoid(narrow), H_BETA)
    xdt = xs * dt
    vb = vc * beta

    nsl = S5N // LANES
    a_re = [jnp.broadcast_to(tab[0:1, k * LANES:(k + 1) * LANES], (SUBLANES, LANES)) for k in range(nsl)]
    a_im = [jnp.broadcast_to(tab[1:2, k * LANES:(k + 1) * LANES], (SUBLANES, LANES)) for k in range(nsl)]
    s5_state = {}

    def s5_step(g, t):
        gs = slice(g * SUBLANES, (g + 1) * SUBLANES)
        if t == 0:
            s5_state['r'] = [s5r_out[gs, k * LANES:(k + 1) * LANES] for k in range(nsl)]
            s5_state['i'] = [s5i_out[gs, k * LANES:(k + 1) * LANES] for k in range(nsl)]
        hr, hi = s5_state['r'], s5_state['i']
        r0 = (g * lc + t) * SUBLANES
        idx = slice(r0, r0 + SUBLANES)
        for k in range(nsl):
            nr = a_re[k] * hr[k] - a_im[k] * hi[k] + s5buf[k, idx, :]
            ni = a_re[k] * hi[k] + a_im[k] * hr[k] + s5buf[nsl + k, idx, :]
            hr[k], hi[k] = nr, ni
            s5buf[k, idx, :] = nr
            s5buf[nsl + k, idx, :] = ni
        if t == lc - 1:
            for k in range(nsl):
                s5r_out[gs, k * LANES:(k + 1) * LANES] = hr[k]
                s5i_out[gs, k * LANES:(k + 1) * LANES] = hi[k]

    s5_steps = [(g, t) for g in range(ngroups) for t in range(lc)]
    n_parts = 16
    s5_done = [0]

    def s5_part():
        per = -(-len(s5_steps) // n_parts)
        for g, t in s5_steps[s5_done[0]:s5_done[0] + per]:
            s5_step(g, t)
        s5_done[0] += per

    t_c = _iota((CHUNK, BW), 0)
    s_c = _iota((CHUNK, BW), 1) & (HD - 1)
    same_unit = (t_c >> _log2(unit)) == (s_c >> _log2(unit))
    valid_incl = same_unit & (s_c <= t_c)
    valid_strict = same_unit & (s_c < t_c)
    eye_cat = jnp.where(s_c == t_c, 1.0, 0.0)
    lm = lm_ref[...]
    if units == 1:
        lm = lm[0:CHUNK]
    chunks = range(nchunks)
    sls = [slice(c * CHUNK, (c + 1) * CHUNK) for c in chunks]

    strict_f = jnp.where(s_c < t_c, 1.0, 0.0)
    acum, atot, decay, dtot, lmat, eg = [], [], [], [], [], []
    for c in chunks:
        da_c, gd_c = da[sls[c]], gdec[sls[c]]
        cs = _exact_left(lm, jnp.concatenate([rate[sls[c]], da_c * strict_f, gd_c * strict_f], axis=1))
        acum.append(spread(cs[0:CHUNK, 0:LANES], H_DT))
        decay.append(spread(cs[0:CHUNK, 0:LANES], H_DECAY))
        lmat.append(jnp.where(valid_incl, jnp.exp(jnp.minimum(cs[0:CHUNK, LANES:LANES + BW], 0.0)), 0.0))
        eg.append(jnp.where(valid_incl, jnp.exp(jnp.minimum(cs[0:CHUNK, LANES + BW:LANES + 2 * BW], 0.0)), 0.0))
        tot = cs[CHUNK - 1:CHUNK, 0:LANES] if units == 1 else cs[CHUNK:2 * CHUNK, 0:LANES]
        atot.append(spread(tot, H_DT))
        dtot.append(spread(tot, H_DECAY))
    s5_part()
    ahead()

    def seq_of(c, u):
        return (c * CHUNK + u * unit) // lc

    def unit_rows(u):
        return slice(u * unit, (u + 1) * unit)

    def only_unit(v, u):
        return v if units == 1 else jnp.where((t_c >> _log2(unit)) == u, v, 0.0)

    def unit_total(tot_c, u):
        r0 = u * unit if units > 1 else 0
        return tot_c[r0:r0 + 1, :]

    def cat_rows(parts):
        return parts[0] if len(parts) == 1 else jnp.concatenate(parts, axis=0)

    cbm = [_dot_nt(cm[sls[c]].astype(BF16), bd(bm[sls[c]])) for c in chunks]
    s5_part()
    y_ssd = [_dot((cbm[c] * lmat[c]).astype(BF16), bd(xdt[sls[c]])) for c in chunks]
    s5_part()
    ahead()
    ht = [[ssd_out[seq_of(c, u)] for u in range(units)] for c in chunks]
    cme = [(cm[sls[c]] * jnp.exp(acum[c])).astype(BF16) for c in chunks]
    y_int = [cat_rows([_dot(cme[c][unit_rows(u)], bd(ht[c][u])) for u in range(units)]) for c in chunks]
    s5_part()
    wbt = [(bm[sls[c]] * jnp.exp(atot[c] - acum[c])).T.astype(BF16) for c in chunks]
    upd = [[_dot(wbt[c], only_unit(xdt[sls[c]], u).astype(BF16)) for u in range(units)] for c in chunks]
    for c in chunks:
        for u in range(units):
            ssd_out[seq_of(c, u)] = jnp.exp(unit_total(atot[c], u)) * ht[c][u] + _diag_blocks(upd[c][u], hd_mask)
    s5_part()
    ahead()

    qk_kk = [_dot_nt(jnp.concatenate([qn[sls[c]], kn[sls[c]]], axis=0).astype(BF16), bd(kn[sls[c]])) for c in chunks]
    s5_part()
    qkg = [qk_kk[c][0:CHUNK] * eg[c] for c in chunks]
    mm = [jnp.where(valid_strict, beta[sls[c]] * qk_kk[c][CHUNK:2 * CHUNK] * eg[c], 0.0) for c in chunks]
    rm = [eye_cat - mm[c] for c in chunks]
    pw = [_dot(mm[c].astype(BF16), bd(mm[c])) for c in chunks]
    s5_part()
    ahead()
    for _step in range(_log2(unit) - 2):
        pr2 = [_dot(jnp.concatenate([pw[c], rm[c]], axis=0).astype(BF16), bd(pw[c])) for c in chunks]
        pw = [pr2[c][0:CHUNK] for c in chunks]
        rm = [rm[c] + pr2[c][CHUNK:2 * CHUNK] for c in chunks]
        s5_part()
    rm = [(rm[c] + _dot(rm[c].astype(BF16), bd(pw[c]))).astype(BF16) for c in chunks]
    s5_part()
    ahead()
    edec = [jnp.exp(decay[c]) for c in chunks]
    value = [_dot(rm[c], bd(vb[sls[c]])) for c in chunks]
    kcum = [_dot(rm[c], bd(kn[sls[c]] * beta[sls[c]] * edec[c])) for c in chunks]
    s5_part()
    sq = [[gdn_out[seq_of(c, u)] for u in range(units)] for c in chunks]
    qdec = [qn[sls[c]] * edec[c] for c in chunks]
    kq = [[_dot(jnp.concatenate([kcum[c][unit_rows(u)], qdec[c][unit_rows(u)]], axis=0).astype(BF16), bd(sq[c][u]))
           for u in range(units)] for c in chunks]
    s5_part()
    ahead()
    wv = [cat_rows([value[c][unit_rows(u)] - kq[c][u][0:unit] for u in range(units)]) for c in chunks]
    o_chunks = [cat_rows([kq[c][u][unit:2 * unit] for u in range(units)]) + _dot(qkg[c].astype(BF16), bd(wv[c]))
                for c in chunks]
    kdec_t = [(kn[sls[c]] * jnp.exp(dtot[c] - decay[c])).T.astype(BF16) for c in chunks]
    upd = [[_dot(kdec_t[c], only_unit(wv[c], u).astype(BF16)) for u in range(units)] for c in chunks]
    for c in chunks:
        for u in range(units):
            gdn_out[seq_of(c, u)] = jnp.exp(unit_total(dtot[c], u)) * sq[c][u] + _diag_blocks(upd[c][u], hd_mask)
    while s5_done[0] < len(s5_steps):
        s5_part()
    ahead(n_conv_tiles + n_rest_tiles)

    def s5_states(m):
        return jnp.concatenate([s5buf[2 * m], s5buf[2 * m + 1]], axis=1).astype(BF16)

    y5 = vec('s5_d') * du
    for m in range(S5N // BW):
        y5 = y5 + _dot(s5_states(m), wc_s[m * BW:(m + 1) * BW, :])
    yb = jnp.concatenate([y_ssd[c] + y_int[c] for c in chunks], axis=0) + head_vec('ssd_d') * xs
    yb = yb * _silu(rest(R_GATE + BW))
    yb = yb * lax.rsqrt(jnp.mean(yb * yb, axis=-1, keepdims=True) + 1e-6) * vec('ssd_norm_w')
    mix[:, BW:2 * BW] = yb.astype(BF16)
    for m in range(S5N // BW, 2 * S5N // BW):
        y5 = y5 + _dot(s5_states(m), wc_s[m * BW:(m + 1) * BW, :])
    o = jnp.concatenate(o_chunks, axis=0)
    gdn_norm = jnp.concatenate([vec('gdn_norm_w')] * NH, axis=1)
    o = o * lax.rsqrt(head_sum(o * o) * (1.0 / HD) + 1e-6) * gdn_norm
    mix[:, 2 * BW:3 * BW] = (o * _silu(rest(R_GATE + 2 * BW))).astype(BF16)
    y5 = 0.5 * y5 * (1.0 + lax.erf(y5 * math.sqrt(0.5)))
    y5 = y5 * _sigmoid(_dot(y5.astype(BF16), glu_w[...]) + vec('s5_glu_b'))
    for k in range(BW // LANES):
        ybuf[k] = y5[:, k * LANES:(k + 1) * LANES]
    y5 = jnp.concatenate(
        [jnp.concatenate([ybuf[k, pl.ds(g * lc * SUBLANES + s, lc, stride=SUBLANES), :] for k in range(BW // LANES)],
                         axis=1) for g in range(ngroups) for s in range(SUBLANES)], axis=0)
    mix[:, 3 * BW:4 * BW] = (y5 * _silu(rest(R_GATE + 3 * BW))).astype(BF16)

    half = rows // 2
    seq_half = nseq // 2

    def out_proj(hh):
        xh = x_ref[hh * seq_half:(hh + 1) * seq_half].reshape(half, D_MODEL)
        return ALPHA * xh + _dot(mix[hh * half:(hh + 1) * half, :], w_out[...])

    def layer_norm(hh, res):
        mu = jnp.mean(res, axis=-1, keepdims=True)
        rc = res - mu
        var = jnp.mean(rc * rc, axis=-1, keepdims=True)
        y = rc * lax.rsqrt(var + 1e-5) * vec('ln_g') + vec('ln_b')
        y_ref[hh * seq_half:(hh + 1) * seq_half] = y.reshape(seq_half, lc, D_MODEL)

    res0 = out_proj(0)
    res1 = out_proj(1)
    layer_norm(0, res0)
    layer_norm(1, res1)


def _chunk_masks(unit):
    t = np.arange(CHUNK)
    same = (t[:, None] // unit) == (t[None, :] // unit)
    incl = same & (t[None, :] <= t[:, None])
    return jnp.asarray(np.concatenate([incl, same], axis=0), BF16)


def _block_diag(blocks):
    *lead, n, r, c = blocks.shape
    eye = jnp.eye(n, dtype=blocks.dtype)
    out = eye[:, None, :, None] * blocks[..., :, :, None, :]
    return out.reshape(*lead, n * r, n * c)


def _prep_params(w_in, w_out, rg_gate_a_w, rg_gate_x_w, s5_lambda_re, s5_lambda_im, s5_log_dt, s5_b_re, s5_b_im,
                 s5_c_re, s5_c_im, s5_glu_w, **small):
    assert w_in.shape[-1] == IN_COLS
    out = dict(small)
    out.update(
        w_in=w_in.astype(BF16),
        w_out=w_out.astype(BF16),
        wg=jnp.concatenate([_block_diag(rg_gate_a_w), _block_diag(rg_gate_x_w)], axis=-1).astype(BF16),
        s5v=jnp.stack([s5_lambda_re.reshape(DEPTH, S5N), s5_lambda_im.reshape(DEPTH, S5N),
                       jnp.repeat(s5_log_dt, S5_STATE, axis=-1)], axis=1),
        bre=jnp.swapaxes(s5_b_re, -1, -2).reshape(DEPTH, S5_GROUPS * S5_GROUP, S5_STATE),
        bim=jnp.swapaxes(s5_b_im, -1, -2).reshape(DEPTH, S5_GROUPS * S5_GROUP, S5_STATE),
        cre=jnp.swapaxes(s5_c_re, -1, -2).reshape(DEPTH, S5N, S5_GROUP),
        cim=jnp.swapaxes(s5_c_im, -1, -2).reshape(DEPTH, S5N, S5_GROUP),
        glu_w=s5_glu_w.astype(BF16),
    )
    return [out[k] for k in _PARAM_NAMES]


def _layer_call(l, x, states, params, nseq, lc):
    bsz, seqlen, _ = x.shape
    rows = nseq * lc
    assert bsz % nseq == 0 and seqlen % lc == 0 and rows % CHUNK == 0 and lc % SUBLANES == 0 and nseq % SUBLANES == 0
    assert CHUNK % min(lc, CHUNK) == 0 and lc % min(lc, CHUNK) == 0
    pitch = _scan_pitch(lc)
    grid = (bsz // nseq, seqlen // lc)
    state_shapes = [(CONV_W - 1, BW), (CONV_W - 1, 2 * BW), (CONV_W - 1, 3 * BW), (BW,), (HD, BW), (HD, BW), (S5N,), (S5N,)]
    assert len(state_shapes) == _N_STATES

    def param_spec(name, a):
        if name in _LAYER_BLOCKS:
            return pl.BlockSpec((None,) + a.shape[1:], lambda i, j: (l,) + (0,) * (a.ndim - 1), pipeline_mode=pl.Buffered(1))
        return pl.BlockSpec(a.shape, lambda i, j: (0,) * a.ndim, pipeline_mode=pl.Buffered(1))

    lm = _chunk_masks(min(lc, CHUNK))
    x_spec = pl.BlockSpec((nseq, lc, D_MODEL), lambda i, j: (i, j, 0))
    state_args = [] if states is None else list(states)
    state_specs = [pl.BlockSpec((None, nseq) + s, lambda i, j, n=len(s): (l, i) + (0,) * n) for s in state_shapes]
    in_specs = ([x_spec] + (state_specs if state_args else [])
                + [param_spec(name, a) for name, a in zip(_PARAM_NAMES, params)]
                + [pl.BlockSpec(lm.shape, lambda i, j: (0, 0), pipeline_mode=pl.Buffered(1))])
    out_state_specs = [pl.BlockSpec((nseq,) + s, lambda i, j, n=len(s): (i,) + (0,) * n) for s in state_shapes]
    out_shape = [jax.ShapeDtypeStruct(x.shape, F32)] + [jax.ShapeDtypeStruct((bsz,) + s, F32) for s in state_shapes]
    scratch = [
        pltpu.VMEM((nseq, HIST + lc, NCONV), F32),
        pltpu.VMEM((rows, NREST), F32),
        pltpu.VMEM((rows, 4 * BW), BF16),
        pltpu.VMEM((S5_GROUPS * S5_GROUP, 2 * S5N), BF16),
        pltpu.VMEM((2 * S5N, S5_GROUPS * S5_GROUP), BF16),
        pltpu.VMEM((SUBLANES, S5N), F32),
        pltpu.VMEM((4, nseq * pitch, LANES), F32),
        pltpu.VMEM((2 * S5N // LANES, rows, LANES), F32),
        pltpu.VMEM((BW // LANES, nseq * pitch, LANES), F32),
        pltpu.VMEM((BW // LANES, rows, LANES), F32),
        pltpu.VMEM((D_MODEL, NCONV), BF16),
        pltpu.VMEM((D_MODEL, NREST), BF16),
    ]
    outs = pl.pallas_call(
        functools.partial(_layer_kernel, l, nseq, lc, not state_args),
        grid=grid,
        in_specs=in_specs,
        out_specs=[x_spec] + out_state_specs,
        out_shape=out_shape,
        scratch_shapes=scratch,
        compiler_params=pltpu.CompilerParams(dimension_semantics=("arbitrary", "arbitrary"),
                                             vmem_limit_bytes=VMEM_LIMIT_BYTES),
    )(x, *state_args, *params, lm)
    return outs[0], outs[1:]


def _states_to_kernel(conv_a, h_a, conv_b, h_b, conv_c, s_c, s5_re, s5_im):
    d, bsz = h_a.shape[0], h_a.shape[1]
    ssd = jnp.transpose(h_b, (0, 1, 4, 2, 3)).reshape(d, bsz, HD, BW)
    gdn = jnp.transpose(s_c, (0, 1, 3, 2, 4)).reshape(d, bsz, HD, BW)
    return [conv_a, conv_b, conv_c, h_a, ssd, gdn, s5_re.reshape(d, bsz, S5N), s5_im.reshape(d, bsz, S5N)]


def _states_from_kernel(per_layer):
    conv_a, conv_b, conv_c, rg, ssd, gdn, s5r, s5i = (jnp.stack(t) for t in zip(*per_layer))
    d, bsz = rg.shape[0], rg.shape[1]
    return (conv_a, rg, conv_b,
            jnp.transpose(ssd.reshape(d, bsz, HD, NH, HD), (0, 1, 3, 4, 2)),
            conv_c,
            jnp.transpose(gdn.reshape(d, bsz, HD, NH, HD), (0, 1, 3, 2, 4)),
            s5r.reshape(d, bsz, S5_GROUPS, S5_STATE), s5i.reshape(d, bsz, S5_GROUPS, S5_STATE))


def kernel(x_prompt, x_sample, cache_rglru_conv, state_rglru, cache_ssd_conv, state_ssd, cache_gdn_conv, state_gdn,
           state_s5_re, state_s5_im, w_in, w_out, ln_g, ln_b, rg_conv_w, rg_conv_b, rg_gate_a_w, rg_gate_a_b,
           rg_gate_x_w, rg_gate_x_b, rg_lambda, ssd_conv_w, ssd_conv_b, ssd_dt_bias, ssd_a_log, ssd_d, ssd_norm_w,
           gdn_conv_w, gdn_conv_b, gdn_dt_bias, gdn_a_log, gdn_norm_w, s5_lambda_re, s5_lambda_im, s5_log_dt,
           s5_b_re, s5_b_im, s5_c_re, s5_c_im, s5_d, s5_glu_w, s5_glu_b):
    params = _prep_params(w_in, w_out, rg_gate_a_w, rg_gate_x_w, s5_lambda_re, s5_lambda_im, s5_log_dt, s5_b_re,
                          s5_b_im, s5_c_re, s5_c_im, s5_glu_w,
                          ln_g=ln_g, ln_b=ln_b, rg_conv_w=rg_conv_w, ssd_conv_w=ssd_conv_w, gdn_conv_w=gdn_conv_w,
                          rg_conv_b=rg_conv_b, ssd_conv_b=ssd_conv_b, gdn_conv_b=gdn_conv_b, rg_gate_a_b=rg_gate_a_b,
                          rg_gate_x_b=rg_gate_x_b, rg_lambda=rg_lambda, ssd_dt_bias=ssd_dt_bias, ssd_a_log=ssd_a_log,
                          ssd_d=ssd_d, ssd_norm_w=ssd_norm_w, gdn_dt_bias=gdn_dt_bias, gdn_a_log=gdn_a_log,
                          gdn_norm_w=gdn_norm_w, s5_d=s5_d, s5_glu_b=s5_glu_b)
    sample_states = _states_to_kernel(cache_rglru_conv, state_rglru, cache_ssd_conv, state_ssd, cache_gdn_conv,
                                      state_gdn, state_s5_re, state_s5_im)
    pb, pl_len = x_prompt.shape[0], x_prompt.shape[1]
    sl = x_sample.shape[1]
    lc_p = ROWS // pb
    assert pl_len % lc_p == 0
    yp, ys = x_prompt, x_sample
    p_new, s_new = [], []
    for l in range(DEPTH):
        yp, st_p = _layer_call(l, yp, None, params, nseq=pb, lc=lc_p)
        ys, st_s = _layer_call(l, ys, sample_states, params, nseq=SAMPLE_SEQS, lc=sl)
        p_new.append(st_p)
        s_new.append(st_s)
    return (yp, ys, *_states_from_kernel(p_new), *_states_from_kernel(s_new))
```

```python
import functools
import math

import numpy as np
import jax
import jax.numpy as jnp
from jax import lax
from jax.experimental import pallas as pl
from jax.experimental.pallas import tpu as pltpu

F32 = jnp.float32
BF16 = jnp.bfloat16

D_MODEL = 1024
DEPTH = 2
BW = 256
HD = 64
NH = BW // HD
SSD_GROUPS = 2
CONV_W = 4
HIST = 8
S5_GROUPS = 16
S5_GROUP = 16
S5_STATE = 64
S5N = S5_GROUPS * S5_STATE
LANES = 128
SUBLANES = 8
RG_C = 8.0
ALPHA = (2.0 * DEPTH) ** 0.25
CHUNK = 64
ROWS = 512
SAMPLE_SEQS = 32
VMEM_LIMIT_BYTES = 60 * 1024 * 1024

NCONV = 6 * BW
C_AX, C_BX, C_BBC, C_Q, C_K, C_V = (k * BW for k in range(6))
NREST = 6 * BW
R_GATE, R_DU, R_HEADS = 0, 4 * BW, 5 * BW
H_DT, H_BETA, H_DECAY = 0, NH, 2 * NH


def _dot(a, b):
    return jnp.dot(a, b, preferred_element_type=F32)


def _dot_nt(a, b):
    return lax.dot_general(a, b, (((1,), (1,)), ((), ())), preferred_element_type=F32)


def _sigmoid(x):
    return 0.5 * jnp.tanh(0.5 * x) + 0.5


def _silu(x):
    hx = 0.5 * x
    return hx * jnp.tanh(hx) + hx


def _softplus(x):
    return jnp.maximum(x, 0.0) + jnp.log1p(jnp.exp(-jnp.abs(x)))


def _split_bf16(x, pieces):
    out = []
    r = x
    for k in range(pieces):
        p = r.astype(BF16)
        out.append(p)
        if k + 1 < pieces:
            r = r - p.astype(F32)
    return out


EXACT_PIECES = 3


def _exact_left(mask_wide, x):
    return _dot(mask_wide, jnp.concatenate(_split_bf16(x, EXACT_PIECES), axis=0))


def _iota(shape, dim):
    return lax.broadcasted_iota(jnp.int32, shape, dim)


def _log2(n):
    k = int(round(math.log2(n)))
    assert 1 << k == n
    return k


def _bd(x, mask):
    return jnp.where(mask, jnp.tile(x, (NH, 1)), 0.0).astype(BF16)


def _diag_blocks(full, mask):
    fm = jnp.where(mask, full, 0.0)
    return fm[0:HD] + fm[HD:2 * HD] + fm[2 * HD:3 * HD] + fm[3 * HD:4 * HD]


def _scan_pitch(lc):
    return lc if (lc // SUBLANES) % 2 == 1 else lc + SUBLANES


_PARAM_NAMES = ('w_in', 'w_out', 'wg', 's5v', 'bre', 'bim', 'cre', 'cim', 'glu_w',
                'ln_g', 'ln_b', 'rg_conv_w', 'ssd_conv_w', 'gdn_conv_w', 'rg_conv_b', 'ssd_conv_b', 'gdn_conv_b',
                'rg_gate_a_b', 'rg_gate_x_b', 'rg_lambda', 'ssd_dt_bias', 'ssd_a_log', 'ssd_d', 'ssd_norm_w',
                'gdn_dt_bias', 'gdn_a_log', 'gdn_norm_w', 's5_d', 's5_glu_b')
_LAYER_BLOCKS = frozenset(_PARAM_NAMES[:9])
_N_STATES = 8
S5V_LRE, S5V_LIM, S5V_LOGDT = range(3)

_IN_COLS = {}
_c0 = 0
for _name, _w in (('a_x', BW), ('a_gate', BW), ('b_xbc', BW + 2 * SSD_GROUPS * HD), ('b_dt', NH), ('b_gate', BW),
                  ('c_qkv', 3 * BW), ('c_beta', NH), ('c_decay', NH), ('c_gate', BW), ('d_u', BW), ('d_gate', BW)):
    _IN_COLS[_name] = (_c0, _w)
    _c0 += _w
IN_COLS = _c0
_CONV_FROM = (('a_x', C_AX), ('b_xbc', C_BX), ('c_qkv', C_Q))
_REST_FROM = (('a_gate', R_GATE), ('b_gate', R_GATE + BW), ('c_gate', R_GATE + 2 * BW), ('d_gate', R_GATE + 3 * BW),
              ('d_u', R_DU), ('b_dt', R_HEADS + H_DT), ('c_beta', R_HEADS + H_BETA), ('c_decay', R_HEADS + H_DECAY))


def _layer_kernel(l, nseq, lc, zero_init, *refs):
    n_in = 0 if zero_init else _N_STATES
    x_ref = refs[0]
    state_in = refs[1:1 + n_in]
    prm = dict(zip(_PARAM_NAMES, refs[1 + n_in:1 + n_in + len(_PARAM_NAMES)]))
    rest_refs = refs[1 + n_in + len(_PARAM_NAMES):]
    lm_ref, y_ref = rest_refs[0], rest_refs[1]
    state_out = rest_refs[2:2 + _N_STATES]
    ca_out, cb_out, cc_out, rg_out, ssd_out, gdn_out, s5r_out, s5i_out = state_out
    zext, zr, mix, wb_s, wc_s, tab, rgbuf, s5buf, dubuf, ybuf, w_conv, w_rest = rest_refs[2 + _N_STATES:]
    w_in, w_out, wg, glu_w = (prm[k] for k in ('w_in', 'w_out', 'wg', 'glu_w'))
    s5v, bre, bim, cre, cim = (prm[k] for k in ('s5v', 'bre', 'bim', 'cre', 'cim'))

    rows = nseq * lc
    pitch = _scan_pitch(lc)
    ngroups = nseq // SUBLANES
    unit = min(lc, CHUNK)
    units = CHUNK // unit
    nchunks = rows // CHUNK
    first_call_step = (pl.program_id(0) == 0) & (pl.program_id(1) == 0)

    def vec(name):
        return prm[name][l:l + 1, :]

    def head_vec(name):
        r = vec(name)
        return jnp.concatenate([jnp.broadcast_to(r[:, h:h + 1], (1, HD)) for h in range(NH)], axis=1)

    @pl.when(first_call_step)
    def _():
        for name, dst in _CONV_FROM:
            c0, width = _IN_COLS[name]
            w_conv[:, dst:dst + width] = w_in[:, c0:c0 + width]
        used = R_HEADS + 3 * NH
        w_rest[:, used:NREST] = jnp.zeros((D_MODEL, NREST - used), BF16)
        for name, dst in _REST_FROM:
            c0, width = _IN_COLS[name]
            w_rest[:, dst:dst + width] = w_in[:, c0:c0 + width]
        lr = s5v[S5V_LRE:S5V_LRE + 1, :]
        li = s5v[S5V_LIM:S5V_LIM + 1, :]
        dt = jnp.exp(s5v[S5V_LOGDT:S5V_LOGDT + 1, :])
        mag = jnp.exp(lr * dt)
        ang = li * dt
        ar = mag * jnp.cos(ang)
        ai = mag * jnp.sin(ang)
        den = lr * lr + li * li
        fr = ((ar - 1.0) * lr + ai * li) / den
        fi = (ai * lr - (ar - 1.0) * li) / den
        def group_diag(blocks, rows_per_group, lanes_per_group):
            t = jnp.concatenate([blocks] * S5_GROUPS, axis=1)
            same = ((_iota(t.shape, 0) >> _log2(rows_per_group)) == (_iota(t.shape, 1) >> _log2(lanes_per_group)))
            return jnp.where(same, t, 0.0)

        b_re, b_im = (group_diag(r[...], S5_GROUP, S5_STATE) for r in (bre, bim))
        wb_s[:, 0:S5N] = (fr * b_re - fi * b_im).astype(BF16)
        wb_s[:, S5N:2 * S5N] = (fr * b_im + fi * b_re).astype(BF16)
        wc_s[0:S5N, :] = group_diag(cre[...], S5_STATE, S5_GROUP).astype(BF16)
        wc_s[S5N:2 * S5N, :] = (-group_diag(cim[...], S5_STATE, S5_GROUP)).astype(BF16)
        tab[0:1, :] = ar
        tab[1:2, :] = ai

    @pl.when(pl.program_id(1) == 0)
    def _():
        for k, out in enumerate(state_out):
            out[...] = jnp.zeros(out.shape, F32) if zero_init else state_in[k][...]

    hd_mask = (_iota((BW, BW), 0) >> _log2(HD)) == (_iota((BW, BW), 1) >> _log2(HD))
    bd = functools.partial(_bd, mask=hd_mask)

    h0 = HIST - (CONV_W - 1)

    conv_src = {C_AX: ('rg', 0), C_BX: ('ssd', 0), C_BBC: ('ssd', BW), C_Q: ('gdn', 0), C_K: ('gdn', BW),
                C_V: ('gdn', 2 * BW)}

    def conv(c0):
        branch, p0 = conv_src[c0]
        wts, bias = prm[branch + '_conv_w'], prm[branch + '_conv_b']
        ext = zext[:, :, c0:c0 + BW]
        acc = bias[l:l + 1, p0:p0 + BW] + wts[l, CONV_W - 1:CONV_W, p0:p0 + BW] * ext[:, HIST:, :]
        for back in range(1, CONV_W):
            tap = CONV_W - 1 - back
            acc = acc + wts[l, tap:tap + 1, p0:p0 + BW] * pltpu.roll(ext, back, 1)[:, HIST:, :]
        return acc.reshape(rows, BW)

    def rest(c0):
        return zr[:, c0:c0 + BW]

    def to_slabs(buf, first, val):
        for k in range(val.shape[1] // LANES):
            piece = val[:, k * LANES:(k + 1) * LANES]
            if pitch == lc:
                buf[first + k] = piece
            else:
                for q in range(nseq):
                    buf[first + k, q * pitch:q * pitch + lc, :] = piece[q * lc:(q + 1) * lc]

    def from_slabs(buf, first, n):
        cols = []
        for k in range(n):
            if pitch == lc:
                cols.append(buf[first + k])
            else:
                cols.append(jnp.concatenate([buf[first + k, q * pitch:q * pitch + lc, :] for q in range(nseq)], axis=0))
        return cols[0] if n == 1 else jnp.concatenate(cols, axis=1)

    def seq_rows(g, t):
        return pl.ds(g * SUBLANES * pitch + t, SUBLANES, stride=pitch)

    ones_bd = hd_mask.astype(BF16)

    def head_sum(v):
        return _dot(v.astype(BF16), ones_bd)

    lane = _iota((rows, LANES), 1)

    def head_rep(pair):
        rolled = pltpu.roll(pair, HD, 1)
        return jnp.concatenate([jnp.where(lane < HD, pair, rolled), jnp.where(lane < HD, rolled, pair)], axis=1)


    n_conv_tiles, n_rest_tiles = NCONV // BW, NREST // BW

    xb = x_ref[...].reshape(rows, D_MODEL).astype(BF16)
    rest_order = [R_GATE // BW, R_DU // BW, R_HEADS // BW, R_GATE // BW + 1, R_GATE // BW + 2, R_GATE // BW + 3]
    assert sorted(rest_order) == list(range(n_rest_tiles))

    def ahead(n=1):
        for _ in range(n):
            if rest_order:
                k = rest_order.pop(0)
                tile = _dot(xb, w_rest[:, k * BW:(k + 1) * BW])
                if k == R_DU // BW:
                    to_slabs(dubuf, 0, tile)
                else:
                    zr[:, k * BW:(k + 1) * BW] = tile

    def proj_conv_tile(k):
        zext[:, HIST:HIST + lc, k * BW:(k + 1) * BW] = _dot(xb, w_conv[:, k * BW:(k + 1) * BW]).reshape(nseq, lc, BW)

    zext[:, h0:HIST, C_AX:C_AX + BW] = ca_out[...]
    zext[:, h0:HIST, C_BX:C_BX + 2 * BW] = cb_out[...]
    zext[:, h0:HIST, C_Q:C_Q + 3 * BW] = cc_out[...]
    proj_conv_tile(0)
    proj_conv_tile(1)
    xc = conv(C_AX)
    proj_conv_tile(2)
    xs = _silu(conv(C_BX))
    proj_conv_tile(3)
    bc = _silu(conv(C_BBC))
    bm = head_rep(bc[:, 0:LANES])
    cm = head_rep(bc[:, LANES:2 * LANES])
    proj_conv_tile(4)
    qc = _silu(conv(C_Q))
    proj_conv_tile(5)
    kc = _silu(conv(C_K))
    ahead()
    vc = _silu(conv(C_V))
    ca_out[...] = zext[:, lc + h0:lc + HIST, C_AX:C_AX + BW]
    cb_out[...] = zext[:, lc + h0:lc + HIST, C_BX:C_BX + 2 * BW]
    cc_out[...] = zext[:, lc + h0:lc + HIST, C_Q:C_Q + 3 * BW]
    ahead()

    gts = _dot(xc.astype(BF16), wg[...])
    gate_r = _sigmoid(gts[:, 0:BW] + vec('rg_gate_a_b'))
    gate_i = _sigmoid(gts[:, BW:2 * BW] + vec('rg_gate_x_b'))
    log_a = (-RG_C * _softplus(-vec('rg_lambda'))) * gate_r
    a = jnp.exp(log_a)
    to_slabs(rgbuf, 0, a)
    to_slabs(rgbuf, 2, jnp.sqrt(-jnp.tanh(log_a) * (a * a + 1.0)) * (gate_i * xc))
    for g in range(ngroups):
        gs = slice(g * SUBLANES, (g + 1) * SUBLANES)
        h = [rg_out[gs, k * LANES:(k + 1) * LANES] for k in range(2)]
        for t in range(lc):
            idx = seq_rows(g, t)
            for k in range(2):
                h[k] = rgbuf[k, idx, :] * h[k] + rgbuf[2 + k, idx, :]
                rgbuf[2 + k, idx, :] = h[k]
        for k in range(2):
            rg_out[gs, k * LANES:(k + 1) * LANES] = h[k]
    ahead()
    mix[:, 0:BW] = (from_slabs(rgbuf, 2, 2) * _silu(rest(R_GATE))).astype(BF16)

    du = jnp.concatenate([jnp.concatenate([dubuf[k, seq_rows(g, t), :] for k in range(BW // LANES)], axis=1)
                          for g in range(ngroups) for t in range(lc)], axis=0)
    dub = du.astype(BF16)
    qn = qc * lax.rsqrt(head_sum(qc * qc) + 1e-6) * (HD ** -0.5)
    kn = kc * lax.rsqrt(head_sum(kc * kc) + 1e-6)
    for m in range(2 * S5N // BW):
        bu = _dot(dub, wb_s[:, m * BW:(m + 1) * BW])
        s5buf[2 * m] = bu[:, 0:LANES]
        s5buf[2 * m + 1] = bu[:, LANES:2 * LANES]
    ahead()
    hl = _iota((1, LANES), 1)

    def head_lanes(name_at):
        out = jnp.zeros((1, LANES), F32)
        for name, off in name_at:
            r = vec(name)
            for h in range(NH):
                out = jnp.where(hl == off + h, r[:, h:h + 1], out)
        return out

    def spread(tile, off):
        return jnp.concatenate([jnp.broadcast_to(tile[:, off + h:off + h + 1], (tile.shape[0], HD))
                                for h in range(NH)], axis=1)

    narrow = zr[:, R_HEADS:R_HEADS + LANES]
    sp = _softplus(narrow + head_lanes((('ssd_dt_bias', H_DT), ('gdn_dt_bias', H_DECAY))))
    rate = sp * (-jnp.exp(head_lanes((('ssd_a_log', H_DT), ('gdn_a_log', H_DECAY)))))
    dt = spread(sp, H_DT)
    da = spread(rate, H_DT)
    gdec = spread(rate, H_DECAY)
    beta = spread(_sigmoid(narrow), H_BETA)
    xdt = xs * dt
    vb = vc * beta

    nsl = S5N // LANES
    a_re = [jnp.broadcast_to(tab[0:1, k * LANES:(k + 1) * LANES], (SUBLANES, LANES)) for k in range(nsl)]
    a_im = [jnp.broadcast_to(tab[1:2, k * LANES:(k + 1) * LANES], (SUBLANES, LANES)) for k in range(nsl)]
    s5_state = {}

    def s5_step(g, t):
        gs = slice(g * SUBLANES, (g + 1) * SUBLANES)
        if t == 0:
            s5_state['r'] = [s5r_out[gs, k * LANES:(k + 1) * LANES] for k in range(nsl)]
            s5_state['i'] = [s5i_out[gs, k * LANES:(k + 1) * LANES] for k in range(nsl)]
        hr, hi = s5_state['r'], s5_state['i']
        r0 = (g * lc + t) * SUBLANES
        idx = slice(r0, r0 + SUBLANES)
        for k in range(nsl):
            nr = a_re[k] * hr[k] - a_im[k] * hi[k] + s5buf[k, idx, :]
            ni = a_re[k] * hi[k] + a_im[k] * hr[k] + s5buf[nsl + k, idx, :]
            hr[k], hi[k] = nr, ni
            s5buf[k, idx, :] = nr
            s5buf[nsl + k, idx, :] = ni
        if t == lc - 1:
            for k in range(nsl):
                s5r_out[gs, k * LANES:(k + 1) * LANES] = hr[k]
                s5i_out[gs, k * LANES:(k + 1) * LANES] = hi[k]

    s5_steps = [(g, t) for g in range(ngroups) for t in range(lc)]
    n_parts = 16
    s5_done = [0]

    def s5_part():
        per = -(-len(s5_steps) // n_parts)
        for g, t in s5_steps[s5_done[0]:s5_done[0] + per]:
            s5_step(g, t)
        s5_done[0] += per

    t_c = _iota((CHUNK, BW), 0)
    s_c = _iota((CHUNK, BW), 1) & (HD - 1)
    same_unit = (t_c >> _log2(unit)) == (s_c >> _log2(unit))
    valid_incl = same_unit & (s_c <= t_c)
    valid_strict = same_unit & (s_c < t_c)
    eye_cat = jnp.where(s_c == t_c, 1.0, 0.0)
    lm = lm_ref[...]
    if units == 1:
        lm = lm[0:CHUNK]
    chunks = range(nchunks)
    sls = [slice(c * CHUNK, (c + 1) * CHUNK) for c in chunks]

    strict_f = jnp.where(s_c < t_c, 1.0, 0.0)
    acum, atot, decay, dtot, lmat, eg = [], [], [], [], [], []
    for c in chunks:
        da_c, gd_c = da[sls[c]], gdec[sls[c]]
        cs = _exact_left(lm, jnp.concatenate([rate[sls[c]], da_c * strict_f, gd_c * strict_f], axis=1))
        acum.append(spread(cs[0:CHUNK, 0:LANES], H_DT))
        decay.append(spread(cs[0:CHUNK, 0:LANES], H_DECAY))
        lmat.append(jnp.where(valid_incl, jnp.exp(jnp.minimum(cs[0:CHUNK, LANES:LANES + BW], 0.0)), 0.0))
        eg.append(jnp.where(valid_incl, jnp.exp(jnp.minimum(cs[0:CHUNK, LANES + BW:LANES + 2 * BW], 0.0)), 0.0))
        tot = cs[CHUNK - 1:CHUNK, 0:LANES] if units == 1 else cs[CHUNK:2 * CHUNK, 0:LANES]
        atot.append(spread(tot, H_DT))
        dtot.append(spread(tot, H_DECAY))
    s5_part()
    ahead()

    def seq_of(c, u):
        return (c * CHUNK + u * unit) // lc

    def unit_rows(u):
        return slice(u * unit, (u + 1) * unit)

    def only_unit(v, u):
        return v if units == 1 else jnp.where((t_c >> _log2(unit)) == u, v, 0.0)

    def unit_total(tot_c, u):
        r0 = u * unit if units > 1 else 0
        return tot_c[r0:r0 + 1, :]

    def cat_rows(parts):
        return parts[0] if len(parts) == 1 else jnp.concatenate(parts, axis=0)

    cbm = [_dot_nt(cm[sls[c]].astype(BF16), bd(bm[sls[c]])) for c in chunks]
    s5_part()
    y_ssd = [_dot((cbm[c] * lmat[c]).astype(BF16), bd(xdt[sls[c]])) for c in chunks]
    s5_part()
    ahead()
    ht = [[ssd_out[seq_of(c, u)] for u in range(units)] for c in chunks]
    cme = [(cm[sls[c]] * jnp.exp(acum[c])).astype(BF16) for c in chunks]
    y_int = [cat_rows([_dot(cme[c][unit_rows(u)], bd(ht[c][u])) for u in range(units)]) for c in chunks]
    s5_part()
    wbt = [(bm[sls[c]] * jnp.exp(atot[c] - acum[c])).T.astype(BF16) for c in chunks]
    upd = [[_dot(wbt[c], only_unit(xdt[sls[c]], u).astype(BF16)) for u in range(units)] for c in chunks]
    for c in chunks:
        for u in range(units):
            ssd_out[seq_of(c, u)] = jnp.exp(unit_total(atot[c], u)) * ht[c][u] + _diag_blocks(upd[c][u], hd_mask)
    s5_part()
    ahead()

    qk_kk = [_dot_nt(jnp.concatenate([qn[sls[c]], kn[sls[c]]], axis=0).astype(BF16), bd(kn[sls[c]])) for c in chunks]
    s5_part()
    qkg = [qk_kk[c][0:CHUNK] * eg[c] for c in chunks]
    mm = [jnp.where(valid_strict, beta[sls[c]] * qk_kk[c][CHUNK:2 * CHUNK] * eg[c], 0.0) for c in chunks]
    rm = [eye_cat - mm[c] for c in chunks]
    pw = [_dot(mm[c].astype(BF16), bd(mm[c])) for c in chunks]
    s5_part()
    ahead()
    for _step in range(_log2(unit) - 2):
        pr2 = [_dot(jnp.concatenate([pw[c], rm[c]], axis=0).astype(BF16), bd(pw[c])) for c in chunks]
        pw = [pr2[c][0:CHUNK] for c in chunks]
        rm = [rm[c] + pr2[c][CHUNK:2 * CHUNK] for c in chunks]
        s5_part()
    rm = [(rm[c] + _dot(rm[c].astype(BF16), bd(pw[c]))).astype(BF16) for c in chunks]
    s5_part()
    ahead()
    edec = [jnp.exp(decay[c]) for c in chunks]
    value = [_dot(rm[c], bd(vb[sls[c]])) for c in chunks]
    kcum = [_dot(rm[c], bd(kn[sls[c]] * beta[sls[c]] * edec[c])) for c in chunks]
    s5_part()
    sq = [[gdn_out[seq_of(c, u)] for u in range(units)] for c in chunks]
    qdec = [qn[sls[c]] * edec[c] for c in chunks]
    kq = [[_dot(jnp.concatenate([kcum[c][unit_rows(u)], qdec[c][unit_rows(u)]], axis=0).astype(BF16), bd(sq[c][u]))
           for u in range(units)] for c in chunks]
    s5_part()
    ahead()
    wv = [cat_rows([value[c][unit_rows(u)] - kq[c][u][0:unit] for u in range(units)]) for c in chunks]
    o_chunks = [cat_rows([kq[c][u][unit:2 * unit] for u in range(units)]) + _dot(qkg[c].astype(BF16), bd(wv[c]))
                for c in chunks]
    kdec_t = [(kn[sls[c]] * jnp.exp(dtot[c] - decay[c])).T.astype(BF16) for c in chunks]
    upd = [[_dot(kdec_t[c], only_unit(wv[c], u).astype(BF16)) for u in range(units)] for c in chunks]
    for c in chunks:
        for u in range(units):
            gdn_out[seq_of(c, u)] = jnp.exp(unit_total(dtot[c], u)) * sq[c][u] + _diag_blocks(upd[c][u], hd_mask)
    while s5_done[0] < len(s5_steps):
        s5_part()
    ahead(n_conv_tiles + n_rest_tiles)

    def s5_states(m):
        return jnp.concatenate([s5buf[2 * m], s5buf[2 * m + 1]], axis=1).astype(BF16)

    y5 = vec('s5_d') * du
    for m in range(S5N // BW):
        y5 = y5 + _dot(s5_states(m), wc_s[m * BW:(m + 1) * BW, :])
    yb = jnp.concatenate([y_ssd[c] + y_int[c] for c in chunks], axis=0) + head_vec('ssd_d') * xs
    yb = yb * _silu(rest(R_GATE + BW))
    yb = yb * lax.rsqrt(jnp.mean(yb * yb, axis=-1, keepdims=True) + 1e-6) * vec('ssd_norm_w')
    mix[:, BW:2 * BW] = yb.astype(BF16)
    for m in range(S5N // BW, 2 * S5N // BW):
        y5 = y5 + _dot(s5_states(m), wc_s[m * BW:(m + 1) * BW, :])
    o = jnp.concatenate(o_chunks, axis=0)
    gdn_norm = jnp.concatenate([vec('gdn_norm_w')] * NH, axis=1)
    o = o * lax.rsqrt(head_sum(o * o) * (1.0 / HD) + 1e-6) * gdn_norm
    mix[:, 2 * BW:3 * BW] = (o * _silu(rest(R_GATE + 2 * BW))).astype(BF16)
    y5 = 0.5 * y5 * (1.0 + lax.erf(y5 * math.sqrt(0.5)))
    y5 = y5 * _sigmoid(_dot(y5.astype(BF16), glu_w[...]) + vec('s5_glu_b'))
    for k in range(BW // LANES):
        ybuf[k] = y5[:, k * LANES:(k + 1) * LANES]
    y5 = jnp.concatenate(
        [jnp.concatenate([ybuf[k, pl.ds(g * lc * SUBLANES + s, lc, stride=SUBLANES), :] for k in range(BW // LANES)],
                         axis=1) for g in range(ngroups) for s in range(SUBLANES)], axis=0)
    mix[:, 3 * BW:4 * BW] = (y5 * _silu(rest(R_GATE + 3 * BW))).astype(BF16)

    half = rows // 2
    seq_half = nseq // 2

    def out_proj(hh):
        xh = x_ref[hh * seq_half:(hh + 1) * seq_half].reshape(half, D_MODEL)
        return ALPHA * xh + _dot(mix[hh * half:(hh + 1) * half, :], w_out[...])

    def layer_norm(hh, res):
        mu = jnp.mean(res, axis=-1, keepdims=True)
        rc = res - mu
        var = jnp.mean(rc * rc, axis=-1, keepdims=True)
        y = rc * lax.rsqrt(var + 1e-5) * vec('ln_g') + vec('ln_b')
        y_ref[hh * seq_half:(hh + 1) * seq_half] = y.reshape(seq_half, lc, D_MODEL)

    res0 = out_proj(0)
    res1 = out_proj(1)
    layer_norm(0, res0)
    layer_norm(1, res1)


def _chunk_masks(unit):
    t = np.arange(CHUNK)
    same = (t[:, None] // unit) == (t[None, :] // unit)
    incl = same & (t[None, :] <= t[:, None])
    return jnp.asarray(np.tile(np.concatenate([incl, same], axis=0), (1, EXACT_PIECES)), BF16)


def _block_diag(blocks):
    *lead, n, r, c = blocks.shape
    eye = jnp.eye(n, dtype=blocks.dtype)
    out = eye[:, None, :, None] * blocks[..., :, :, None, :]
    return out.reshape(*lead, n * r, n * c)


def _prep_params(w_in, w_out, rg_gate_a_w, rg_gate_x_w, s5_lambda_re, s5_lambda_im, s5_log_dt, s5_b_re, s5_b_im,
                 s5_c_re, s5_c_im, s5_glu_w, **small):
    assert w_in.shape[-1] == IN_COLS
    out = dict(small)
    out.update(
        w_in=w_in.astype(BF16),
        w_out=w_out.astype(BF16),
        wg=jnp.concatenate([_block_diag(rg_gate_a_w), _block_diag(rg_gate_x_w)], axis=-1).astype(BF16),
        s5v=jnp.stack([s5_lambda_re.reshape(DEPTH, S5N), s5_lambda_im.reshape(DEPTH, S5N),
                       jnp.repeat(s5_log_dt, S5_STATE, axis=-1)], axis=1),
        bre=jnp.swapaxes(s5_b_re, -1, -2).reshape(DEPTH, S5_GROUPS * S5_GROUP, S5_STATE),
        bim=jnp.swapaxes(s5_b_im, -1, -2).reshape(DEPTH, S5_GROUPS * S5_GROUP, S5_STATE),
        cre=jnp.swapaxes(s5_c_re, -1, -2).reshape(DEPTH, S5N, S5_GROUP),
        cim=jnp.swapaxes(s5_c_im, -1, -2).reshape(DEPTH, S5N, S5_GROUP),
        glu_w=s5_glu_w.astype(BF16),
    )
    return [out[k] for k in _PARAM_NAMES]


def _layer_call(l, x, states, params, nseq, lc):
    bsz, seqlen, _ = x.shape
    rows = nseq * lc
    assert bsz % nseq == 0 and seqlen % lc == 0 and rows % CHUNK == 0 and lc % SUBLANES == 0 and nseq % SUBLANES == 0
    assert CHUNK % min(lc, CHUNK) == 0 and lc % min(lc, CHUNK) == 0
    pitch = _scan_pitch(lc)
    grid = (bsz // nseq, seqlen // lc)
    state_shapes = [(CONV_W - 1, BW), (CONV_W - 1, 2 * BW), (CONV_W - 1, 3 * BW), (BW,), (HD, BW), (HD, BW), (S5N,), (S5N,)]
    assert len(state_shapes) == _N_STATES

    def param_spec(name, a):
        if name in _LAYER_BLOCKS:
            return pl.BlockSpec((None,) + a.shape[1:], lambda i, j: (l,) + (0,) * (a.ndim - 1), pipeline_mode=pl.Buffered(1))
        return pl.BlockSpec(a.shape, lambda i, j: (0,) * a.ndim, pipeline_mode=pl.Buffered(1))

    lm = _chunk_masks(min(lc, CHUNK))
    x_spec = pl.BlockSpec((nseq, lc, D_MODEL), lambda i, j: (i, j, 0))
    state_args = [] if states is None else list(states)
    state_specs = [pl.BlockSpec((None, nseq) + s, lambda i, j, n=len(s): (l, i) + (0,) * n) for s in state_shapes]
    in_specs = ([x_spec] + (state_specs if state_args else [])
                + [param_spec(name, a) for name, a in zip(_PARAM_NAMES, params)]
                + [pl.BlockSpec(lm.shape, lambda i, j: (0, 0), pipeline_mode=pl.Buffered(1))])
    out_state_specs = [pl.BlockSpec((nseq,) + s, lambda i, j, n=len(s): (i,) + (0,) * n) for s in state_shapes]
    out_shape = [jax.ShapeDtypeStruct(x.shape, F32)] + [jax.ShapeDtypeStruct((bsz,) + s, F32) for s in state_shapes]
    scratch = [
        pltpu.VMEM((nseq, HIST + lc, NCONV), F32),
        pltpu.VMEM((rows, NREST), F32),
        pltpu.VMEM((rows, 4 * BW), BF16),
        pltpu.VMEM((S5_GROUPS * S5_GROUP, 2 * S5N), BF16),
        pltpu.VMEM((2 * S5N, S5_GROUPS * S5_GROUP), BF16),
        pltpu.VMEM((SUBLANES, S5N), F32),
        pltpu.VMEM((4, nseq * pitch, LANES), F32),
        pltpu.VMEM((2 * S5N // LANES, rows, LANES), F32),
        pltpu.VMEM((BW // LANES, nseq * pitch, LANES), F32),
        pltpu.VMEM((BW // LANES, rows, LANES), F32),
        pltpu.VMEM((D_MODEL, NCONV), BF16),
        pltpu.VMEM((D_MODEL, NREST), BF16),
    ]
    outs = pl.pallas_call(
        functools.partial(_layer_kernel, l, nseq, lc, not state_args),
        grid=grid,
        in_specs=in_specs,
        out_specs=[x_spec] + out_state_specs,
        out_shape=out_shape,
        scratch_shapes=scratch,
        compiler_params=pltpu.CompilerParams(dimension_semantics=("arbitrary", "arbitrary"),
                                             vmem_limit_bytes=VMEM_LIMIT_BYTES),
    )(x, *state_args, *params, lm)
    return outs[0], outs[1:]


def _states_to_kernel(conv_a, h_a, conv_b, h_b, conv_c, s_c, s5_re, s5_im):
    d, bsz = h_a.shape[0], h_a.shape[1]
    ssd = jnp.transpose(h_b, (0, 1, 4, 2, 3)).reshape(d, bsz, HD, BW)
    gdn = jnp.transpose(s_c, (0, 1, 3, 2, 4)).reshape(d, bsz, HD, BW)
    return [conv_a, conv_b, conv_c, h_a, ssd, gdn, s5_re.reshape(d, bsz, S5N), s5_im.reshape(d, bsz, S5N)]


def _states_from_kernel(per_layer):
    conv_a, conv_b, conv_c, rg, ssd, gdn, s5r, s5i = (jnp.stack(t) for t in zip(*per_layer))
    d, bsz = rg.shape[0], rg.shape[1]
    return (conv_a, rg, conv_b,
            jnp.transpose(ssd.reshape(d, bsz, HD, NH, HD), (0, 1, 3, 4, 2)),
            conv_c,
            jnp.transpose(gdn.reshape(d, bsz, HD, NH, HD), (0, 1, 3, 2, 4)),
            s5r.reshape(d, bsz, S5_GROUPS, S5_STATE), s5i.reshape(d, bsz, S5_GROUPS, S5_STATE))


def kernel(x_prompt, x_sample, cache_rglru_conv, state_rglru, cache_ssd_conv, state_ssd, cache_gdn_conv, state_gdn,
           state_s5_re, state_s5_im, w_in, w_out, ln_g, ln_b, rg_conv_w, rg_conv_b, rg_gate_a_w, rg_gate_a_b,
           rg_gate_x_w, rg_gate_x_b, rg_lambda, ssd_conv_w, ssd_conv_b, ssd_dt_bias, ssd_a_log, ssd_d, ssd_norm_w,
           gdn_conv_w, gdn_conv_b, gdn_dt_bias, gdn_a_log, gdn_norm_w, s5_lambda_re, s5_lambda_im, s5_log_dt,
           s5_b_re, s5_b_im, s5_c_re, s5_c_im, s5_d, s5_glu_w, s5_glu_b):
    params = _prep_params(w_in, w_out, rg_gate_a_w, rg_gate_x_w, s5_lambda_re, s5_lambda_im, s5_log_dt, s5_b_re,
                          s5_b_im, s5_c_re, s5_c_im, s5_glu_w,
                          ln_g=ln_g, ln_b=ln_b, rg_conv_w=rg_conv_w, ssd_conv_w=ssd_conv_w, gdn_conv_w=gdn_conv_w,
                          rg_conv_b=rg_conv_b, ssd_conv_b=ssd_conv_b, gdn_conv_b=gdn_conv_b, rg_gate_a_b=rg_gate_a_b,
                          rg_gate_x_b=rg_gate_x_b, rg_lambda=rg_lambda, ssd_dt_bias=ssd_dt_bias, ssd_a_log=ssd_a_log,
                          ssd_d=ssd_d, ssd_norm_w=ssd_norm_w, gdn_dt_bias=gdn_dt_bias, gdn_a_log=gdn_a_log,
                          gdn_norm_w=gdn_norm_w, s5_d=s5_d, s5_glu_b=s5_glu_b)
    sample_states = _states_to_kernel(cache_rglru_conv, state_rglru, cache_ssd_conv, state_ssd, cache_gdn_conv,
                                      state_gdn, state_s5_re, state_s5_im)
    pb, pl_len = x_prompt.shape[0], x_prompt.shape[1]
    sl = x_sample.shape[1]
    lc_p = ROWS // pb
    assert pl_len % lc_p == 0
    yp, ys = x_prompt, x_sample
    p_new, s_new = [], []
    for l in range(DEPTH):
        yp, st_p = _layer_call(l, yp, None, params, nseq=pb, lc=lc_p)
        ys, st_s = _layer_call(l, ys, sample_states, params, nseq=SAMPLE_SEQS, lc=sl)
        p_new.append(st_p)
        s_new.append(st_s)
    return (yp, ys, *_states_from_kernel(p_new), *_states_from_kernel(s_new))
```

```python
import functools
import math

import numpy as np
import jax
import jax.numpy as jnp
from jax import lax
from jax.experimental import pallas as pl
from jax.experimental.pallas import tpu as pltpu

F32 = jnp.float32
BF16 = jnp.bfloat16

D_MODEL = 1024
DEPTH = 2
BW = 256
HD = 64
NH = BW // HD
SSD_GROUPS = 2
CONV_W = 4
HIST = 8
S5_GROUPS = 16
S5_GROUP = 16
S5_STATE = 64
S5N = S5_GROUPS * S5_STATE
LANES = 128
SUBLANES = 8
RG_C = 8.0
ALPHA = (2.0 * DEPTH) ** 0.25
CHUNK = 64
ROWS = 512
SAMPLE_SEQS = 32
VMEM_LIMIT_BYTES = 60 * 1024 * 1024

NCONV = 6 * BW
C_AX, C_BX, C_BBC, C_Q, C_K, C_V = (k * BW for k in range(6))
NREST = 6 * BW
R_GATE, R_DU, R_HEADS = 0, 4 * BW, 5 * BW
H_DT, H_BETA, H_DECAY = 0, NH, 2 * NH


def _dot(a, b):
    return jnp.dot(a, b, preferred_element_type=F32)


def _dot_nt(a, b):
    return lax.dot_general(a, b, (((1,), (1,)), ((), ())), preferred_element_type=F32)


def _sigmoid(x):
    return 0.5 * jnp.tanh(0.5 * x) + 0.5


def _silu(x):
    hx = 0.5 * x
    return hx * jnp.tanh(hx) + hx


def _softplus(x):
    return jnp.maximum(x, 0.0) + jnp.log1p(jnp.exp(-jnp.abs(x)))


def _split_bf16(x, pieces):
    out = []
    r = x
    for k in range(pieces):
        p = r.astype(BF16)
        out.append(p)
        if k + 1 < pieces:
            r = r - p.astype(F32)
    return out


EXACT_PIECES = 3


def _exact_left(mask_wide, x):
    return _dot(mask_wide, jnp.concatenate(_split_bf16(x, EXACT_PIECES), axis=0))


def _iota(shape, dim):
    return lax.broadcasted_iota(jnp.int32, shape, dim)


def _log2(n):
    k = int(round(math.log2(n)))
    assert 1 << k == n
    return k


def _bd(x, mask):
    return jnp.where(mask, jnp.tile(x, (NH, 1)), 0.0).astype(BF16)


def _diag_blocks(full, mask):
    fm = jnp.where(mask, full, 0.0)
    return fm[0:HD] + fm[HD:2 * HD] + fm[2 * HD:3 * HD] + fm[3 * HD:4 * HD]


def _scan_pitch(lc):
    return lc if (lc // SUBLANES) % 2 == 1 else lc + SUBLANES


_PARAM_NAMES = ('w_in', 'w_out', 'wg', 's5v', 'bre', 'bim', 'cre', 'cim', 'glu_w',
                'ln_g', 'ln_b', 'rg_conv_w', 'ssd_conv_w', 'gdn_conv_w', 'rg_conv_b', 'ssd_conv_b', 'gdn_conv_b',
                'rg_gate_a_b', 'rg_gate_x_b', 'rg_lambda', 'ssd_dt_bias', 'ssd_a_log', 'ssd_d', 'ssd_norm_w',
                'gdn_dt_bias', 'gdn_a_log', 'gdn_norm_w', 's5_d', 's5_glu_b')
_LAYER_BLOCKS = frozenset(_PARAM_NAMES[:9])
_N_STATES = 8
S5V_LRE, S5V_LIM, S5V_LOGDT = range(3)

_IN_COLS = {}
_c0 = 0
for _name, _w in (('a_x', BW), ('a_gate', BW), ('b_xbc', BW + 2 * SSD_GROUPS * HD), ('b_dt', NH), ('b_gate', BW),
                  ('c_qkv', 3 * BW), ('c_beta', NH), ('c_decay', NH), ('c_gate', BW), ('d_u', BW), ('d_gate', BW)):
    _IN_COLS[_name] = (_c0, _w)
    _c0 += _w
IN_COLS = _c0
_CONV_FROM = (('a_x', C_AX), ('b_xbc', C_BX), ('c_qkv', C_Q))
_REST_FROM = (('a_gate', R_GATE), ('b_gate', R_GATE + BW), ('c_gate', R_GATE + 2 * BW), ('d_gate', R_GATE + 3 * BW),
              ('d_u', R_DU), ('b_dt', R_HEADS + H_DT), ('c_beta', R_HEADS + H_BETA), ('c_decay', R_HEADS + H_DECAY))


def _layer_kernel(l, nseq, lc, zero_init, *refs):
    n_in = 0 if zero_init else _N_STATES
    x_ref = refs[0]
    state_in = refs[1:1 + n_in]
    prm = dict(zip(_PARAM_NAMES, refs[1 + n_in:1 + n_in + len(_PARAM_NAMES)]))
    rest_refs = refs[1 + n_in + len(_PARAM_NAMES):]
    lm_ref, y_ref = rest_refs[0], rest_refs[1]
    state_out = rest_refs[2:2 + _N_STATES]
    ca_out, cb_out, cc_out, rg_out, ssd_out, gdn_out, s5r_out, s5i_out = state_out
    zext, zr, mix, wb_s, wc_s, tab, rgbuf, s5buf, dubuf, ybuf, w_conv, w_rest = rest_refs[2 + _N_STATES:]
    w_in, w_out, wg, glu_w = (prm[k] for k in ('w_in', 'w_out', 'wg', 'glu_w'))
    s5v, bre, bim, cre, cim = (prm[k] for k in ('s5v', 'bre', 'bim', 'cre', 'cim'))

    rows = nseq * lc
    pitch = _scan_pitch(lc)
    ngroups = nseq // SUBLANES
    unit = min(lc, CHUNK)
    units = CHUNK // unit
    nchunks = rows // CHUNK
    first_call_step = (pl.program_id(0) == 0) & (pl.program_id(1) == 0)

    def vec(name):
        return prm[name][l:l + 1, :]

    def head_vec(name):
        r = vec(name)
        return jnp.concatenate([jnp.broadcast_to(r[:, h:h + 1], (1, HD)) for h in range(NH)], axis=1)

    @pl.when(first_call_step)
    def _():
        for name, dst in _CONV_FROM:
            c0, width = _IN_COLS[name]
            w_conv[:, dst:dst + width] = w_in[:, c0:c0 + width]
        used = R_HEADS + 3 * NH
        w_rest[:, used:NREST] = jnp.zeros((D_MODEL, NREST - used), BF16)
        for name, dst in _REST_FROM:
            c0, width = _IN_COLS[name]
            w_rest[:, dst:dst + width] = w_in[:, c0:c0 + width]
        lr = s5v[S5V_LRE:S5V_LRE + 1, :]
        li = s5v[S5V_LIM:S5V_LIM + 1, :]
        dt = jnp.exp(s5v[S5V_LOGDT:S5V_LOGDT + 1, :])
        mag = jnp.exp(lr * dt)
        ang = li * dt
        ar = mag * jnp.cos(ang)
        ai = mag * jnp.sin(ang)
        den = lr * lr + li * li
        fr = ((ar - 1.0) * lr + ai * li) / den
        fi = (ai * lr - (ar - 1.0) * li) / den
        def group_diag(blocks, rows_per_group, lanes_per_group):
            t = jnp.concatenate([blocks] * S5_GROUPS, axis=1)
            same = ((_iota(t.shape, 0) >> _log2(rows_per_group)) == (_iota(t.shape, 1) >> _log2(lanes_per_group)))
            return jnp.where(same, t, 0.0)

        b_re, b_im = (group_diag(r[...], S5_GROUP, S5_STATE) for r in (bre, bim))
        wb_s[:, 0:S5N] = (fr * b_re - fi * b_im).astype(BF16)
        wb_s[:, S5N:2 * S5N] = (fr * b_im + fi * b_re).astype(BF16)
        wc_s[0:S5N, :] = group_diag(cre[...], S5_STATE, S5_GROUP).astype(BF16)
        wc_s[S5N:2 * S5N, :] = (-group_diag(cim[...], S5_STATE, S5_GROUP)).astype(BF16)
        tab[0:1, :] = ar
        tab[1:2, :] = ai

    @pl.when(pl.program_id(1) == 0)
    def _():
        for k, out in enumerate(state_out):
            out[...] = jnp.zeros(out.shape, F32) if zero_init else state_in[k][...]

    hd_mask = (_iota((BW, BW), 0) >> _log2(HD)) == (_iota((BW, BW), 1) >> _log2(HD))
    bd = functools.partial(_bd, mask=hd_mask)

    h0 = HIST - (CONV_W - 1)

    conv_src = {C_AX: ('rg', 0), C_BX: ('ssd', 0), C_BBC: ('ssd', BW), C_Q: ('gdn', 0), C_K: ('gdn', BW),
                C_V: ('gdn', 2 * BW)}

    def conv(c0):
        branch, p0 = conv_src[c0]
        wts, bias = prm[branch + '_conv_w'], prm[branch + '_conv_b']
        ext = zext[:, :, c0:c0 + BW]
        def tap(k):
            return wts[l, k:k + 1, p0:p0 + BW]

        assert CONV_W == 4
        prev = pltpu.roll(ext, 1, 1)
        newer = tap(3) * ext[:, HIST:, :] + tap(2) * prev[:, HIST:, :]
        older = tap(1) * ext + tap(0) * prev
        acc = bias[l:l + 1, p0:p0 + BW] + newer + pltpu.roll(older, 2, 1)[:, HIST:, :]
        return acc.reshape(rows, BW)

    def rest(c0):
        return zr[:, c0:c0 + BW]

    def to_slabs(buf, first, val):
        for k in range(val.shape[1] // LANES):
            piece = val[:, k * LANES:(k + 1) * LANES]
            if pitch == lc:
                buf[first + k] = piece
            else:
                for q in range(nseq):
                    buf[first + k, q * pitch:q * pitch + lc, :] = piece[q * lc:(q + 1) * lc]

    def from_slabs(buf, first, n):
        cols = []
        for k in range(n):
            if pitch == lc:
                cols.append(buf[first + k])
            else:
                cols.append(jnp.concatenate([buf[first + k, q * pitch:q * pitch + lc, :] for q in range(nseq)], axis=0))
        return cols[0] if n == 1 else jnp.concatenate(cols, axis=1)

    def seq_rows(g, t):
        return pl.ds(g * SUBLANES * pitch + t, SUBLANES, stride=pitch)

    ones_bd = hd_mask.astype(BF16)

    def head_sum(v):
        return _dot(v.astype(BF16), ones_bd)

    lane = _iota((rows, LANES), 1)

    def head_rep(pair):
        rolled = pltpu.roll(pair, HD, 1)
        return jnp.concatenate([jnp.where(lane < HD, pair, rolled), jnp.where(lane < HD, rolled, pair)], axis=1)


    n_conv_tiles, n_rest_tiles = NCONV // BW, NREST // BW

    xb = x_ref[...].reshape(rows, D_MODEL).astype(BF16)
    rest_order = [R_GATE // BW, R_DU // BW, R_HEADS // BW, R_GATE // BW + 1, R_GATE // BW + 2, R_GATE // BW + 3]
    assert sorted(rest_order) == list(range(n_rest_tiles))

    def ahead(n=1):
        for _ in range(n):
            if rest_order:
                k = rest_order.pop(0)
                tile = _dot(xb, w_rest[:, k * BW:(k + 1) * BW])
                if k == R_DU // BW:
                    to_slabs(dubuf, 0, tile)
                else:
                    zr[:, k * BW:(k + 1) * BW] = tile

    def proj_conv_tile(k):
        zext[:, HIST:HIST + lc, k * BW:(k + 1) * BW] = _dot(xb, w_conv[:, k * BW:(k + 1) * BW]).reshape(nseq, lc, BW)

    zext[:, h0:HIST, C_AX:C_AX + BW] = ca_out[...]
    zext[:, h0:HIST, C_BX:C_BX + 2 * BW] = cb_out[...]
    zext[:, h0:HIST, C_Q:C_Q + 3 * BW] = cc_out[...]
    proj_conv_tile(0)
    proj_conv_tile(1)
    xc = conv(C_AX)
    proj_conv_tile(2)
    xs = _silu(conv(C_BX))
    proj_conv_tile(3)
    bc = _silu(conv(C_BBC))
    bm = head_rep(bc[:, 0:LANES])
    cm = head_rep(bc[:, LANES:2 * LANES])
    proj_conv_tile(4)
    qc = _silu(conv(C_Q))
    proj_conv_tile(5)
    kc = _silu(conv(C_K))
    ahead()
    vc = _silu(conv(C_V))
    ca_out[...] = zext[:, lc + h0:lc + HIST, C_AX:C_AX + BW]
    cb_out[...] = zext[:, lc + h0:lc + HIST, C_BX:C_BX + 2 * BW]
    cc_out[...] = zext[:, lc + h0:lc + HIST, C_Q:C_Q + 3 * BW]
    ahead()

    gts = _dot(xc.astype(BF16), wg[...])
    gate_r = _sigmoid(gts[:, 0:BW] + vec('rg_gate_a_b'))
    gate_i = _sigmoid(gts[:, BW:2 * BW] + vec('rg_gate_x_b'))
    log_a = (-RG_C * _softplus(-vec('rg_lambda'))) * gate_r
    a = jnp.exp(log_a)
    to_slabs(rgbuf, 0, a)
    to_slabs(rgbuf, 2, jnp.sqrt(-jnp.tanh(log_a) * (a * a + 1.0)) * (gate_i * xc))
    for g in range(ngroups):
        gs = slice(g * SUBLANES, (g + 1) * SUBLANES)
        h = [rg_out[gs, k * LANES:(k + 1) * LANES] for k in range(2)]
        for t in range(lc):
            idx = seq_rows(g, t)
            for k in range(2):
                h[k] = rgbuf[k, idx, :] * h[k] + rgbuf[2 + k, idx, :]
                rgbuf[2 + k, idx, :] = h[k]
        for k in range(2):
            rg_out[gs, k * LANES:(k + 1) * LANES] = h[k]
    ahead()
    mix[:, 0:BW] = (from_slabs(rgbuf, 2, 2) * _silu(rest(R_GATE))).astype(BF16)

    du = jnp.concatenate([jnp.concatenate([dubuf[k, seq_rows(g, t), :] for k in range(BW // LANES)], axis=1)
                          for g in range(ngroups) for t in range(lc)], axis=0)
    dub = du.astype(BF16)
    qn = qc * lax.rsqrt(head_sum(qc * qc) + 1e-6) * (HD ** -0.5)
    kn = kc * lax.rsqrt(head_sum(kc * kc) + 1e-6)
    for m in range(2 * S5N // BW):
        bu = _dot(dub, wb_s[:, m * BW:(m + 1) * BW])
        s5buf[2 * m] = bu[:, 0:LANES]
        s5buf[2 * m + 1] = bu[:, LANES:2 * LANES]
    ahead()
    hl = _iota((1, LANES), 1)

    def head_lanes(name_at):
        out = jnp.zeros((1, LANES), F32)
        for name, off in name_at:
            r = vec(name)
            for h in range(NH):
                out = jnp.where(hl == off + h, r[:, h:h + 1], out)
        return out

    def spread(tile, off):
        return jnp.concatenate([jnp.broadcast_to(tile[:, off + h:off + h + 1], (tile.shape[0], HD))
                                for h in range(NH)], axis=1)

    narrow = zr[:, R_HEADS:R_HEADS + LANES]
    sp = _softplus(narrow + head_lanes((('ssd_dt_bias', H_DT), ('gdn_dt_bias', H_DECAY))))
    rate = sp * (-jnp.exp(head_lanes((('ssd_a_log', H_DT), ('gdn_a_log', H_DECAY)))))
    dt = spread(sp, H_DT)
    da = spread(rate, H_DT)
    gdec = spread(rate, H_DECAY)
    beta = spread(_sigmoid(narrow), H_BETA)
    xdt = xs * dt
    vb = vc * beta

    nsl = S5N // LANES
    a_re = [jnp.broadcast_to(tab[0:1, k * LANES:(k + 1) * LANES], (SUBLANES, LANES)) for k in range(nsl)]
    a_im = [jnp.broadcast_to(tab[1:2, k * LANES:(k + 1) * LANES], (SUBLANES, LANES)) for k in range(nsl)]
    s5_state = {}

    def s5_step(g, t):
        gs = slice(g * SUBLANES, (g + 1) * SUBLANES)
        if t == 0:
            s5_state['r'] = [s5r_out[gs, k * LANES:(k + 1) * LANES] for k in range(nsl)]
            s5_state['i'] = [s5i_out[gs, k * LANES:(k + 1) * LANES] for k in range(nsl)]
        hr, hi = s5_state['r'], s5_state['i']
        r0 = (g * lc + t) * SUBLANES
        idx = slice(r0, r0 + SUBLANES)
        for k in range(nsl):
            nr = a_re[k] * hr[k] - a_im[k] * hi[k] + s5buf[k, idx, :]
            ni = a_re[k] * hi[k] + a_im[k] * hr[k] + s5buf[nsl + k, idx, :]
            hr[k], hi[k] = nr, ni
            s5buf[k, idx, :] = nr
            s5buf[nsl + k, idx, :] = ni
        if t == lc - 1:
            for k in range(nsl):
                s5r_out[gs, k * LANES:(k + 1) * LANES] = hr[k]
                s5i_out[gs, k * LANES:(k + 1) * LANES] = hi[k]

    s5_steps = [(g, t) for g in range(ngroups) for t in range(lc)]
    n_parts = 16
    s5_done = [0]

    def s5_part():
        per = -(-len(s5_steps) // n_parts)
        for g, t in s5_steps[s5_done[0]:s5_done[0] + per]:
            s5_step(g, t)
        s5_done[0] += per

    t_c = _iota((CHUNK, BW), 0)
    s_c = _iota((CHUNK, BW), 1) & (HD - 1)
    same_unit = (t_c >> _log2(unit)) == (s_c >> _log2(unit))
    valid_incl = same_unit & (s_c <= t_c)
    valid_strict = same_unit & (s_c < t_c)
    eye_cat = jnp.where(s_c == t_c, 1.0, 0.0)
    lm = lm_ref[...]
    if units == 1:
        lm = lm[0:CHUNK]
    chunks = range(nchunks)
    sls = [slice(c * CHUNK, (c + 1) * CHUNK) for c in chunks]

    strict_f = jnp.where(s_c < t_c, 1.0, 0.0)
    acum, atot, decay, dtot, lmat, eg = [], [], [], [], [], []
    for c in chunks:
        da_c, gd_c = da[sls[c]], gdec[sls[c]]
        cs = _exact_left(lm, jnp.concatenate([rate[sls[c]], da_c * strict_f, gd_c * strict_f], axis=1))
        acum.append(spread(cs[0:CHUNK, 0:LANES], H_DT))
        decay.append(spread(cs[0:CHUNK, 0:LANES], H_DECAY))
        lmat.append(jnp.where(valid_incl, jnp.exp(jnp.minimum(cs[0:CHUNK, LANES:LANES + BW], 0.0)), 0.0))
        eg.append(jnp.where(valid_incl, jnp.exp(jnp.minimum(cs[0:CHUNK, LANES + BW:LANES + 2 * BW], 0.0)), 0.0))
        tot = cs[CHUNK - 1:CHUNK, 0:LANES] if units == 1 else cs[CHUNK:2 * CHUNK, 0:LANES]
        atot.append(spread(tot, H_DT))
        dtot.append(spread(tot, H_DECAY))
    s5_part()
    ahead()

    def seq_of(c, u):
        return (c * CHUNK + u * unit) // lc

    def unit_rows(u):
        return slice(u * unit, (u + 1) * unit)

    def only_unit(v, u):
        return v if units == 1 else jnp.where((t_c >> _log2(unit)) == u, v, 0.0)

    def unit_total(tot_c, u):
        r0 = u * unit if units > 1 else 0
        return tot_c[r0:r0 + 1, :]

    def cat_rows(parts):
        return parts[0] if len(parts) == 1 else jnp.concatenate(parts, axis=0)

    cbm = [_dot_nt(cm[sls[c]].astype(BF16), bd(bm[sls[c]])) for c in chunks]
    s5_part()
    y_ssd = [_dot((cbm[c] * lmat[c]).astype(BF16), bd(xdt[sls[c]])) for c in chunks]
    s5_part()
    ahead()
    ht = [[ssd_out[seq_of(c, u)] for u in range(units)] for c in chunks]
    cme = [(cm[sls[c]] * jnp.exp(acum[c])).astype(BF16) for c in chunks]
    y_int = [cat_rows([_dot(cme[c][unit_rows(u)], bd(ht[c][u])) for u in range(units)]) for c in chunks]
    s5_part()
    wbt = [(bm[sls[c]] * jnp.exp(atot[c] - acum[c])).T.astype(BF16) for c in chunks]
    upd = [[_dot(wbt[c], only_unit(xdt[sls[c]], u).astype(BF16)) for u in range(units)] for c in chunks]
    for c in chunks:
        for u in range(units):
            ssd_out[seq_of(c, u)] = jnp.exp(unit_total(atot[c], u)) * ht[c][u] + _diag_blocks(upd[c][u], hd_mask)
    s5_part()
    ahead()

    qk_kk = [_dot_nt(jnp.concatenate([qn[sls[c]], kn[sls[c]]], axis=0).astype(BF16), bd(kn[sls[c]])) for c in chunks]
    s5_part()
    qkg = [qk_kk[c][0:CHUNK] * eg[c] for c in chunks]
    mm = [jnp.where(valid_strict, beta[sls[c]] * qk_kk[c][CHUNK:2 * CHUNK] * eg[c], 0.0) for c in chunks]
    rm = [eye_cat - mm[c] for c in chunks]
    pw = [_dot(mm[c].astype(BF16), bd(mm[c])) for c in chunks]
    s5_part()
    ahead()
    for _step in range(_log2(unit) - 2):
        pr2 = [_dot(jnp.concatenate([pw[c], rm[c]], axis=0).astype(BF16), bd(pw[c])) for c in chunks]
        pw = [pr2[c][0:CHUNK] for c in chunks]
        rm = [rm[c] + pr2[c][CHUNK:2 * CHUNK] for c in chunks]
        s5_part()
    rm = [(rm[c] + _dot(rm[c].astype(BF16), bd(pw[c]))).astype(BF16) for c in chunks]
    s5_part()
    ahead()
    edec = [jnp.exp(decay[c]) for c in chunks]
    value = [_dot(rm[c], bd(vb[sls[c]])) for c in chunks]
    kcum = [_dot(rm[c], bd(kn[sls[c]] * beta[sls[c]] * edec[c])) for c in chunks]
    s5_part()
    sq = [[gdn_out[seq_of(c, u)] for u in range(units)] for c in chunks]
    qdec = [qn[sls[c]] * edec[c] for c in chunks]
    kq = [[_dot(jnp.concatenate([kcum[c][unit_rows(u)], qdec[c][unit_rows(u)]], axis=0).astype(BF16), bd(sq[c][u]))
           for u in range(units)] for c in chunks]
    s5_part()
    ahead()
    wv = [cat_rows([value[c][unit_rows(u)] - kq[c][u][0:unit] for u in range(units)]) for c in chunks]
    o_chunks = [cat_rows([kq[c][u][unit:2 * unit] for u in range(units)]) + _dot(qkg[c].astype(BF16), bd(wv[c]))
                for c in chunks]
    kdec_t = [(kn[sls[c]] * jnp.exp(dtot[c] - decay[c])).T.astype(BF16) for c in chunks]
    upd = [[_dot(kdec_t[c], only_unit(wv[c], u).astype(BF16)) for u in range(units)] for c in chunks]
    for c in chunks:
        for u in range(units):
            gdn_out[seq_of(c, u)] = jnp.exp(unit_total(dtot[c], u)) * sq[c][u] + _diag_blocks(upd[c][u], hd_mask)
    while s5_done[0] < len(s5_steps):
        s5_part()
    ahead(n_conv_tiles + n_rest_tiles)

    def s5_states(m):
        return jnp.concatenate([s5buf[2 * m], s5buf[2 * m + 1]], axis=1).astype(BF16)

    y5 = vec('s5_d') * du
    for m in range(S5N // BW):
        y5 = y5 + _dot(s5_states(m), wc_s[m * BW:(m + 1) * BW, :])
    yb = jnp.concatenate([y_ssd[c] + y_int[c] for c in chunks], axis=0) + head_vec('ssd_d') * xs
    yb = yb * _silu(rest(R_GATE + BW))
    yb = yb * lax.rsqrt(jnp.mean(yb * yb, axis=-1, keepdims=True) + 1e-6) * vec('ssd_norm_w')
    mix[:, BW:2 * BW] = yb.astype(BF16)
    for m in range(S5N // BW, 2 * S5N // BW):
        y5 = y5 + _dot(s5_states(m), wc_s[m * BW:(m + 1) * BW, :])
    o = jnp.concatenate(o_chunks, axis=0)
    gdn_norm = jnp.concatenate([vec('gdn_norm_w')] * NH, axis=1)
    o = o * lax.rsqrt(head_sum(o * o) * (1.0 / HD) + 1e-6) * gdn_norm
    mix[:, 2 * BW:3 * BW] = (o * _silu(rest(R_GATE + 2 * BW))).astype(BF16)
    y5 = 0.5 * y5 * (1.0 + lax.erf(y5 * math.sqrt(0.5)))
    y5 = y5 * _sigmoid(_dot(y5.astype(BF16), glu_w[...]) + vec('s5_glu_b'))
    for k in range(BW // LANES):
        ybuf[k] = y5[:, k * LANES:(k + 1) * LANES]
    y5 = jnp.concatenate(
        [jnp.concatenate([ybuf[k, pl.ds(g * lc * SUBLANES + s, lc, stride=SUBLANES), :] for k in range(BW // LANES)],
                         axis=1) for g in range(ngroups) for s in range(SUBLANES)], axis=0)
    mix[:, 3 * BW:4 * BW] = (y5 * _silu(rest(R_GATE + 3 * BW))).astype(BF16)

    half = rows // 2
    seq_half = nseq // 2

    def out_proj(hh):
        xh = x_ref[hh * seq_half:(hh + 1) * seq_half].reshape(half, D_MODEL)
        return ALPHA * xh + _dot(mix[hh * half:(hh + 1) * half, :], w_out[...])

    def layer_norm(hh, res):
        mu = jnp.mean(res, axis=-1, keepdims=True)
        rc = res - mu
        var = jnp.mean(rc * rc, axis=-1, keepdims=True)
        y = rc * lax.rsqrt(var + 1e-5) * vec('ln_g') + vec('ln_b')
        y_ref[hh * seq_half:(hh + 1) * seq_half] = y.reshape(seq_half, lc, D_MODEL)

    res0 = out_proj(0)
    res1 = out_proj(1)
    layer_norm(0, res0)
    layer_norm(1, res1)


def _chunk_masks(unit):
    t = np.arange(CHUNK)
    same = (t[:, None] // unit) == (t[None, :] // unit)
    incl = same & (t[None, :] <= t[:, None])
    return jnp.asarray(np.tile(np.concatenate([incl, same], axis=0), (1, EXACT_PIECES)), BF16)


def _block_diag(blocks):
    *lead, n, r, c = blocks.shape
    eye = jnp.eye(n, dtype=blocks.dtype)
    out = eye[:, None, :, None] * blocks[..., :, :, None, :]
    return out.reshape(*lead, n * r, n * c)


def _prep_params(w_in, w_out, rg_gate_a_w, rg_gate_x_w, s5_lambda_re, s5_lambda_im, s5_log_dt, s5_b_re, s5_b_im,
                 s5_c_re, s5_c_im, s5_glu_w, **small):
    assert w_in.shape[-1] == IN_COLS
    out = dict(small)
    out.update(
        w_in=w_in.astype(BF16),
        w_out=w_out.astype(BF16),
        wg=jnp.concatenate([_block_diag(rg_gate_a_w), _block_diag(rg_gate_x_w)], axis=-1).astype(BF16),
        s5v=jnp.stack([s5_lambda_re.reshape(DEPTH, S5N), s5_lambda_im.reshape(DEPTH, S5N),
                       jnp.repeat(s5_log_dt, S5_STATE, axis=-1)], axis=1),
        bre=jnp.swapaxes(s5_b_re, -1, -2).reshape(DEPTH, S5_GROUPS * S5_GROUP, S5_STATE),
        bim=jnp.swapaxes(s5_b_im, -1, -2).reshape(DEPTH, S5_GROUPS * S5_GROUP, S5_STATE),
        cre=jnp.swapaxes(s5_c_re, -1, -2).reshape(DEPTH, S5N, S5_GROUP),
        cim=jnp.swapaxes(s5_c_im, -1, -2).reshape(DEPTH, S5N, S5_GROUP),
        glu_w=s5_glu_w.astype(BF16),
    )
    return [out[k] for k in _PARAM_NAMES]


def _layer_call(l, x, states, params, nseq, lc):
    bsz, seqlen, _ = x.shape
    rows = nseq * lc
    assert bsz % nseq == 0 and seqlen % lc == 0 and rows % CHUNK == 0 and lc % SUBLANES == 0 and nseq % SUBLANES == 0
    assert CHUNK % min(lc, CHUNK) == 0 and lc % min(lc, CHUNK) == 0
    pitch = _scan_pitch(lc)
    grid = (bsz // nseq, seqlen // lc)
    state_shapes = [(CONV_W - 1, BW), (CONV_W - 1, 2 * BW), (CONV_W - 1, 3 * BW), (BW,), (HD, BW), (HD, BW), (S5N,), (S5N,)]
    assert len(state_shapes) == _N_STATES

    def param_spec(name, a):
        if name in _LAYER_BLOCKS:
            return pl.BlockSpec((None,) + a.shape[1:], lambda i, j: (l,) + (0,) * (a.ndim - 1), pipeline_mode=pl.Buffered(1))
        return pl.BlockSpec(a.shape, lambda i, j: (0,) * a.ndim, pipeline_mode=pl.Buffered(1))

    lm = _chunk_masks(min(lc, CHUNK))
    x_spec = pl.BlockSpec((nseq, lc, D_MODEL), lambda i, j: (i, j, 0))
    state_args = [] if states is None else list(states)
    state_specs = [pl.BlockSpec((None, nseq) + s, lambda i, j, n=len(s): (l, i) + (0,) * n) for s in state_shapes]
    in_specs = ([x_spec] + (state_specs if state_args else [])
                + [param_spec(name, a) for name, a in zip(_PARAM_NAMES, params)]
                + [pl.BlockSpec(lm.shape, lambda i, j: (0, 0), pipeline_mode=pl.Buffered(1))])
    out_state_specs = [pl.BlockSpec((nseq,) + s, lambda i, j, n=len(s): (i,) + (0,) * n) for s in state_shapes]
    out_shape = [jax.ShapeDtypeStruct(x.shape, F32)] + [jax.ShapeDtypeStruct((bsz,) + s, F32) for s in state_shapes]
    scratch = [
        pltpu.VMEM((nseq, HIST + lc, NCONV), F32),
        pltpu.VMEM((rows, NREST), F32),
        pltpu.VMEM((rows, 4 * BW), BF16),
        pltpu.VMEM((S5_GROUPS * S5_GROUP, 2 * S5N), BF16),
        pltpu.VMEM((2 * S5N, S5_GROUPS * S5_GROUP), BF16),
        pltpu.VMEM((SUBLANES, S5N), F32),
        pltpu.VMEM((4, nseq * pitch, LANES), F32),
        pltpu.VMEM((2 * S5N // LANES, rows, LANES), F32),
        pltpu.VMEM((BW // LANES, nseq * pitch, LANES), F32),
        pltpu.VMEM((BW // LANES, rows, LANES), F32),
        pltpu.VMEM((D_MODEL, NCONV), BF16),
        pltpu.VMEM((D_MODEL, NREST), BF16),
    ]
    outs = pl.pallas_call(
        functools.partial(_layer_kernel, l, nseq, lc, not state_args),
        grid=grid,
        in_specs=in_specs,
        out_specs=[x_spec] + out_state_specs,
        out_shape=out_shape,
        scratch_shapes=scratch,
        compiler_params=pltpu.CompilerParams(dimension_semantics=("arbitrary", "arbitrary"),
                                             vmem_limit_bytes=VMEM_LIMIT_BYTES),
    )(x, *state_args, *params, lm)
    return outs[0], outs[1:]


def _states_to_kernel(conv_a, h_a, conv_b, h_b, conv_c, s_c, s5_re, s5_im):
    d, bsz = h_a.shape[0], h_a.shape[1]
    ssd = jnp.transpose(h_b, (0, 1, 4, 2, 3)).reshape(d, bsz, HD, BW)
    gdn = jnp.transpose(s_c, (0, 1, 3, 2, 4)).reshape(d, bsz, HD, BW)
    return [conv_a, conv_b, conv_c, h_a, ssd, gdn, s5_re.reshape(d, bsz, S5N), s5_im.reshape(d, bsz, S5N)]


def _states_from_kernel(per_layer):
    conv_a, conv_b, conv_c, rg, ssd, gdn, s5r, s5i = (jnp.stack(t) for t in zip(*per_layer))
    d, bsz = rg.shape[0], rg.shape[1]
    return (conv_a, rg, conv_b,
            jnp.transpose(ssd.reshape(d, bsz, HD, NH, HD), (0, 1, 3, 4, 2)),
            conv_c,
            jnp.transpose(gdn.reshape(d, bsz, HD, NH, HD), (0, 1, 3, 2, 4)),
            s5r.reshape(d, bsz, S5_GROUPS, S5_STATE), s5i.reshape(d, bsz, S5_GROUPS, S5_STATE))


def kernel(x_prompt, x_sample, cache_rglru_conv, state_rglru, cache_ssd_conv, state_ssd, cache_gdn_conv, state_gdn,
           state_s5_re, state_s5_im, w_in, w_out, ln_g, ln_b, rg_conv_w, rg_conv_b, rg_gate_a_w, rg_gate_a_b,
           rg_gate_x_w, rg_gate_x_b, rg_lambda, ssd_conv_w, ssd_conv_b, ssd_dt_bias, ssd_a_log, ssd_d, ssd_norm_w,
           gdn_conv_w, gdn_conv_b, gdn_dt_bias, gdn_a_log, gdn_norm_w, s5_lambda_re, s5_lambda_im, s5_log_dt,
           s5_b_re, s5_b_im, s5_c_re, s5_c_im, s5_d, s5_glu_w, s5_glu_b):
    params = _prep_params(w_in, w_out, rg_gate_a_w, rg_gate_x_w, s5_lambda_re, s5_lambda_im, s5_log_dt, s5_b_re,
                          s5_b_im, s5_c_re, s5_c_im, s5_glu_w,
                          ln_g=ln_g, ln_b=ln_b, rg_conv_w=rg_conv_w, ssd_conv_w=ssd_conv_w, gdn_conv_w=gdn_conv_w,
                          rg_conv_b=rg_conv_b, ssd_conv_b=ssd_conv_b, gdn_conv_b=gdn_conv_b, rg_gate_a_b=rg_gate_a_b,
                          rg_gate_x_b=rg_gate_x_b, rg_lambda=rg_lambda, ssd_dt_bias=ssd_dt_bias, ssd_a_log=ssd_a_log,
                          ssd_d=ssd_d, ssd_norm_w=ssd_norm_w, gdn_dt_bias=gdn_dt_bias, gdn_a_log=gdn_a_log,
                          gdn_norm_w=gdn_norm_w, s5_d=s5_d, s5_glu_b=s5_glu_b)
    sample_states = _states_to_kernel(cache_rglru_conv, state_rglru, cache_ssd_conv, state_ssd, cache_gdn_conv,
                                      state_gdn, state_s5_re, state_s5_im)
    pb, pl_len = x_prompt.shape[0], x_prompt.shape[1]
    sl = x_sample.shape[1]
    lc_p = ROWS // pb
    assert pl_len % lc_p == 0
    yp, ys = x_prompt, x_sample
    p_new, s_new = [], []
    for l in range(DEPTH):
        yp, st_p = _layer_call(l, yp, None, params, nseq=pb, lc=lc_p)
        ys, st_s = _layer_call(l, ys, sample_states, params, nseq=SAMPLE_SEQS, lc=sl)
        p_new.append(st_p)
        s_new.append(st_s)
    return (yp, ys, *_states_from_kernel(p_new), *_states_from_kernel(s_new))
```

```python
import functools
import math

import numpy as np
import jax
import jax.numpy as jnp
from jax import lax
from jax.experimental import pallas as pl
from jax.experimental.pallas import tpu as pltpu

F32 = jnp.float32
BF16 = jnp.bfloat16

D_MODEL = 1024
DEPTH = 2
BW = 256
HD = 64
NH = BW // HD
SSD_GROUPS = 2
CONV_W = 4
HIST = 8
S5_GROUPS = 16
S5_GROUP = 16
S5_STATE = 64
S5N = S5_GROUPS * S5_STATE
LANES = 128
SUBLANES = 8
RG_C = 8.0
ALPHA = (2.0 * DEPTH) ** 0.25
CHUNK = 64
ROWS = 512
SAMPLE_SEQS = 32
VMEM_LIMIT_BYTES = 60 * 1024 * 1024

NCONV = 6 * BW
C_AX, C_BX, C_BBC, C_Q, C_K, C_V = (k * BW for k in range(6))
NREST = 6 * BW
R_GATE, R_DU, R_HEADS = 0, 4 * BW, 5 * BW
H_DT, H_BETA, H_DECAY = 0, NH, 2 * NH


def _dot(a, b):
    return jnp.dot(a, b, preferred_element_type=F32)


def _dot_nt(a, b):
    return lax.dot_general(a, b, (((1,), (1,)), ((), ())), preferred_element_type=F32)


def _sigmoid(x):
    return 0.5 * jnp.tanh(0.5 * x) + 0.5


def _silu(x):
    hx = 0.5 * x
    return hx * jnp.tanh(hx) + hx


def _softplus(x):
    return jnp.maximum(x, 0.0) + jnp.log1p(jnp.exp(-jnp.abs(x)))


def _split_bf16(x, pieces):
    out = []
    r = x
    for k in range(pieces):
        p = r.astype(BF16)
        out.append(p)
        if k + 1 < pieces:
            r = r - p.astype(F32)
    return out


EXACT_PIECES = 3


def _exact_left(mask_wide, x):
    return _dot(mask_wide, jnp.concatenate(_split_bf16(x, EXACT_PIECES), axis=0))


def _iota(shape, dim):
    return lax.broadcasted_iota(jnp.int32, shape, dim)


def _log2(n):
    k = int(round(math.log2(n)))
    assert 1 << k == n
    return k


def _bd(x, mask):
    return jnp.where(mask, jnp.tile(x, (NH, 1)), 0.0).astype(BF16)


def _diag_blocks(full, mask):
    fm = jnp.where(mask, full, 0.0)
    return fm[0:HD] + fm[HD:2 * HD] + fm[2 * HD:3 * HD] + fm[3 * HD:4 * HD]


def _scan_pitch(lc):
    return lc if (lc // SUBLANES) % 2 == 1 else lc + SUBLANES


_PARAM_NAMES = ('w_in', 'w_out', 'wg', 's5v', 'bre', 'bim', 'cre', 'cim', 'glu_w',
                'ln_g', 'ln_b', 'rg_conv_w', 'ssd_conv_w', 'gdn_conv_w', 'rg_conv_b', 'ssd_conv_b', 'gdn_conv_b',
                'rg_gate_a_b', 'rg_gate_x_b', 'rg_lambda', 'ssd_dt_bias', 'ssd_a_log', 'ssd_d', 'ssd_norm_w',
                'gdn_dt_bias', 'gdn_a_log', 'gdn_norm_w', 's5_d', 's5_glu_b')
_LAYER_BLOCKS = frozenset(_PARAM_NAMES[:9])
_N_STATES = 8
S5V_LRE, S5V_LIM, S5V_LOGDT = range(3)

_IN_COLS = {}
_c0 = 0
for _name, _w in (('a_x', BW), ('a_gate', BW), ('b_xbc', BW + 2 * SSD_GROUPS * HD), ('b_dt', NH), ('b_gate', BW),
                  ('c_qkv', 3 * BW), ('c_beta', NH), ('c_decay', NH), ('c_gate', BW), ('d_u', BW), ('d_gate', BW)):
    _IN_COLS[_name] = (_c0, _w)
    _c0 += _w
IN_COLS = _c0
_CONV_FROM = (('a_x', C_AX), ('b_xbc', C_BX), ('c_qkv', C_Q))
_REST_FROM = (('a_gate', R_GATE), ('b_gate', R_GATE + BW), ('c_gate', R_GATE + 2 * BW), ('d_gate', R_GATE + 3 * BW),
              ('d_u', R_DU), ('b_dt', R_HEADS + H_DT), ('c_beta', R_HEADS + H_BETA), ('c_decay', R_HEADS + H_DECAY))


def _layer_kernel(l, nseq, lc, zero_init, *refs):
    n_in = 0 if zero_init else _N_STATES
    x_ref = refs[0]
    state_in = refs[1:1 + n_in]
    prm = dict(zip(_PARAM_NAMES, refs[1 + n_in:1 + n_in + len(_PARAM_NAMES)]))
    rest_refs = refs[1 + n_in + len(_PARAM_NAMES):]
    lm_ref, y_ref = rest_refs[0], rest_refs[1]
    state_out = rest_refs[2:2 + _N_STATES]
    ca_out, cb_out, cc_out, rg_out, ssd_out, gdn_out, s5r_out, s5i_out = state_out
    zext, zr, mix, wb_s, wc_s, tab, rgbuf, s5buf, dubuf, ybuf, w_conv, w_rest = rest_refs[2 + _N_STATES:]
    w_in, w_out, wg, glu_w = (prm[k] for k in ('w_in', 'w_out', 'wg', 'glu_w'))
    s5v, bre, bim, cre, cim = (prm[k] for k in ('s5v', 'bre', 'bim', 'cre', 'cim'))

    rows = nseq * lc
    pitch = _scan_pitch(lc)
    ngroups = nseq // SUBLANES
    unit = min(lc, CHUNK)
    units = CHUNK // unit
    nchunks = rows // CHUNK
    first_call_step = (pl.program_id(0) == 0) & (pl.program_id(1) == 0)

    def vec(name):
        return prm[name][l:l + 1, :]

    def head_vec(name):
        r = vec(name)
        return jnp.concatenate([jnp.broadcast_to(r[:, h:h + 1], (1, HD)) for h in range(NH)], axis=1)

    @pl.when(first_call_step)
    def _():
        for name, dst in _CONV_FROM:
            c0, width = _IN_COLS[name]
            w_conv[:, dst:dst + width] = w_in[:, c0:c0 + width]
        used = R_HEADS + 3 * NH
        w_rest[:, used:NREST] = jnp.zeros((D_MODEL, NREST - used), BF16)
        for name, dst in _REST_FROM:
            c0, width = _IN_COLS[name]
            w_rest[:, dst:dst + width] = w_in[:, c0:c0 + width]
        lr = s5v[S5V_LRE:S5V_LRE + 1, :]
        li = s5v[S5V_LIM:S5V_LIM + 1, :]
        dt = jnp.exp(s5v[S5V_LOGDT:S5V_LOGDT + 1, :])
        mag = jnp.exp(lr * dt)
        ang = li * dt
        ar = mag * jnp.cos(ang)
        ai = mag * jnp.sin(ang)
        den = lr * lr + li * li
        fr = ((ar - 1.0) * lr + ai * li) / den
        fi = (ai * lr - (ar - 1.0) * li) / den
        def group_diag(blocks, rows_per_group, lanes_per_group):
            t = jnp.concatenate([blocks] * S5_GROUPS, axis=1)
            same = ((_iota(t.shape, 0) >> _log2(rows_per_group)) == (_iota(t.shape, 1) >> _log2(lanes_per_group)))
            return jnp.where(same, t, 0.0)

        b_re, b_im = (group_diag(r[...], S5_GROUP, S5_STATE) for r in (bre, bim))
        wb_s[:, 0:S5N] = (fr * b_re - fi * b_im).astype(BF16)
        wb_s[:, S5N:2 * S5N] = (fr * b_im + fi * b_re).astype(BF16)
        wc_s[0:S5N, :] = group_diag(cre[...], S5_STATE, S5_GROUP).astype(BF16)
        wc_s[S5N:2 * S5N, :] = (-group_diag(cim[...], S5_STATE, S5_GROUP)).astype(BF16)
        tab[0:1, :] = ar
        tab[1:2, :] = ai

    @pl.when(pl.program_id(1) == 0)
    def _():
        for k, out in enumerate(state_out):
            out[...] = jnp.zeros(out.shape, F32) if zero_init else state_in[k][...]

    hd_mask = (_iota((BW, BW), 0) >> _log2(HD)) == (_iota((BW, BW), 1) >> _log2(HD))
    bd = functools.partial(_bd, mask=hd_mask)

    h0 = HIST - (CONV_W - 1)

    conv_src = {C_AX: ('rg', 0), C_BX: ('ssd', 0), C_BBC: ('ssd', BW), C_Q: ('gdn', 0), C_K: ('gdn', BW),
                C_V: ('gdn', 2 * BW)}

    def conv(c0):
        branch, p0 = conv_src[c0]
        wts, bias = prm[branch + '_conv_w'], prm[branch + '_conv_b']
        ext = zext[:, :, c0:c0 + BW]
        def tap(k):
            return wts[l, k:k + 1, p0:p0 + BW]

        assert CONV_W == 4
        prev = pltpu.roll(ext, 1, 1)
        newer = tap(3) * ext[:, HIST:, :] + tap(2) * prev[:, HIST:, :]
        older = tap(1) * ext + tap(0) * prev
        acc = bias[l:l + 1, p0:p0 + BW] + newer + pltpu.roll(older, 2, 1)[:, HIST:, :]
        return acc.reshape(rows, BW)

    def rest(c0):
        return zr[:, c0:c0 + BW]

    def to_slabs(buf, first, val):
        for k in range(val.shape[1] // LANES):
            piece = val[:, k * LANES:(k + 1) * LANES]
            if pitch == lc:
                buf[first + k] = piece
            else:
                for q in range(nseq):
                    buf[first + k, q * pitch:q * pitch + lc, :] = piece[q * lc:(q + 1) * lc]

    def from_slabs(buf, first, n):
        cols = []
        for k in range(n):
            if pitch == lc:
                cols.append(buf[first + k])
            else:
                cols.append(jnp.concatenate([buf[first + k, q * pitch:q * pitch + lc, :] for q in range(nseq)], axis=0))
        return cols[0] if n == 1 else jnp.concatenate(cols, axis=1)

    def seq_rows(g, t):
        return pl.ds(g * SUBLANES * pitch + t, SUBLANES, stride=pitch)

    ones_bd = hd_mask.astype(BF16)

    def head_sum(v):
        return _dot(v.astype(BF16), ones_bd)

    lane = _iota((rows, LANES), 1)

    def head_rep(pair):
        rolled = pltpu.roll(pair, HD, 1)
        return jnp.concatenate([jnp.where(lane < HD, pair, rolled), jnp.where(lane < HD, rolled, pair)], axis=1)


    n_conv_tiles, n_rest_tiles = NCONV // BW, NREST // BW

    xb = x_ref[...].reshape(rows, D_MODEL).astype(BF16)
    rest_order = [R_GATE // BW, R_DU // BW, R_HEADS // BW, R_GATE // BW + 1, R_GATE // BW + 2, R_GATE // BW + 3]
    assert sorted(rest_order) == list(range(n_rest_tiles))

    def ahead(n=1):
        for _ in range(n):
            if rest_order:
                k = rest_order.pop(0)
                tile = _dot(xb, w_rest[:, k * BW:(k + 1) * BW])
                if k == R_DU // BW:
                    to_slabs(dubuf, 0, tile)
                else:
                    zr[:, k * BW:(k + 1) * BW] = tile

    def proj_conv_tile(k):
        zext[:, HIST:HIST + lc, k * BW:(k + 1) * BW] = _dot(xb, w_conv[:, k * BW:(k + 1) * BW]).reshape(nseq, lc, BW)

    zext[:, h0:HIST, C_AX:C_AX + BW] = ca_out[...]
    zext[:, h0:HIST, C_BX:C_BX + 2 * BW] = cb_out[...]
    zext[:, h0:HIST, C_Q:C_Q + 3 * BW] = cc_out[...]
    proj_conv_tile(0)
    proj_conv_tile(1)
    xc = conv(C_AX)
    proj_conv_tile(2)
    xs = _silu(conv(C_BX))
    proj_conv_tile(3)
    bc = _silu(conv(C_BBC))
    bm = head_rep(bc[:, 0:LANES])
    cm = head_rep(bc[:, LANES:2 * LANES])
    proj_conv_tile(4)
    qc = _silu(conv(C_Q))
    proj_conv_tile(5)
    kc = _silu(conv(C_K))
    ahead()
    vc = _silu(conv(C_V))
    ca_out[...] = zext[:, lc + h0:lc + HIST, C_AX:C_AX + BW]
    cb_out[...] = zext[:, lc + h0:lc + HIST, C_BX:C_BX + 2 * BW]
    cc_out[...] = zext[:, lc + h0:lc + HIST, C_Q:C_Q + 3 * BW]
    ahead()

    gts = _dot(xc.astype(BF16), wg[...])
    gate_r = _sigmoid(gts[:, 0:BW] + vec('rg_gate_a_b'))
    gate_i = _sigmoid(gts[:, BW:2 * BW] + vec('rg_gate_x_b'))
    log_a = (-RG_C * _softplus(-vec('rg_lambda'))) * gate_r
    a = jnp.exp(log_a)
    to_slabs(rgbuf, 0, a)
    to_slabs(rgbuf, 2, jnp.sqrt(-jnp.tanh(log_a) * (a * a + 1.0)) * (gate_i * xc))
    for g in range(ngroups):
        gs = slice(g * SUBLANES, (g + 1) * SUBLANES)
        h = [rg_out[gs, k * LANES:(k + 1) * LANES] for k in range(2)]
        for t in range(lc):
            idx = seq_rows(g, t)
            for k in range(2):
                h[k] = rgbuf[k, idx, :] * h[k] + rgbuf[2 + k, idx, :]
                rgbuf[2 + k, idx, :] = h[k]
        for k in range(2):
            rg_out[gs, k * LANES:(k + 1) * LANES] = h[k]
    ahead()
    mix[:, 0:BW] = (from_slabs(rgbuf, 2, 2) * _silu(rest(R_GATE))).astype(BF16)

    du = jnp.concatenate([jnp.concatenate([dubuf[k, seq_rows(g, t), :] for k in range(BW // LANES)], axis=1)
                          for g in range(ngroups) for t in range(lc)], axis=0)
    dub = du.astype(BF16)
    qn = qc * lax.rsqrt(head_sum(qc * qc) + 1e-6) * (HD ** -0.5)
    kn = kc * lax.rsqrt(head_sum(kc * kc) + 1e-6)
    for m in range(2 * S5N // BW):
        bu = _dot(dub, wb_s[:, m * BW:(m + 1) * BW])
        s5buf[2 * m] = bu[:, 0:LANES]
        s5buf[2 * m + 1] = bu[:, LANES:2 * LANES]
    ahead()
    hl = _iota((1, LANES), 1)

    def head_lanes(name_at):
        out = jnp.zeros((1, LANES), F32)
        for name, off in name_at:
            r = vec(name)
            for h in range(NH):
                out = jnp.where(hl == off + h, r[:, h:h + 1], out)
        return out

    def spread(tile, off):
        return jnp.concatenate([jnp.broadcast_to(tile[:, off + h:off + h + 1], (tile.shape[0], HD))
                                for h in range(NH)], axis=1)

    narrow = zr[:, R_HEADS:R_HEADS + LANES]
    sp = _softplus(narrow + head_lanes((('ssd_dt_bias', H_DT), ('gdn_dt_bias', H_DECAY))))
    rate = sp * (-jnp.exp(head_lanes((('ssd_a_log', H_DT), ('gdn_a_log', H_DECAY)))))
    dt = spread(sp, H_DT)
    da = spread(rate, H_DT)
    gdec = spread(rate, H_DECAY)
    beta = spread(_sigmoid(narrow), H_BETA)
    xdt = xs * dt
    vb = vc * beta

    nsl = S5N // LANES
    a_re = [jnp.broadcast_to(tab[0:1, k * LANES:(k + 1) * LANES], (SUBLANES, LANES)) for k in range(nsl)]
    a_im = [jnp.broadcast_to(tab[1:2, k * LANES:(k + 1) * LANES], (SUBLANES, LANES)) for k in range(nsl)]
    s5_state = {}

    def s5_step(g, t):
        gs = slice(g * SUBLANES, (g + 1) * SUBLANES)
        if t == 0:
            s5_state['r'] = [s5r_out[gs, k * LANES:(k + 1) * LANES] for k in range(nsl)]
            s5_state['i'] = [s5i_out[gs, k * LANES:(k + 1) * LANES] for k in range(nsl)]
        hr, hi = s5_state['r'], s5_state['i']
        r0 = (g * lc + t) * SUBLANES
        idx = slice(r0, r0 + SUBLANES)
        for k in range(nsl):
            nr = a_re[k] * hr[k] - a_im[k] * hi[k] + s5buf[k, idx, :]
            ni = a_re[k] * hi[k] + a_im[k] * hr[k] + s5buf[nsl + k, idx, :]
            hr[k], hi[k] = nr, ni
            s5buf[k, idx, :] = nr
            s5buf[nsl + k, idx, :] = ni
        if t == lc - 1:
            for k in range(nsl):
                s5r_out[gs, k * LANES:(k + 1) * LANES] = hr[k]
                s5i_out[gs, k * LANES:(k + 1) * LANES] = hi[k]

    s5_steps = [(g, t) for g in range(ngroups) for t in range(lc)]
    n_parts = 16
    s5_done = [0]

    def s5_part():
        per = -(-len(s5_steps) // n_parts)
        for g, t in s5_steps[s5_done[0]:s5_done[0] + per]:
            s5_step(g, t)
        s5_done[0] += per

    t_c = _iota((CHUNK, BW), 0)
    s_c = _iota((CHUNK, BW), 1) & (HD - 1)
    same_unit = (t_c >> _log2(unit)) == (s_c >> _log2(unit))
    valid_incl = same_unit & (s_c <= t_c)
    valid_strict = same_unit & (s_c < t_c)
    eye_cat = jnp.where(s_c == t_c, 1.0, 0.0)
    lm = lm_ref[...]
    if units == 1:
        lm = lm[0:CHUNK]
    chunks = range(nchunks)
    sls = [slice(c * CHUNK, (c + 1) * CHUNK) for c in chunks]

    strict_f = jnp.where(s_c < t_c, 1.0, 0.0)
    acum, atot, decay, dtot, lmat, eg = [], [], [], [], [], []
    for c in chunks:
        da_c, gd_c = da[sls[c]], gdec[sls[c]]
        cs = _exact_left(lm, jnp.concatenate([rate[sls[c]], da_c * strict_f, gd_c * strict_f], axis=1))
        acum.append(spread(cs[0:CHUNK, 0:LANES], H_DT))
        decay.append(spread(cs[0:CHUNK, 0:LANES], H_DECAY))
        lmat.append(jnp.where(valid_incl, jnp.exp(jnp.minimum(cs[0:CHUNK, LANES:LANES + BW], 0.0)), 0.0))
        eg.append(jnp.where(valid_incl, jnp.exp(jnp.minimum(cs[0:CHUNK, LANES + BW:LANES + 2 * BW], 0.0)), 0.0))
        tot = cs[CHUNK - 1:CHUNK, 0:LANES] if units == 1 else cs[CHUNK:2 * CHUNK, 0:LANES]
        atot.append(spread(tot, H_DT))
        dtot.append(spread(tot, H_DECAY))
    s5_part()
    ahead()

    def seq_of(c, u):
        return (c * CHUNK + u * unit) // lc

    def unit_rows(u):
        return slice(u * unit, (u + 1) * unit)

    def unit_total(tot_c, u):
        r0 = u * unit if units > 1 else 0
        return tot_c[r0:r0 + 1, :]

    def cat_rows(parts):
        return parts[0] if len(parts) == 1 else jnp.concatenate(parts, axis=0)

    unit_of_lane = (_iota((HD, BW), 1) & (HD - 1)) >> _log2(unit)

    def state_updates(w_t, x):
        if units == 1:
            return [_diag_blocks(_dot(w_t.astype(BF16), x.astype(BF16)), hd_mask)]
        w_l = jnp.concatenate([w_t[h * HD:(h + 1) * HD] for h in range(NH)], axis=1)
        lhs = jnp.concatenate([jnp.where(unit_of_lane == u, w_l, 0.0) for u in range(units)], axis=0)
        full = _dot(lhs.astype(BF16), bd(x))
        return [full[u * HD:(u + 1) * HD] for u in range(units)]

    cbm = [_dot_nt(cm[sls[c]].astype(BF16), bd(bm[sls[c]])) for c in chunks]
    s5_part()
    y_ssd = [_dot((cbm[c] * lmat[c]).astype(BF16), bd(xdt[sls[c]])) for c in chunks]
    s5_part()
    ahead()
    ht = [[ssd_out[seq_of(c, u)] for u in range(units)] for c in chunks]
    cme = [(cm[sls[c]] * jnp.exp(acum[c])).astype(BF16) for c in chunks]
    y_int = [cat_rows([_dot(cme[c][unit_rows(u)], bd(ht[c][u])) for u in range(units)]) for c in chunks]
    s5_part()
    wbt = [(bm[sls[c]] * jnp.exp(atot[c] - acum[c])).T for c in chunks]
    upd = [state_updates(wbt[c], xdt[sls[c]]) for c in chunks]
    for c in chunks:
        for u in range(units):
            ssd_out[seq_of(c, u)] = jnp.exp(unit_total(atot[c], u)) * ht[c][u] + upd[c][u]
    s5_part()
    ahead()

    qk_kk = [_dot_nt(jnp.concatenate([qn[sls[c]], kn[sls[c]]], axis=0).astype(BF16), bd(kn[sls[c]])) for c in chunks]
    s5_part()
    qkg = [qk_kk[c][0:CHUNK] * eg[c] for c in chunks]
    mm = [jnp.where(valid_strict, beta[sls[c]] * qk_kk[c][CHUNK:2 * CHUNK] * eg[c], 0.0) for c in chunks]
    rm = [eye_cat - mm[c] for c in chunks]
    pw = [_dot(mm[c].astype(BF16), bd(mm[c])) for c in chunks]
    s5_part()
    ahead()
    for _step in range(_log2(unit) - 2):
        pr2 = [_dot(jnp.concatenate([pw[c], rm[c]], axis=0).astype(BF16), bd(pw[c])) for c in chunks]
        pw = [pr2[c][0:CHUNK] for c in chunks]
        rm = [rm[c] + pr2[c][CHUNK:2 * CHUNK] for c in chunks]
        s5_part()
    rm = [(rm[c] + _dot(rm[c].astype(BF16), bd(pw[c]))).astype(BF16) for c in chunks]
    s5_part()
    ahead()
    edec = [jnp.exp(decay[c]) for c in chunks]
    value = [_dot(rm[c], bd(vb[sls[c]])) for c in chunks]
    kcum = [_dot(rm[c], bd(kn[sls[c]] * beta[sls[c]] * edec[c])) for c in chunks]
    s5_part()
    sq = [[gdn_out[seq_of(c, u)] for u in range(units)] for c in chunks]
    qdec = [qn[sls[c]] * edec[c] for c in chunks]
    kq = [[_dot(jnp.concatenate([kcum[c][unit_rows(u)], qdec[c][unit_rows(u)]], axis=0).astype(BF16), bd(sq[c][u]))
           for u in range(units)] for c in chunks]
    s5_part()
    ahead()
    wv = [cat_rows([value[c][unit_rows(u)] - kq[c][u][0:unit] for u in range(units)]) for c in chunks]
    o_chunks = [cat_rows([kq[c][u][unit:2 * unit] for u in range(units)]) + _dot(qkg[c].astype(BF16), bd(wv[c]))
                for c in chunks]
    kdec_t = [(kn[sls[c]] * jnp.exp(dtot[c] - decay[c])).T for c in chunks]
    upd = [state_updates(kdec_t[c], wv[c]) for c in chunks]
    for c in chunks:
        for u in range(units):
            gdn_out[seq_of(c, u)] = jnp.exp(unit_total(dtot[c], u)) * sq[c][u] + upd[c][u]
    while s5_done[0] < len(s5_steps):
        s5_part()
    ahead(n_conv_tiles + n_rest_tiles)

    def s5_states(m):
        return jnp.concatenate([s5buf[2 * m], s5buf[2 * m + 1]], axis=1).astype(BF16)

    y5 = vec('s5_d') * du
    for m in range(S5N // BW):
        y5 = y5 + _dot(s5_states(m), wc_s[m * BW:(m + 1) * BW, :])
    yb = jnp.concatenate([y_ssd[c] + y_int[c] for c in chunks], axis=0) + head_vec('ssd_d') * xs
    yb = yb * _silu(rest(R_GATE + BW))
    yb = yb * lax.rsqrt(jnp.mean(yb * yb, axis=-1, keepdims=True) + 1e-6) * vec('ssd_norm_w')
    mix[:, BW:2 * BW] = yb.astype(BF16)
    for m in range(S5N // BW, 2 * S5N // BW):
        y5 = y5 + _dot(s5_states(m), wc_s[m * BW:(m + 1) * BW, :])
    o = jnp.concatenate(o_chunks, axis=0)
    gdn_norm = jnp.concatenate([vec('gdn_norm_w')] * NH, axis=1)
    o = o * lax.rsqrt(head_sum(o * o) * (1.0 / HD) + 1e-6) * gdn_norm
    mix[:, 2 * BW:3 * BW] = (o * _silu(rest(R_GATE + 2 * BW))).astype(BF16)
    y5 = 0.5 * y5 * (1.0 + lax.erf(y5 * math.sqrt(0.5)))
    y5 = y5 * _sigmoid(_dot(y5.astype(BF16), glu_w[...]) + vec('s5_glu_b'))
    for k in range(BW // LANES):
        ybuf[k] = y5[:, k * LANES:(k + 1) * LANES]
    y5 = jnp.concatenate(
        [jnp.concatenate([ybuf[k, pl.ds(g * lc * SUBLANES + s, lc, stride=SUBLANES), :] for k in range(BW // LANES)],
                         axis=1) for g in range(ngroups) for s in range(SUBLANES)], axis=0)
    mix[:, 3 * BW:4 * BW] = (y5 * _silu(rest(R_GATE + 3 * BW))).astype(BF16)

    half = rows // 2
    seq_half = nseq // 2

    def out_proj(hh):
        xh = x_ref[hh * seq_half:(hh + 1) * seq_half].reshape(half, D_MODEL)
        return ALPHA * xh + _dot(mix[hh * half:(hh + 1) * half, :], w_out[...])

    def layer_norm(hh, res):
        mu = jnp.mean(res, axis=-1, keepdims=True)
        rc = res - mu
        var = jnp.mean(rc * rc, axis=-1, keepdims=True)
        y = rc * lax.rsqrt(var + 1e-5) * vec('ln_g') + vec('ln_b')
        y_ref[hh * seq_half:(hh + 1) * seq_half] = y.reshape(seq_half, lc, D_MODEL)

    res0 = out_proj(0)
    res1 = out_proj(1)
    layer_norm(0, res0)
    layer_norm(1, res1)


def _chunk_masks(unit):
    t = np.arange(CHUNK)
    same = (t[:, None] // unit) == (t[None, :] // unit)
    incl = same & (t[None, :] <= t[:, None])
    return jnp.asarray(np.tile(np.concatenate([incl, same], axis=0), (1, EXACT_PIECES)), BF16)


def _block_diag(blocks):
    *lead, n, r, c = blocks.shape
    eye = jnp.eye(n, dtype=blocks.dtype)
    out = eye[:, None, :, None] * blocks[..., :, :, None, :]
    return out.reshape(*lead, n * r, n * c)


def _prep_params(w_in, w_out, rg_gate_a_w, rg_gate_x_w, s5_lambda_re, s5_lambda_im, s5_log_dt, s5_b_re, s5_b_im,
                 s5_c_re, s5_c_im, s5_glu_w, **small):
    assert w_in.shape[-1] == IN_COLS
    out = dict(small)
    out.update(
        w_in=w_in.astype(BF16),
        w_out=w_out.astype(BF16),
        wg=jnp.concatenate([_block_diag(rg_gate_a_w), _block_diag(rg_gate_x_w)], axis=-1).astype(BF16),
        s5v=jnp.stack([s5_lambda_re.reshape(DEPTH, S5N), s5_lambda_im.reshape(DEPTH, S5N),
                       jnp.repeat(s5_log_dt, S5_STATE, axis=-1)], axis=1),
        bre=jnp.swapaxes(s5_b_re, -1, -2).reshape(DEPTH, S5_GROUPS * S5_GROUP, S5_STATE),
        bim=jnp.swapaxes(s5_b_im, -1, -2).reshape(DEPTH, S5_GROUPS * S5_GROUP, S5_STATE),
        cre=jnp.swapaxes(s5_c_re, -1, -2).reshape(DEPTH, S5N, S5_GROUP),
        cim=jnp.swapaxes(s5_c_im, -1, -2).reshape(DEPTH, S5N, S5_GROUP),
        glu_w=s5_glu_w.astype(BF16),
    )
    return [out[k] for k in _PARAM_NAMES]


def _layer_call(l, x, states, params, nseq, lc):
    bsz, seqlen, _ = x.shape
    rows = nseq * lc
    assert bsz % nseq == 0 and seqlen % lc == 0 and rows % CHUNK == 0 and lc % SUBLANES == 0 and nseq % SUBLANES == 0
    assert CHUNK % min(lc, CHUNK) == 0 and lc % min(lc, CHUNK) == 0
    pitch = _scan_pitch(lc)
    grid = (bsz // nseq, seqlen // lc)
    state_shapes = [(CONV_W - 1, BW), (CONV_W - 1, 2 * BW), (CONV_W - 1, 3 * BW), (BW,), (HD, BW), (HD, BW), (S5N,), (S5N,)]
    assert len(state_shapes) == _N_STATES

    def param_spec(name, a):
        if name in _LAYER_BLOCKS:
            return pl.BlockSpec((None,) + a.shape[1:], lambda i, j: (l,) + (0,) * (a.ndim - 1), pipeline_mode=pl.Buffered(1))
        return pl.BlockSpec(a.shape, lambda i, j: (0,) * a.ndim, pipeline_mode=pl.Buffered(1))

    lm = _chunk_masks(min(lc, CHUNK))
    x_spec = pl.BlockSpec((nseq, lc, D_MODEL), lambda i, j: (i, j, 0))
    state_args = [] if states is None else list(states)
    state_specs = [pl.BlockSpec((None, nseq) + s, lambda i, j, n=len(s): (l, i) + (0,) * n) for s in state_shapes]
    in_specs = ([x_spec] + (state_specs if state_args else [])
                + [param_spec(name, a) for name, a in zip(_PARAM_NAMES, params)]
                + [pl.BlockSpec(lm.shape, lambda i, j: (0, 0), pipeline_mode=pl.Buffered(1))])
    out_state_specs = [pl.BlockSpec((nseq,) + s, lambda i, j, n=len(s): (i,) + (0,) * n) for s in state_shapes]
    out_shape = [jax.ShapeDtypeStruct(x.shape, F32)] + [jax.ShapeDtypeStruct((bsz,) + s, F32) for s in state_shapes]
    scratch = [
        pltpu.VMEM((nseq, HIST + lc, NCONV), F32),
        pltpu.VMEM((rows, NREST), F32),
        pltpu.VMEM((rows, 4 * BW), BF16),
        pltpu.VMEM((S5_GROUPS * S5_GROUP, 2 * S5N), BF16),
        pltpu.VMEM((2 * S5N, S5_GROUPS * S5_GROUP), BF16),
        pltpu.VMEM((SUBLANES, S5N), F32),
        pltpu.VMEM((4, nseq * pitch, LANES), F32),
        pltpu.VMEM((2 * S5N // LANES, rows, LANES), F32),
        pltpu.VMEM((BW // LANES, nseq * pitch, LANES), F32),
        pltpu.VMEM((BW // LANES, rows, LANES), F32),
        pltpu.VMEM((D_MODEL, NCONV), BF16),
        pltpu.VMEM((D_MODEL, NREST), BF16),
    ]
    outs = pl.pallas_call(
        functools.partial(_layer_kernel, l, nseq, lc, not state_args),
        grid=grid,
        in_specs=in_specs,
        out_specs=[x_spec] + out_state_specs,
        out_shape=out_shape,
        scratch_shapes=scratch,
        compiler_params=pltpu.CompilerParams(dimension_semantics=("arbitrary", "arbitrary"),
                                             vmem_limit_bytes=VMEM_LIMIT_BYTES),
    )(x, *state_args, *params, lm)
    return outs[0], outs[1:]


def _states_to_kernel(conv_a, h_a, conv_b, h_b, conv_c, s_c, s5_re, s5_im):
    d, bsz = h_a.shape[0], h_a.shape[1]
    ssd = jnp.transpose(h_b, (0, 1, 4, 2, 3)).reshape(d, bsz, HD, BW)
    gdn = jnp.transpose(s_c, (0, 1, 3, 2, 4)).reshape(d, bsz, HD, BW)
    return [conv_a, conv_b, conv_c, h_a, ssd, gdn, s5_re.reshape(d, bsz, S5N), s5_im.reshape(d, bsz, S5N)]


def _states_from_kernel(per_layer):
    conv_a, conv_b, conv_c, rg, ssd, gdn, s5r, s5i = (jnp.stack(t) for t in zip(*per_layer))
    d, bsz = rg.shape[0], rg.shape[1]
    return (conv_a, rg, conv_b,
            jnp.transpose(ssd.reshape(d, bsz, HD, NH, HD), (0, 1, 3, 4, 2)),
            conv_c,
            jnp.transpose(gdn.reshape(d, bsz, HD, NH, HD), (0, 1, 3, 2, 4)),
            s5r.reshape(d, bsz, S5_GROUPS, S5_STATE), s5i.reshape(d, bsz, S5_GROUPS, S5_STATE))


def kernel(x_prompt, x_sample, cache_rglru_conv, state_rglru, cache_ssd_conv, state_ssd, cache_gdn_conv, state_gdn,
           state_s5_re, state_s5_im, w_in, w_out, ln_g, ln_b, rg_conv_w, rg_conv_b, rg_gate_a_w, rg_gate_a_b,
           rg_gate_x_w, rg_gate_x_b, rg_lambda, ssd_conv_w, ssd_conv_b, ssd_dt_bias, ssd_a_log, ssd_d, ssd_norm_w,
           gdn_conv_w, gdn_conv_b, gdn_dt_bias, gdn_a_log, gdn_norm_w, s5_lambda_re, s5_lambda_im, s5_log_dt,
           s5_b_re, s5_b_im, s5_c_re, s5_c_im, s5_d, s5_glu_w, s5_glu_b):
    params = _prep_params(w_in, w_out, rg_gate_a_w, rg_gate_x_w, s5_lambda_re, s5_lambda_im, s5_log_dt, s5_b_re,
                          s5_b_im, s5_c_re, s5_c_im, s5_glu_w,
                          ln_g=ln_g, ln_b=ln_b, rg_conv_w=rg_conv_w, ssd_conv_w=ssd_conv_w, gdn_conv_w=gdn_conv_w,
                          rg_conv_b=rg_conv_b, ssd_conv_b=ssd_conv_b, gdn_conv_b=gdn_conv_b, rg_gate_a_b=rg_gate_a_b,
                          rg_gate_x_b=rg_gate_x_b, rg_lambda=rg_lambda, ssd_dt_bias=ssd_dt_bias, ssd_a_log=ssd_a_log,
                          ssd_d=ssd_d, ssd_norm_w=ssd_norm_w, gdn_dt_bias=gdn_dt_bias, gdn_a_log=gdn_a_log,
                          gdn_norm_w=gdn_norm_w, s5_d=s5_d, s5_glu_b=s5_glu_b)
    sample_states = _states_to_kernel(cache_rglru_conv, state_rglru, cache_ssd_conv, state_ssd, cache_gdn_conv,
                                      state_gdn, state_s5_re, state_s5_im)
    pb, pl_len = x_prompt.shape[0], x_prompt.shape[1]
    sl = x_sample.shape[1]
    lc_p = ROWS // pb
    assert pl_len % lc_p == 0
    yp, ys = x_prompt, x_sample
    p_new, s_new = [], []
    for l in range(DEPTH):
        yp, st_p = _layer_call(l, yp, None, params, nseq=pb, lc=lc_p)
        ys, st_s = _layer_call(l, ys, sample_states, params, nseq=SAMPLE_SEQS, lc=sl)
        p_new.append(st_p)
        s_new.append(st_s)
    return (yp, ys, *_states_from_kernel(p_new), *_states_from_kernel(s_new))
```

```python
import functools
import math

import numpy as np
import jax
import jax.numpy as jnp
from jax import lax
from jax.experimental import pallas as pl
from jax.experimental.pallas import tpu as pltpu

F32 = jnp.float32
BF16 = jnp.bfloat16

D_MODEL = 1024
DEPTH = 2
BW = 256
HD = 64
NH = BW // HD
SSD_GROUPS = 2
CONV_W = 4
HIST = 8
S5_GROUPS = 16
S5_GROUP = 16
S5_STATE = 64
S5N = S5_GROUPS * S5_STATE
LANES = 128
SUBLANES = 8
RG_C = 8.0
ALPHA = (2.0 * DEPTH) ** 0.25
CHUNK = 64
ROWS = 512
SAMPLE_SEQS = 32
VMEM_LIMIT_BYTES = 60 * 1024 * 1024

NCONV = 6 * BW
C_AX, C_BX, C_BBC, C_Q, C_K, C_V = (k * BW for k in range(6))
NREST = 6 * BW
R_GATE, R_DU, R_HEADS = 0, 4 * BW, 5 * BW
H_DT, H_BETA, H_DECAY = 0, NH, 2 * NH


def _dot(a, b):
    return jnp.dot(a, b, preferred_element_type=F32)


def _dot_nt(a, b):
    return lax.dot_general(a, b, (((1,), (1,)), ((), ())), preferred_element_type=F32)


def _sigmoid(x):
    return 0.5 * jnp.tanh(0.5 * x) + 0.5


def _silu(x):
    hx = 0.5 * x
    return hx * jnp.tanh(hx) + hx


def _softplus(x):
    return jnp.maximum(x, 0.0) + jnp.log1p(jnp.exp(-jnp.abs(x)))


def _split_bf16(x, pieces):
    out = []
    r = x
    for k in range(pieces):
        p = r.astype(BF16)
        out.append(p)
        if k + 1 < pieces:
            r = r - p.astype(F32)
    return out


EXACT_PIECES = 3


def _exact_left(mask_wide, x):
    return _dot(mask_wide, jnp.concatenate(_split_bf16(x, EXACT_PIECES), axis=0))


def _iota(shape, dim):
    return lax.broadcasted_iota(jnp.int32, shape, dim)


def _log2(n):
    k = int(round(math.log2(n)))
    assert 1 << k == n
    return k


def _bd(x, mask):
    return jnp.where(mask, jnp.tile(x, (NH, 1)), 0.0).astype(BF16)


def _diag_blocks(full, mask):
    fm = jnp.where(mask, full, 0.0)
    return fm[0:HD] + fm[HD:2 * HD] + fm[2 * HD:3 * HD] + fm[3 * HD:4 * HD]


def _scan_pitch(lc):
    return lc if (lc // SUBLANES) % 2 == 1 else lc + SUBLANES


_PARAM_NAMES = ('w_in', 'w_out', 'wg', 's5v', 'bre', 'bim', 'cre', 'cim', 'glu_w',
                'ln_g', 'ln_b', 'rg_conv_w', 'ssd_conv_w', 'gdn_conv_w', 'rg_conv_b', 'ssd_conv_b', 'gdn_conv_b',
                'rg_gate_a_b', 'rg_gate_x_b', 'rg_lambda', 'ssd_dt_bias', 'ssd_a_log', 'ssd_d', 'ssd_norm_w',
                'gdn_dt_bias', 'gdn_a_log', 'gdn_norm_w', 's5_d', 's5_glu_b')
_LAYER_BLOCKS = frozenset(_PARAM_NAMES[:9])
_N_STATES = 8
S5V_LRE, S5V_LIM, S5V_LOGDT = range(3)

_IN_COLS = {}
_c0 = 0
for _name, _w in (('a_x', BW), ('a_gate', BW), ('b_xbc', BW + 2 * SSD_GROUPS * HD), ('b_dt', NH), ('b_gate', BW),
                  ('c_qkv', 3 * BW), ('c_beta', NH), ('c_decay', NH), ('c_gate', BW), ('d_u', BW), ('d_gate', BW)):
    _IN_COLS[_name] = (_c0, _w)
    _c0 += _w
IN_COLS = _c0
_CONV_FROM = (('a_x', C_AX), ('b_xbc', C_BX), ('c_qkv', C_Q))
_REST_FROM = (('a_gate', R_GATE), ('b_gate', R_GATE + BW), ('c_gate', R_GATE + 2 * BW), ('d_gate', R_GATE + 3 * BW),
              ('d_u', R_DU), ('b_dt', R_HEADS + H_DT), ('c_beta', R_HEADS + H_BETA), ('c_decay', R_HEADS + H_DECAY))


def _layer_kernel(l, nseq, lc, zero_init, *refs):
    n_in = 0 if zero_init else _N_STATES
    x_ref = refs[0]
    state_in = refs[1:1 + n_in]
    prm = dict(zip(_PARAM_NAMES, refs[1 + n_in:1 + n_in + len(_PARAM_NAMES)]))
    rest_refs = refs[1 + n_in + len(_PARAM_NAMES):]
    lm_ref, y_ref = rest_refs[0], rest_refs[1]
    state_out = rest_refs[2:2 + _N_STATES]
    ca_out, cb_out, cc_out, rg_out, ssd_out, gdn_out, s5r_out, s5i_out = state_out
    (zext, zr, mix, wb_s, wc_s, tab, rgbuf, s5buf, dubuf, ybuf, w_conv, w_rest,
     w_in_buf, w_in_sem) = rest_refs[2 + _N_STATES:]
    w_in, w_out, wg, glu_w = (prm[k] for k in ('w_in', 'w_out', 'wg', 'glu_w'))
    s5v, bre, bim, cre, cim = (prm[k] for k in ('s5v', 'bre', 'bim', 'cre', 'cim'))

    rows = nseq * lc
    pitch = _scan_pitch(lc)
    ngroups = nseq // SUBLANES
    unit = min(lc, CHUNK)
    units = CHUNK // unit
    nchunks = rows // CHUNK
    first_call_step = (pl.program_id(0) == 0) & (pl.program_id(1) == 0)

    def vec(name):
        return prm[name][l:l + 1, :]

    def head_vec(name):
        r = vec(name)
        return jnp.concatenate([jnp.broadcast_to(r[:, h:h + 1], (1, HD)) for h in range(NH)], axis=1)

    @pl.when(first_call_step)
    def _():
        w_in_copy = pltpu.make_async_copy(w_in.at[l], w_in_buf, w_in_sem)
        w_in_copy.start()
        lr = s5v[S5V_LRE:S5V_LRE + 1, :]
        li = s5v[S5V_LIM:S5V_LIM + 1, :]
        dt = jnp.exp(s5v[S5V_LOGDT:S5V_LOGDT + 1, :])
        mag = jnp.exp(lr * dt)
        ang = li * dt
        ar = mag * jnp.cos(ang)
        ai = mag * jnp.sin(ang)
        den = lr * lr + li * li
        fr = ((ar - 1.0) * lr + ai * li) / den
        fi = (ai * lr - (ar - 1.0) * li) / den
        def group_diag(blocks, rows_per_group, lanes_per_group):
            t = jnp.concatenate([blocks] * S5_GROUPS, axis=1)
            same = ((_iota(t.shape, 0) >> _log2(rows_per_group)) == (_iota(t.shape, 1) >> _log2(lanes_per_group)))
            return jnp.where(same, t, 0.0)

        b_re, b_im = (group_diag(r[...], S5_GROUP, S5_STATE) for r in (bre, bim))
        wb_s[:, 0:S5N] = (fr * b_re - fi * b_im).astype(BF16)
        wb_s[:, S5N:2 * S5N] = (fr * b_im + fi * b_re).astype(BF16)
        wc_s[0:S5N, :] = group_diag(cre[...], S5_STATE, S5_GROUP).astype(BF16)
        wc_s[S5N:2 * S5N, :] = (-group_diag(cim[...], S5_STATE, S5_GROUP)).astype(BF16)
        tab[0:1, :] = ar
        tab[1:2, :] = ai
        w_in_copy.wait()
        for name, dst in _CONV_FROM:
            c0, width = _IN_COLS[name]
            w_conv[:, dst:dst + width] = w_in_buf[:, c0:c0 + width]
        used = R_HEADS + 3 * NH
        w_rest[:, used:NREST] = jnp.zeros((D_MODEL, NREST - used), BF16)
        for name, dst in _REST_FROM:
            c0, width = _IN_COLS[name]
            w_rest[:, dst:dst + width] = w_in_buf[:, c0:c0 + width]

    @pl.when(pl.program_id(1) == 0)
    def _():
        for k, out in enumerate(state_out):
            out[...] = jnp.zeros(out.shape, F32) if zero_init else state_in[k][...]

    hd_mask = (_iota((BW, BW), 0) >> _log2(HD)) == (_iota((BW, BW), 1) >> _log2(HD))
    bd = functools.partial(_bd, mask=hd_mask)

    h0 = HIST - (CONV_W - 1)

    conv_src = {C_AX: ('rg', 0), C_BX: ('ssd', 0), C_BBC: ('ssd', BW), C_Q: ('gdn', 0), C_K: ('gdn', BW),
                C_V: ('gdn', 2 * BW)}

    def conv(c0):
        branch, p0 = conv_src[c0]
        wts, bias = prm[branch + '_conv_w'], prm[branch + '_conv_b']
        ext = zext[:, :, c0:c0 + BW]
        def tap(k):
            return wts[l, k:k + 1, p0:p0 + BW]

        assert CONV_W == 4
        prev = pltpu.roll(ext, 1, 1)
        newer = tap(3) * ext[:, HIST:, :] + tap(2) * prev[:, HIST:, :]
        older = tap(1) * ext + tap(0) * prev
        acc = bias[l:l + 1, p0:p0 + BW] + newer + pltpu.roll(older, 2, 1)[:, HIST:, :]
        return acc.reshape(rows, BW)

    def rest(c0):
        return zr[:, c0:c0 + BW]

    def to_slabs(buf, first, val):
        for k in range(val.shape[1] // LANES):
            piece = val[:, k * LANES:(k + 1) * LANES]
            if pitch == lc:
                buf[first + k] = piece
            else:
                for q in range(nseq):
                    buf[first + k, q * pitch:q * pitch + lc, :] = piece[q * lc:(q + 1) * lc]

    def from_slabs(buf, first, n):
        cols = []
        for k in range(n):
            if pitch == lc:
                cols.append(buf[first + k])
            else:
                cols.append(jnp.concatenate([buf[first + k, q * pitch:q * pitch + lc, :] for q in range(nseq)], axis=0))
        return cols[0] if n == 1 else jnp.concatenate(cols, axis=1)

    def seq_rows(g, t):
        return pl.ds(g * SUBLANES * pitch + t, SUBLANES, stride=pitch)

    ones_bd = hd_mask.astype(BF16)

    def head_sum(v):
        return _dot(v.astype(BF16), ones_bd)

    lane = _iota((rows, LANES), 1)

    def head_rep(pair):
        rolled = pltpu.roll(pair, HD, 1)
        return jnp.concatenate([jnp.where(lane < HD, pair, rolled), jnp.where(lane < HD, rolled, pair)], axis=1)


    n_conv_tiles, n_rest_tiles = NCONV // BW, NREST // BW

    xb = x_ref[...].reshape(rows, D_MODEL).astype(BF16)
    rest_order = [R_GATE // BW, R_DU // BW, R_HEADS // BW, R_GATE // BW + 1, R_GATE // BW + 2, R_GATE // BW + 3]
    assert sorted(rest_order) == list(range(n_rest_tiles))

    def ahead(n=1):
        for _ in range(n):
            if rest_order:
                k = rest_order.pop(0)
                tile = _dot(xb, w_rest[:, k * BW:(k + 1) * BW])
                if k == R_DU // BW:
                    to_slabs(dubuf, 0, tile)
                else:
                    zr[:, k * BW:(k + 1) * BW] = tile

    def proj_conv_tile(k):
        zext[:, HIST:HIST + lc, k * BW:(k + 1) * BW] = _dot(xb, w_conv[:, k * BW:(k + 1) * BW]).reshape(nseq, lc, BW)

    zext[:, h0:HIST, C_AX:C_AX + BW] = ca_out[...]
    zext[:, h0:HIST, C_BX:C_BX + 2 * BW] = cb_out[...]
    zext[:, h0:HIST, C_Q:C_Q + 3 * BW] = cc_out[...]
    proj_conv_tile(0)
    proj_conv_tile(1)
    xc = conv(C_AX)
    proj_conv_tile(2)
    xs = _silu(conv(C_BX))
    proj_conv_tile(3)
    bc = _silu(conv(C_BBC))
    bm = head_rep(bc[:, 0:LANES])
    cm = head_rep(bc[:, LANES:2 * LANES])
    proj_conv_tile(4)
    qc = _silu(conv(C_Q))
    proj_conv_tile(5)
    kc = _silu(conv(C_K))
    ahead()
    vc = _silu(conv(C_V))
    ca_out[...] = zext[:, lc + h0:lc + HIST, C_AX:C_AX + BW]
    cb_out[...] = zext[:, lc + h0:lc + HIST, C_BX:C_BX + 2 * BW]
    cc_out[...] = zext[:, lc + h0:lc + HIST, C_Q:C_Q + 3 * BW]
    ahead()

    gts = _dot(xc.astype(BF16), wg[...])
    gate_r = _sigmoid(gts[:, 0:BW] + vec('rg_gate_a_b'))
    gate_i = _sigmoid(gts[:, BW:2 * BW] + vec('rg_gate_x_b'))
    log_a = (-RG_C * _softplus(-vec('rg_lambda'))) * gate_r
    a = jnp.exp(log_a)
    to_slabs(rgbuf, 0, a)
    to_slabs(rgbuf, 2, jnp.sqrt(-jnp.tanh(log_a) * (a * a + 1.0)) * (gate_i * xc))
    for g in range(ngroups):
        gs = slice(g * SUBLANES, (g + 1) * SUBLANES)
        h = [rg_out[gs, k * LANES:(k + 1) * LANES] for k in range(2)]
        for t in range(lc):
            idx = seq_rows(g, t)
            for k in range(2):
                h[k] = rgbuf[k, idx, :] * h[k] + rgbuf[2 + k, idx, :]
                rgbuf[2 + k, idx, :] = h[k]
        for k in range(2):
            rg_out[gs, k * LANES:(k + 1) * LANES] = h[k]
    ahead()
    mix[:, 0:BW] = (from_slabs(rgbuf, 2, 2) * _silu(rest(R_GATE))).astype(BF16)

    du = jnp.concatenate([jnp.concatenate([dubuf[k, seq_rows(g, t), :] for k in range(BW // LANES)], axis=1)
                          for g in range(ngroups) for t in range(lc)], axis=0)
    dub = du.astype(BF16)
    qn = qc * lax.rsqrt(head_sum(qc * qc) + 1e-6) * (HD ** -0.5)
    kn = kc * lax.rsqrt(head_sum(kc * kc) + 1e-6)
    for m in range(2 * S5N // BW):
        bu = _dot(dub, wb_s[:, m * BW:(m + 1) * BW])
        s5buf[2 * m] = bu[:, 0:LANES]
        s5buf[2 * m + 1] = bu[:, LANES:2 * LANES]
    ahead()
    hl = _iota((1, LANES), 1)

    def head_lanes(name_at):
        out = jnp.zeros((1, LANES), F32)
        for name, off in name_at:
            r = vec(name)
            for h in range(NH):
                out = jnp.where(hl == off + h, r[:, h:h + 1], out)
        return out

    def spread(tile, off):
        return jnp.concatenate([jnp.broadcast_to(tile[:, off + h:off + h + 1], (tile.shape[0], HD))
                                for h in range(NH)], axis=1)

    narrow = zr[:, R_HEADS:R_HEADS + LANES]
    sp = _softplus(narrow + head_lanes((('ssd_dt_bias', H_DT), ('gdn_dt_bias', H_DECAY))))
    rate = sp * (-jnp.exp(head_lanes((('ssd_a_log', H_DT), ('gdn_a_log', H_DECAY)))))
    dt = spread(sp, H_DT)
    da = spread(rate, H_DT)
    gdec = spread(rate, H_DECAY)
    beta = spread(_sigmoid(narrow), H_BETA)
    xdt = xs * dt
    vb = vc * beta

    nsl = S5N // LANES
    a_re = [jnp.broadcast_to(tab[0:1, k * LANES:(k + 1) * LANES], (SUBLANES, LANES)) for k in range(nsl)]
    a_im = [jnp.broadcast_to(tab[1:2, k * LANES:(k + 1) * LANES], (SUBLANES, LANES)) for k in range(nsl)]
    s5_state = {}

    def s5_step(g, t):
        gs = slice(g * SUBLANES, (g + 1) * SUBLANES)
        if t == 0:
            s5_state['r'] = [s5r_out[gs, k * LANES:(k + 1) * LANES] for k in range(nsl)]
            s5_state['i'] = [s5i_out[gs, k * LANES:(k + 1) * LANES] for k in range(nsl)]
        hr, hi = s5_state['r'], s5_state['i']
        r0 = (g * lc + t) * SUBLANES
        idx = slice(r0, r0 + SUBLANES)
        for k in range(nsl):
            nr = a_re[k] * hr[k] - a_im[k] * hi[k] + s5buf[k, idx, :]
            ni = a_re[k] * hi[k] + a_im[k] * hr[k] + s5buf[nsl + k, idx, :]
            hr[k], hi[k] = nr, ni
            s5buf[k, idx, :] = nr
            s5buf[nsl + k, idx, :] = ni
        if t == lc - 1:
            for k in range(nsl):
                s5r_out[gs, k * LANES:(k + 1) * LANES] = hr[k]
                s5i_out[gs, k * LANES:(k + 1) * LANES] = hi[k]

    s5_steps = [(g, t) for g in range(ngroups) for t in range(lc)]
    n_parts = 16
    s5_done = [0]

    def s5_part():
        per = -(-len(s5_steps) // n_parts)
        for g, t in s5_steps[s5_done[0]:s5_done[0] + per]:
            s5_step(g, t)
        s5_done[0] += per

    t_c = _iota((CHUNK, BW), 0)
    s_c = _iota((CHUNK, BW), 1) & (HD - 1)
    same_unit = (t_c >> _log2(unit)) == (s_c >> _log2(unit))
    valid_incl = same_unit & (s_c <= t_c)
    valid_strict = same_unit & (s_c < t_c)
    eye_cat = jnp.where(s_c == t_c, 1.0, 0.0)
    lm = lm_ref[...]
    if units == 1:
        lm = lm[0:CHUNK]
    chunks = range(nchunks)
    sls = [slice(c * CHUNK, (c + 1) * CHUNK) for c in chunks]

    strict_f = jnp.where(s_c < t_c, 1.0, 0.0)
    acum, atot, decay, dtot, lmat, eg = [], [], [], [], [], []
    for c in chunks:
        da_c, gd_c = da[sls[c]], gdec[sls[c]]
        cs = _exact_left(lm, jnp.concatenate([rate[sls[c]], da_c * strict_f, gd_c * strict_f], axis=1))
        acum.append(spread(cs[0:CHUNK, 0:LANES], H_DT))
        decay.append(spread(cs[0:CHUNK, 0:LANES], H_DECAY))
        lmat.append(jnp.where(valid_incl, jnp.exp(jnp.minimum(cs[0:CHUNK, LANES:LANES + BW], 0.0)), 0.0))
        eg.append(jnp.where(valid_incl, jnp.exp(jnp.minimum(cs[0:CHUNK, LANES + BW:LANES + 2 * BW], 0.0)), 0.0))
        tot = cs[CHUNK - 1:CHUNK, 0:LANES] if units == 1 else cs[CHUNK:2 * CHUNK, 0:LANES]
        atot.append(spread(tot, H_DT))
        dtot.append(spread(tot, H_DECAY))
    s5_part()
    ahead()

    def seq_of(c, u):
        return (c * CHUNK + u * unit) // lc

    def unit_rows(u):
        return slice(u * unit, (u + 1) * unit)

    def unit_total(tot_c, u):
        r0 = u * unit if units > 1 else 0
        return tot_c[r0:r0 + 1, :]

    def cat_rows(parts):
        return parts[0] if len(parts) == 1 else jnp.concatenate(parts, axis=0)

    unit_of_lane = (_iota((HD, BW), 1) & (HD - 1)) >> _log2(unit)

    def state_updates(w_t, x):
        if units == 1:
            return [_diag_blocks(_dot(w_t.astype(BF16), x.astype(BF16)), hd_mask)]
        w_l = jnp.concatenate([w_t[h * HD:(h + 1) * HD] for h in range(NH)], axis=1)
        lhs = jnp.concatenate([jnp.where(unit_of_lane == u, w_l, 0.0) for u in range(units)], axis=0)
        full = _dot(lhs.astype(BF16), bd(x))
        return [full[u * HD:(u + 1) * HD] for u in range(units)]

    cbm = [_dot_nt(cm[sls[c]].astype(BF16), bd(bm[sls[c]])) for c in chunks]
    s5_part()
    y_ssd = [_dot((cbm[c] * lmat[c]).astype(BF16), bd(xdt[sls[c]])) for c in chunks]
    s5_part()
    ahead()
    ht = [[ssd_out[seq_of(c, u)] for u in range(units)] for c in chunks]
    cme = [(cm[sls[c]] * jnp.exp(acum[c])).astype(BF16) for c in chunks]
    y_int = [cat_rows([_dot(cme[c][unit_rows(u)], bd(ht[c][u])) for u in range(units)]) for c in chunks]
    s5_part()
    wbt = [(bm[sls[c]] * jnp.exp(atot[c] - acum[c])).T for c in chunks]
    upd = [state_updates(wbt[c], xdt[sls[c]]) for c in chunks]
    for c in chunks:
        for u in range(units):
            ssd_out[seq_of(c, u)] = jnp.exp(unit_total(atot[c], u)) * ht[c][u] + upd[c][u]
    s5_part()
    ahead()

    qk_kk = [_dot_nt(jnp.concatenate([qn[sls[c]], kn[sls[c]]], axis=0).astype(BF16), bd(kn[sls[c]])) for c in chunks]
    s5_part()
    qkg = [qk_kk[c][0:CHUNK] * eg[c] for c in chunks]
    mm = [jnp.where(valid_strict, beta[sls[c]] * qk_kk[c][CHUNK:2 * CHUNK] * eg[c], 0.0) for c in chunks]
    rm = [eye_cat - mm[c] for c in chunks]
    pw = [_dot(mm[c].astype(BF16), bd(mm[c])) for c in chunks]
    s5_part()
    ahead()
    for _step in range(_log2(unit) - 2):
        pr2 = [_dot(jnp.concatenate([pw[c], rm[c]], axis=0).astype(BF16), bd(pw[c])) for c in chunks]
        pw = [pr2[c][0:CHUNK] for c in chunks]
        rm = [rm[c] + pr2[c][CHUNK:2 * CHUNK] for c in chunks]
        s5_part()
    rm = [(rm[c] + _dot(rm[c].astype(BF16), bd(pw[c]))).astype(BF16) for c in chunks]
    s5_part()
    ahead()
    edec = [jnp.exp(decay[c]) for c in chunks]
    value = [_dot(rm[c], bd(vb[sls[c]])) for c in chunks]
    kcum = [_dot(rm[c], bd(kn[sls[c]] * beta[sls[c]] * edec[c])) for c in chunks]
    s5_part()
    sq = [[gdn_out[seq_of(c, u)] for u in range(units)] for c in chunks]
    qdec = [qn[sls[c]] * edec[c] for c in chunks]
    kq = [[_dot(jnp.concatenate([kcum[c][unit_rows(u)], qdec[c][unit_rows(u)]], axis=0).astype(BF16), bd(sq[c][u]))
           for u in range(units)] for c in chunks]
    s5_part()
    ahead()
    wv = [cat_rows([value[c][unit_rows(u)] - kq[c][u][0:unit] for u in range(units)]) for c in chunks]
    o_chunks = [cat_rows([kq[c][u][unit:2 * unit] for u in range(units)]) + _dot(qkg[c].astype(BF16), bd(wv[c]))
                for c in chunks]
    kdec_t = [(kn[sls[c]] * jnp.exp(dtot[c] - decay[c])).T for c in chunks]
    upd = [state_updates(kdec_t[c], wv[c]) for c in chunks]
    for c in chunks:
        for u in range(units):
            gdn_out[seq_of(c, u)] = jnp.exp(unit_total(dtot[c], u)) * sq[c][u] + upd[c][u]
    while s5_done[0] < len(s5_steps):
        s5_part()
    ahead(n_conv_tiles + n_rest_tiles)

    def s5_states(m):
        return jnp.concatenate([s5buf[2 * m], s5buf[2 * m + 1]], axis=1).astype(BF16)

    y5 = vec('s5_d') * du
    for m in range(S5N // BW):
        y5 = y5 + _dot(s5_states(m), wc_s[m * BW:(m + 1) * BW, :])
    yb = jnp.concatenate([y_ssd[c] + y_int[c] for c in chunks], axis=0) + head_vec('ssd_d') * xs
    yb = yb * _silu(rest(R_GATE + BW))
    yb = yb * lax.rsqrt(jnp.mean(yb * yb, axis=-1, keepdims=True) + 1e-6) * vec('ssd_norm_w')
    mix[:, BW:2 * BW] = yb.astype(BF16)
    for m in range(S5N // BW, 2 * S5N // BW):
        y5 = y5 + _dot(s5_states(m), wc_s[m * BW:(m + 1) * BW, :])
    o = jnp.concatenate(o_chunks, axis=0)
    gdn_norm = jnp.concatenate([vec('gdn_norm_w')] * NH, axis=1)
    o = o * lax.rsqrt(head_sum(o * o) * (1.0 / HD) + 1e-6) * gdn_norm
    mix[:, 2 * BW:3 * BW] = (o * _silu(rest(R_GATE + 2 * BW))).astype(BF16)
    y5 = 0.5 * y5 * (1.0 + lax.erf(y5 * math.sqrt(0.5)))
    y5 = y5 * _sigmoid(_dot(y5.astype(BF16), glu_w[...]) + vec('s5_glu_b'))
    for k in range(BW // LANES):
        ybuf[k] = y5[:, k * LANES:(k + 1) * LANES]
    y5 = jnp.concatenate(
        [jnp.concatenate([ybuf[k, pl.ds(g * lc * SUBLANES + s, lc, stride=SUBLANES), :] for k in range(BW // LANES)],
                         axis=1) for g in range(ngroups) for s in range(SUBLANES)], axis=0)
    mix[:, 3 * BW:4 * BW] = (y5 * _silu(rest(R_GATE + 3 * BW))).astype(BF16)

    half = rows // 2
    seq_half = nseq // 2

    def out_proj(hh):
        xh = x_ref[hh * seq_half:(hh + 1) * seq_half].reshape(half, D_MODEL)
        return ALPHA * xh + _dot(mix[hh * half:(hh + 1) * half, :], w_out[...])

    def layer_norm(hh, res):
        mu = jnp.mean(res, axis=-1, keepdims=True)
        rc = res - mu
        var = jnp.mean(rc * rc, axis=-1, keepdims=True)
        y = rc * lax.rsqrt(var + 1e-5) * vec('ln_g') + vec('ln_b')
        y_ref[hh * seq_half:(hh + 1) * seq_half] = y.reshape(seq_half, lc, D_MODEL)

    res0 = out_proj(0)
    res1 = out_proj(1)
    layer_norm(0, res0)
    layer_norm(1, res1)


def _chunk_masks(unit):
    t = np.arange(CHUNK)
    same = (t[:, None] // unit) == (t[None, :] // unit)
    incl = same & (t[None, :] <= t[:, None])
    return jnp.asarray(np.tile(np.concatenate([incl, same], axis=0), (1, EXACT_PIECES)), BF16)


def _block_diag(blocks):
    *lead, n, r, c = blocks.shape
    eye = jnp.eye(n, dtype=blocks.dtype)
    out = eye[:, None, :, None] * blocks[..., :, :, None, :]
    return out.reshape(*lead, n * r, n * c)


def _prep_params(w_in, w_out, rg_gate_a_w, rg_gate_x_w, s5_lambda_re, s5_lambda_im, s5_log_dt, s5_b_re, s5_b_im,
                 s5_c_re, s5_c_im, s5_glu_w, **small):
    assert w_in.shape[-1] == IN_COLS
    out = dict(small)
    out.update(
        w_in=w_in.astype(BF16),
        w_out=w_out.astype(BF16),
        wg=jnp.concatenate([_block_diag(rg_gate_a_w), _block_diag(rg_gate_x_w)], axis=-1).astype(BF16),
        s5v=jnp.stack([s5_lambda_re.reshape(DEPTH, S5N), s5_lambda_im.reshape(DEPTH, S5N),
                       jnp.repeat(s5_log_dt, S5_STATE, axis=-1)], axis=1),
        bre=jnp.swapaxes(s5_b_re, -1, -2).reshape(DEPTH, S5_GROUPS * S5_GROUP, S5_STATE),
        bim=jnp.swapaxes(s5_b_im, -1, -2).reshape(DEPTH, S5_GROUPS * S5_GROUP, S5_STATE),
        cre=jnp.swapaxes(s5_c_re, -1, -2).reshape(DEPTH, S5N, S5_GROUP),
        cim=jnp.swapaxes(s5_c_im, -1, -2).reshape(DEPTH, S5N, S5_GROUP),
        glu_w=s5_glu_w.astype(BF16),
    )
    return [out[k] for k in _PARAM_NAMES]


def _layer_call(l, x, states, params, nseq, lc):
    bsz, seqlen, _ = x.shape
    rows = nseq * lc
    assert bsz % nseq == 0 and seqlen % lc == 0 and rows % CHUNK == 0 and lc % SUBLANES == 0 and nseq % SUBLANES == 0
    assert CHUNK % min(lc, CHUNK) == 0 and lc % min(lc, CHUNK) == 0
    pitch = _scan_pitch(lc)
    grid = (bsz // nseq, seqlen // lc)
    state_shapes = [(CONV_W - 1, BW), (CONV_W - 1, 2 * BW), (CONV_W - 1, 3 * BW), (BW,), (HD, BW), (HD, BW), (S5N,), (S5N,)]
    assert len(state_shapes) == _N_STATES

    def param_spec(name, a):
        if name == 'w_in':
            return pl.BlockSpec(memory_space=pl.ANY)
        if name in _LAYER_BLOCKS:
            return pl.BlockSpec((None,) + a.shape[1:], lambda i, j: (l,) + (0,) * (a.ndim - 1), pipeline_mode=pl.Buffered(1))
        return pl.BlockSpec(a.shape, lambda i, j: (0,) * a.ndim, pipeline_mode=pl.Buffered(1))

    lm = _chunk_masks(min(lc, CHUNK))
    x_spec = pl.BlockSpec((nseq, lc, D_MODEL), lambda i, j: (i, j, 0))
    state_args = [] if states is None else list(states)
    state_specs = [pl.BlockSpec((None, nseq) + s, lambda i, j, n=len(s): (l, i) + (0,) * n) for s in state_shapes]
    in_specs = ([x_spec] + (state_specs if state_args else [])
                + [param_spec(name, a) for name, a in zip(_PARAM_NAMES, params)]
                + [pl.BlockSpec(lm.shape, lambda i, j: (0, 0), pipeline_mode=pl.Buffered(1))])
    out_state_specs = [pl.BlockSpec((nseq,) + s, lambda i, j, n=len(s): (i,) + (0,) * n) for s in state_shapes]
    out_shape = [jax.ShapeDtypeStruct(x.shape, F32)] + [jax.ShapeDtypeStruct((bsz,) + s, F32) for s in state_shapes]
    scratch = [
        pltpu.VMEM((nseq, HIST + lc, NCONV), F32),
        pltpu.VMEM((rows, NREST), F32),
        pltpu.VMEM((rows, 4 * BW), BF16),
        pltpu.VMEM((S5_GROUPS * S5_GROUP, 2 * S5N), BF16),
        pltpu.VMEM((2 * S5N, S5_GROUPS * S5_GROUP), BF16),
        pltpu.VMEM((SUBLANES, S5N), F32),
        pltpu.VMEM((4, nseq * pitch, LANES), F32),
        pltpu.VMEM((2 * S5N // LANES, rows, LANES), F32),
        pltpu.VMEM((BW // LANES, nseq * pitch, LANES), F32),
        pltpu.VMEM((BW // LANES, rows, LANES), F32),
        pltpu.VMEM((D_MODEL, NCONV), BF16),
        pltpu.VMEM((D_MODEL, NREST), BF16),
        pltpu.VMEM((D_MODEL, IN_COLS), BF16),
        pltpu.SemaphoreType.DMA(()),
    ]
    outs = pl.pallas_call(
        functools.partial(_layer_kernel, l, nseq, lc, not state_args),
        grid=grid,
        in_specs=in_specs,
        out_specs=[x_spec] + out_state_specs,
        out_shape=out_shape,
        scratch_shapes=scratch,
        compiler_params=pltpu.CompilerParams(dimension_semantics=("arbitrary", "arbitrary"),
                                             vmem_limit_bytes=VMEM_LIMIT_BYTES),
    )(x, *state_args, *params, lm)
    return outs[0], outs[1:]


def _states_to_kernel(conv_a, h_a, conv_b, h_b, conv_c, s_c, s5_re, s5_im):
    d, bsz = h_a.shape[0], h_a.shape[1]
    ssd = jnp.transpose(h_b, (0, 1, 4, 2, 3)).reshape(d, bsz, HD, BW)
    gdn = jnp.transpose(s_c, (0, 1, 3, 2, 4)).reshape(d, bsz, HD, BW)
    return [conv_a, conv_b, conv_c, h_a, ssd, gdn, s5_re.reshape(d, bsz, S5N), s5_im.reshape(d, bsz, S5N)]


def _states_from_kernel(per_layer):
    conv_a, conv_b, conv_c, rg, ssd, gdn, s5r, s5i = (jnp.stack(t) for t in zip(*per_layer))
    d, bsz = rg.shape[0], rg.shape[1]
    return (conv_a, rg, conv_b,
            jnp.transpose(ssd.reshape(d, bsz, HD, NH, HD), (0, 1, 3, 4, 2)),
            conv_c,
            jnp.transpose(gdn.reshape(d, bsz, HD, NH, HD), (0, 1, 3, 2, 4)),
            s5r.reshape(d, bsz, S5_GROUPS, S5_STATE), s5i.reshape(d, bsz, S5_GROUPS, S5_STATE))


def kernel(x_prompt, x_sample, cache_rglru_conv, state_rglru, cache_ssd_conv, state_ssd, cache_gdn_conv, state_gdn,
           state_s5_re, state_s5_im, w_in, w_out, ln_g, ln_b, rg_conv_w, rg_conv_b, rg_gate_a_w, rg_gate_a_b,
           rg_gate_x_w, rg_gate_x_b, rg_lambda, ssd_conv_w, ssd_conv_b, ssd_dt_bias, ssd_a_log, ssd_d, ssd_norm_w,
           gdn_conv_w, gdn_conv_b, gdn_dt_bias, gdn_a_log, gdn_norm_w, s5_lambda_re, s5_lambda_im, s5_log_dt,
           s5_b_re, s5_b_im, s5_c_re, s5_c_im, s5_d, s5_glu_w, s5_glu_b):
    params = _prep_params(w_in, w_out, rg_gate_a_w, rg_gate_x_w, s5_lambda_re, s5_lambda_im, s5_log_dt, s5_b_re,
                          s5_b_im, s5_c_re, s5_c_im, s5_glu_w,
                          ln_g=ln_g, ln_b=ln_b, rg_conv_w=rg_conv_w, ssd_conv_w=ssd_conv_w, gdn_conv_w=gdn_conv_w,
                          rg_conv_b=rg_conv_b, ssd_conv_b=ssd_conv_b, gdn_conv_b=gdn_conv_b, rg_gate_a_b=rg_gate_a_b,
                          rg_gate_x_b=rg_gate_x_b, rg_lambda=rg_lambda, ssd_dt_bias=ssd_dt_bias, ssd_a_log=ssd_a_log,
                          ssd_d=ssd_d, ssd_norm_w=ssd_norm_w, gdn_dt_bias=gdn_dt_bias, gdn_a_log=gdn_a_log,
                          gdn_norm_w=gdn_norm_w, s5_d=s5_d, s5_glu_b=s5_glu_b)
    sample_states = _states_to_kernel(cache_rglru_conv, state_rglru, cache_ssd_conv, state_ssd, cache_gdn_conv,
                                      state_gdn, state_s5_re, state_s5_im)
    pb, pl_len = x_prompt.shape[0], x_prompt.shape[1]
    sl = x_sample.shape[1]
    lc_p = ROWS // pb
    assert pl_len % lc_p == 0
    yp, ys = x_prompt, x_sample
    p_new, s_new = [], []
    for l in range(DEPTH):
        yp, st_p = _layer_call(l, yp, None, params, nseq=pb, lc=lc_p)
        ys, st_s = _layer_call(l, ys, sample_states, params, nseq=SAMPLE_SEQS, lc=sl)
        p_new.append(st_p)
        s_new.append(st_s)
    return (yp, ys, *_states_from_kernel(p_new), *_states_from_kernel(s_new))
```

```python
import functools
import math

import numpy as np
import jax
import jax.numpy as jnp
from jax import lax
from jax.experimental import pallas as pl
from jax.experimental.pallas import tpu as pltpu

F32 = jnp.float32
BF16 = jnp.bfloat16

D_MODEL = 1024
DEPTH = 2
BW = 256
HD = 64
NH = BW // HD
SSD_GROUPS = 2
CONV_W = 4
HIST = 8
S5_GROUPS = 16
S5_GROUP = 16
S5_STATE = 64
S5N = S5_GROUPS * S5_STATE
LANES = 128
SUBLANES = 8
RG_C = 8.0
ALPHA = (2.0 * DEPTH) ** 0.25
CHUNK = 64
ROWS = 512
SAMPLE_SEQS = 32
VMEM_LIMIT_BYTES = 60 * 1024 * 1024

NCONV = 6 * BW
C_AX, C_BX, C_BBC, C_Q, C_K, C_V = (k * BW for k in range(6))
NREST = 6 * BW
R_GATE, R_DU, R_HEADS = 0, 4 * BW, 5 * BW
H_DT, H_BETA, H_DECAY = 0, NH, 2 * NH


def _dot(a, b):
    return jnp.dot(a, b, preferred_element_type=F32)


def _dot_nt(a, b):
    return lax.dot_general(a, b, (((1,), (1,)), ((), ())), preferred_element_type=F32)


def _sigmoid(x):
    return 0.5 * jnp.tanh(0.5 * x) + 0.5


def _silu(x):
    hx = 0.5 * x
    return hx * jnp.tanh(hx) + hx


def _softplus(x):
    return jnp.maximum(x, 0.0) + jnp.log1p(jnp.exp(-jnp.abs(x)))


def _split_bf16(x, pieces):
    out = []
    r = x
    for k in range(pieces):
        p = r.astype(BF16)
        out.append(p)
        if k + 1 < pieces:
            r = r - p.astype(F32)
    return out


EXACT_PIECES = 3


def _exact_left(mask_wide, x):
    return _dot(mask_wide, jnp.concatenate(_split_bf16(x, EXACT_PIECES), axis=0))


def _iota(shape, dim):
    return lax.broadcasted_iota(jnp.int32, shape, dim)


def _log2(n):
    k = int(round(math.log2(n)))
    assert 1 << k == n
    return k


def _bd(x, mask):
    return jnp.where(mask, jnp.tile(x, (NH, 1)), 0.0).astype(BF16)


def _diag_blocks(full, mask):
    fm = jnp.where(mask, full, 0.0)
    return fm[0:HD] + fm[HD:2 * HD] + fm[2 * HD:3 * HD] + fm[3 * HD:4 * HD]


def _scan_pitch(lc):
    return lc if (lc // SUBLANES) % 2 == 1 else lc + SUBLANES


_PARAM_NAMES = ('w_in', 'w_out', 'wg', 's5v', 'bre', 'bim', 'cre', 'cim', 'glu_w',
                'ln_g', 'ln_b', 'rg_conv_w', 'ssd_conv_w', 'gdn_conv_w', 'rg_conv_b', 'ssd_conv_b', 'gdn_conv_b',
                'rg_gate_a_b', 'rg_gate_x_b', 'rg_lambda', 'ssd_dt_bias', 'ssd_a_log', 'ssd_d', 'ssd_norm_w',
                'gdn_dt_bias', 'gdn_a_log', 'gdn_norm_w', 's5_d', 's5_glu_b')
_LAYER_BLOCKS = frozenset(_PARAM_NAMES[:9])
_N_STATES = 8
S5V_LRE, S5V_LIM, S5V_LOGDT = range(3)

_IN_COLS = {}
_c0 = 0
for _name, _w in (('a_x', BW), ('a_gate', BW), ('b_xbc', BW + 2 * SSD_GROUPS * HD), ('b_dt', NH), ('b_gate', BW),
                  ('c_qkv', 3 * BW), ('c_beta', NH), ('c_decay', NH), ('c_gate', BW), ('d_u', BW), ('d_gate', BW)):
    _IN_COLS[_name] = (_c0, _w)
    _c0 += _w
IN_COLS = _c0
_CONV_FROM = (('a_x', C_AX), ('b_xbc', C_BX), ('c_qkv', C_Q))
_REST_FROM = (('a_gate', R_GATE), ('b_gate', R_GATE + BW), ('c_gate', R_GATE + 2 * BW), ('d_gate', R_GATE + 3 * BW),
              ('d_u', R_DU), ('b_dt', R_HEADS + H_DT), ('c_beta', R_HEADS + H_BETA), ('c_decay', R_HEADS + H_DECAY))


def _layer_kernel(l, nseq, lc, zero_init, *refs):
    n_in = 0 if zero_init else _N_STATES
    x_ref = refs[0]
    state_in = refs[1:1 + n_in]
    prm = dict(zip(_PARAM_NAMES, refs[1 + n_in:1 + n_in + len(_PARAM_NAMES)]))
    rest_refs = refs[1 + n_in + len(_PARAM_NAMES):]
    lm_ref, y_ref = rest_refs[0], rest_refs[1]
    state_out = rest_refs[2:2 + _N_STATES]
    ca_out, cb_out, cc_out, rg_out, ssd_out, gdn_out, s5r_out, s5i_out = state_out
    (zext, zr, mix, wb_s, wc_s, tab, rgbuf, s5buf, dubuf, ybuf, w_conv, w_rest,
     w_in_buf, w_in_sem) = rest_refs[2 + _N_STATES:]
    w_in, w_out, wg, glu_w = (prm[k] for k in ('w_in', 'w_out', 'wg', 'glu_w'))
    s5v, bre, bim, cre, cim = (prm[k] for k in ('s5v', 'bre', 'bim', 'cre', 'cim'))

    rows = nseq * lc
    pitch = _scan_pitch(lc)
    ngroups = nseq // SUBLANES
    unit = min(lc, CHUNK)
    units = CHUNK // unit
    nchunks = rows // CHUNK
    first_call_step = (pl.program_id(0) == 0) & (pl.program_id(1) == 0)

    def vec(name):
        return prm[name][l:l + 1, :]

    def head_vec(name):
        r = vec(name)
        return jnp.concatenate([jnp.broadcast_to(r[:, h:h + 1], (1, HD)) for h in range(NH)], axis=1)

    @pl.when(first_call_step)
    def _():
        w_in_copy = pltpu.make_async_copy(w_in.at[l], w_in_buf, w_in_sem)
        w_in_copy.start()
        lr = s5v[S5V_LRE:S5V_LRE + 1, :]
        li = s5v[S5V_LIM:S5V_LIM + 1, :]
        dt = jnp.exp(s5v[S5V_LOGDT:S5V_LOGDT + 1, :])
        mag = jnp.exp(lr * dt)
        ang = li * dt
        ar = mag * jnp.cos(ang)
        ai = mag * jnp.sin(ang)
        den = lr * lr + li * li
        fr = ((ar - 1.0) * lr + ai * li) / den
        fi = (ai * lr - (ar - 1.0) * li) / den
        def group_diag(blocks, rows_per_group, lanes_per_group):
            t = jnp.concatenate([blocks] * S5_GROUPS, axis=1)
            same = ((_iota(t.shape, 0) >> _log2(rows_per_group)) == (_iota(t.shape, 1) >> _log2(lanes_per_group)))
            return jnp.where(same, t, 0.0)

        b_re, b_im = (group_diag(r[...], S5_GROUP, S5_STATE) for r in (bre, bim))
        wb_s[:, 0:S5N] = (fr * b_re - fi * b_im).astype(BF16)
        wb_s[:, S5N:2 * S5N] = (fr * b_im + fi * b_re).astype(BF16)
        wc_s[0:S5N, :] = group_diag(cre[...], S5_STATE, S5_GROUP).astype(BF16)
        wc_s[S5N:2 * S5N, :] = (-group_diag(cim[...], S5_STATE, S5_GROUP)).astype(BF16)
        tab[0:1, :] = ar
        tab[1:2, :] = ai
        w_in_copy.wait()
        for name, dst in _CONV_FROM:
            c0, width = _IN_COLS[name]
            w_conv[:, dst:dst + width] = w_in_buf[:, c0:c0 + width]
        used = R_HEADS + 3 * NH
        w_rest[:, used:NREST] = jnp.zeros((D_MODEL, NREST - used), BF16)
        for name, dst in _REST_FROM:
            c0, width = _IN_COLS[name]
            w_rest[:, dst:dst + width] = w_in_buf[:, c0:c0 + width]

    @pl.when(pl.program_id(1) == 0)
    def _():
        for k, out in enumerate(state_out):
            out[...] = jnp.zeros(out.shape, F32) if zero_init else state_in[k][...]

    hd_mask = (_iota((BW, BW), 0) >> _log2(HD)) == (_iota((BW, BW), 1) >> _log2(HD))
    bd = functools.partial(_bd, mask=hd_mask)

    h0 = HIST - (CONV_W - 1)

    conv_src = {C_AX: ('rg', 0), C_BX: ('ssd', 0), C_BBC: ('ssd', BW), C_Q: ('gdn', 0), C_K: ('gdn', BW),
                C_V: ('gdn', 2 * BW)}

    def conv(c0):
        branch, p0 = conv_src[c0]
        wts, bias = prm[branch + '_conv_w'], prm[branch + '_conv_b']
        ext = zext[:, :, c0:c0 + BW]
        def tap(k):
            return wts[l, k:k + 1, p0:p0 + BW]

        assert CONV_W == 4
        prev = pltpu.roll(ext, 1, 1)
        newer = tap(3) * ext[:, HIST:, :] + tap(2) * prev[:, HIST:, :]
        older = tap(1) * ext + tap(0) * prev
        acc = bias[l:l + 1, p0:p0 + BW] + newer + pltpu.roll(older, 2, 1)[:, HIST:, :]
        return acc.reshape(rows, BW)

    def rest(c0):
        return zr[:, c0:c0 + BW]

    def to_slabs(buf, first, val):
        for k in range(val.shape[1] // LANES):
            piece = val[:, k * LANES:(k + 1) * LANES]
            if pitch == lc:
                buf[first + k] = piece
            else:
                for q in range(nseq):
                    buf[first + k, q * pitch:q * pitch + lc, :] = piece[q * lc:(q + 1) * lc]

    def from_slabs(buf, first, n):
        cols = []
        for k in range(n):
            if pitch == lc:
                cols.append(buf[first + k])
            else:
                cols.append(jnp.concatenate([buf[first + k, q * pitch:q * pitch + lc, :] for q in range(nseq)], axis=0))
        return cols[0] if n == 1 else jnp.concatenate(cols, axis=1)

    def seq_rows(g, t):
        return pl.ds(g * SUBLANES * pitch + t, SUBLANES, stride=pitch)

    ones_bd = hd_mask.astype(BF16)

    def head_sum(v):
        return _dot(v.astype(BF16), ones_bd)

    lane = _iota((rows, LANES), 1)

    def head_rep(pair):
        rolled = pltpu.roll(pair, HD, 1)
        return jnp.concatenate([jnp.where(lane < HD, pair, rolled), jnp.where(lane < HD, rolled, pair)], axis=1)


    n_conv_tiles, n_rest_tiles = NCONV // BW, NREST // BW

    xb = x_ref[...].reshape(rows, D_MODEL).astype(BF16)
    rest_order = [R_GATE // BW, R_DU // BW, R_HEADS // BW, R_GATE // BW + 1, R_GATE // BW + 2, R_GATE // BW + 3]
    assert sorted(rest_order) == list(range(n_rest_tiles))

    def ahead(n=1):
        for _ in range(n):
            if rest_order:
                k = rest_order.pop(0)
                tile = _dot(xb, w_rest[:, k * BW:(k + 1) * BW])
                if k == R_DU // BW:
                    to_slabs(dubuf, 0, tile)
                else:
                    zr[:, k * BW:(k + 1) * BW] = tile

    def proj_conv_tile(k):
        zext[:, HIST:HIST + lc, k * BW:(k + 1) * BW] = _dot(xb, w_conv[:, k * BW:(k + 1) * BW]).reshape(nseq, lc, BW)

    zext[:, h0:HIST, C_AX:C_AX + BW] = ca_out[...]
    zext[:, h0:HIST, C_BX:C_BX + 2 * BW] = cb_out[...]
    zext[:, h0:HIST, C_Q:C_Q + 3 * BW] = cc_out[...]
    proj_conv_tile(0)
    proj_conv_tile(1)
    xc = conv(C_AX)
    proj_conv_tile(2)
    xs = _silu(conv(C_BX))
    proj_conv_tile(3)
    bc = _silu(conv(C_BBC))
    bm = head_rep(bc[:, 0:LANES])
    cm = head_rep(bc[:, LANES:2 * LANES])
    proj_conv_tile(4)
    qc = _silu(conv(C_Q))
    proj_conv_tile(5)
    kc = _silu(conv(C_K))
    ahead()
    vc = _silu(conv(C_V))
    ca_out[...] = zext[:, lc + h0:lc + HIST, C_AX:C_AX + BW]
    cb_out[...] = zext[:, lc + h0:lc + HIST, C_BX:C_BX + 2 * BW]
    cc_out[...] = zext[:, lc + h0:lc + HIST, C_Q:C_Q + 3 * BW]
    ahead()

    gts = _dot(xc.astype(BF16), wg[...])
    gate_r = _sigmoid(gts[:, 0:BW] + vec('rg_gate_a_b'))
    gate_i = _sigmoid(gts[:, BW:2 * BW] + vec('rg_gate_x_b'))
    log_a = (-RG_C * _softplus(-vec('rg_lambda'))) * gate_r
    a = jnp.exp(log_a)
    to_slabs(rgbuf, 0, a)
    to_slabs(rgbuf, 2, jnp.sqrt(-jnp.tanh(log_a) * (a * a + 1.0)) * (gate_i * xc))
    for g in range(ngroups):
        gs = slice(g * SUBLANES, (g + 1) * SUBLANES)
        h = [rg_out[gs, k * LANES:(k + 1) * LANES] for k in range(2)]
        for t in range(lc):
            idx = seq_rows(g, t)
            for k in range(2):
                h[k] = rgbuf[k, idx, :] * h[k] + rgbuf[2 + k, idx, :]
                rgbuf[2 + k, idx, :] = h[k]
        for k in range(2):
            rg_out[gs, k * LANES:(k + 1) * LANES] = h[k]
    ahead()
    mix[:, 0:BW] = (from_slabs(rgbuf, 2, 2) * _silu(rest(R_GATE))).astype(BF16)

    du = jnp.concatenate([jnp.concatenate([dubuf[k, seq_rows(g, t), :] for k in range(BW // LANES)], axis=1)
                          for g in range(ngroups) for t in range(lc)], axis=0)
    dub = du.astype(BF16)
    qn = qc * lax.rsqrt(head_sum(qc * qc) + 1e-6) * (HD ** -0.5)
    kn = kc * lax.rsqrt(head_sum(kc * kc) + 1e-6)
    for m in range(2 * S5N // BW):
        bu = _dot(dub, wb_s[:, m * BW:(m + 1) * BW])
        s5buf[2 * m] = bu[:, 0:LANES]
        s5buf[2 * m + 1] = bu[:, LANES:2 * LANES]
    ahead()
    hl = _iota((1, LANES), 1)

    def head_lanes(name_at):
        out = jnp.zeros((1, LANES), F32)
        for name, off in name_at:
            r = vec(name)
            for h in range(NH):
                out = jnp.where(hl == off + h, r[:, h:h + 1], out)
        return out

    def spread(tile, off):
        return jnp.concatenate([jnp.broadcast_to(tile[:, off + h:off + h + 1], (tile.shape[0], HD))
                                for h in range(NH)], axis=1)

    narrow = zr[:, R_HEADS:R_HEADS + LANES]
    sp = _softplus(narrow + head_lanes((('ssd_dt_bias', H_DT), ('gdn_dt_bias', H_DECAY))))
    rate = sp * (-jnp.exp(head_lanes((('ssd_a_log', H_DT), ('gdn_a_log', H_DECAY)))))
    dt = spread(sp, H_DT)
    da = spread(rate, H_DT)
    gdec = spread(rate, H_DECAY)
    beta = spread(_sigmoid(narrow), H_BETA)
    xdt = xs * dt
    vb = vc * beta

    nsl = S5N // LANES
    a_re = [jnp.broadcast_to(tab[0:1, k * LANES:(k + 1) * LANES], (SUBLANES, LANES)) for k in range(nsl)]
    a_im = [jnp.broadcast_to(tab[1:2, k * LANES:(k + 1) * LANES], (SUBLANES, LANES)) for k in range(nsl)]
    s5_state = {}

    def s5_step(g, t):
        gs = slice(g * SUBLANES, (g + 1) * SUBLANES)
        if t == 0:
            s5_state['r'] = [s5r_out[gs, k * LANES:(k + 1) * LANES] for k in range(nsl)]
            s5_state['i'] = [s5i_out[gs, k * LANES:(k + 1) * LANES] for k in range(nsl)]
        hr, hi = s5_state['r'], s5_state['i']
        r0 = (g * lc + t) * SUBLANES
        idx = slice(r0, r0 + SUBLANES)
        for k in range(nsl):
            nr = a_re[k] * hr[k] - a_im[k] * hi[k] + s5buf[k, idx, :]
            ni = a_re[k] * hi[k] + a_im[k] * hr[k] + s5buf[nsl + k, idx, :]
            hr[k], hi[k] = nr, ni
            s5buf[k, idx, :] = nr
            s5buf[nsl + k, idx, :] = ni
        if t == lc - 1:
            for k in range(nsl):
                s5r_out[gs, k * LANES:(k + 1) * LANES] = hr[k]
                s5i_out[gs, k * LANES:(k + 1) * LANES] = hi[k]

    s5_steps = [(g, t) for g in range(ngroups) for t in range(lc)]
    n_parts = 16
    s5_done = [0]

    def s5_part():
        per = -(-len(s5_steps) // n_parts)
        for g, t in s5_steps[s5_done[0]:s5_done[0] + per]:
            s5_step(g, t)
        s5_done[0] += per

    t_c = _iota((CHUNK, BW), 0)
    s_c = _iota((CHUNK, BW), 1) & (HD - 1)
    same_unit = (t_c >> _log2(unit)) == (s_c >> _log2(unit))
    valid_incl = same_unit & (s_c <= t_c)
    valid_strict = same_unit & (s_c < t_c)
    eye_cat = jnp.where(s_c == t_c, 1.0, 0.0)
    lm = lm_ref[...]
    if units == 1:
        lm = lm[0:CHUNK]
    chunks = range(nchunks)
    sls = [slice(c * CHUNK, (c + 1) * CHUNK) for c in chunks]

    strict_f = jnp.where(s_c < t_c, 1.0, 0.0)
    acum, atot, decay, dtot, lmat, eg = [], [], [], [], [], []
    for c in chunks:
        da_c, gd_c = da[sls[c]], gdec[sls[c]]
        cs = _exact_left(lm, jnp.concatenate([rate[sls[c]], da_c * strict_f, gd_c * strict_f], axis=1))
        acum.append(spread(cs[0:CHUNK, 0:LANES], H_DT))
        decay.append(spread(cs[0:CHUNK, 0:LANES], H_DECAY))
        lmat.append(jnp.where(valid_incl, jnp.exp(jnp.minimum(cs[0:CHUNK, LANES:LANES + BW], 0.0)), 0.0))
        eg.append(jnp.where(valid_incl, jnp.exp(jnp.minimum(cs[0:CHUNK, LANES + BW:LANES + 2 * BW], 0.0)), 0.0))
        tot = cs[CHUNK - 1:CHUNK, 0:LANES] if units == 1 else cs[CHUNK:2 * CHUNK, 0:LANES]
        atot.append(spread(tot, H_DT))
        dtot.append(spread(tot, H_DECAY))
    s5_part()
    ahead()

    def seq_of(c, u):
        return (c * CHUNK + u * unit) // lc

    def unit_rows(u):
        return slice(u * unit, (u + 1) * unit)

    def unit_total(tot_c, u):
        r0 = u * unit if units > 1 else 0
        return tot_c[r0:r0 + 1, :]

    def cat_rows(parts):
        return parts[0] if len(parts) == 1 else jnp.concatenate(parts, axis=0)

    unit_of_lane = (_iota((HD, BW), 1) & (HD - 1)) >> _log2(unit)

    def state_updates(w_t, x):
        if units == 1:
            return [_diag_blocks(_dot(w_t.astype(BF16), x.astype(BF16)), hd_mask)]
        w_l = jnp.concatenate([w_t[h * HD:(h + 1) * HD] for h in range(NH)], axis=1)
        lhs = jnp.concatenate([jnp.where(unit_of_lane == u, w_l, 0.0) for u in range(units)], axis=0)
        full = _dot(lhs.astype(BF16), bd(x))
        return [full[u * HD:(u + 1) * HD] for u in range(units)]

    def apply_state(v, state):
        if units == 1:
            return _dot(v.astype(BF16), bd(state))
        r = v.shape[0]
        by_head = jnp.concatenate([v[:, h * HD:(h + 1) * HD] for h in range(NH)], axis=0)
        full = _dot(by_head.astype(BF16), state.astype(BF16))
        lane_head = _iota((r, BW), 1) >> _log2(HD)
        out = full[0:r]
        for h in range(1, NH):
            out = jnp.where(lane_head == h, full[h * r:(h + 1) * r], out)
        return out

    cbm = [_dot_nt(cm[sls[c]].astype(BF16), bd(bm[sls[c]])) for c in chunks]
    s5_part()
    y_ssd = [_dot((cbm[c] * lmat[c]).astype(BF16), bd(xdt[sls[c]])) for c in chunks]
    s5_part()
    ahead()
    ht = [[ssd_out[seq_of(c, u)] for u in range(units)] for c in chunks]
    cme = [cm[sls[c]] * jnp.exp(acum[c]) for c in chunks]
    y_int = [cat_rows([apply_state(cme[c][unit_rows(u)], ht[c][u]) for u in range(units)]) for c in chunks]
    s5_part()
    wbt = [(bm[sls[c]] * jnp.exp(atot[c] - acum[c])).T for c in chunks]
    upd = [state_updates(wbt[c], xdt[sls[c]]) for c in chunks]
    for c in chunks:
        for u in range(units):
            ssd_out[seq_of(c, u)] = jnp.exp(unit_total(atot[c], u)) * ht[c][u] + upd[c][u]
    s5_part()
    ahead()

    qk_kk = [_dot_nt(jnp.concatenate([qn[sls[c]], kn[sls[c]]], axis=0).astype(BF16), bd(kn[sls[c]])) for c in chunks]
    s5_part()
    qkg = [qk_kk[c][0:CHUNK] * eg[c] for c in chunks]
    mm = [jnp.where(valid_strict, beta[sls[c]] * qk_kk[c][CHUNK:2 * CHUNK] * eg[c], 0.0) for c in chunks]
    rm = [eye_cat - mm[c] for c in chunks]
    pw = [_dot(mm[c].astype(BF16), bd(mm[c])) for c in chunks]
    s5_part()
    ahead()
    for _step in range(_log2(unit) - 2):
        pr2 = [_dot(jnp.concatenate([pw[c], rm[c]], axis=0).astype(BF16), bd(pw[c])) for c in chunks]
        pw = [pr2[c][0:CHUNK] for c in chunks]
        rm = [rm[c] + pr2[c][CHUNK:2 * CHUNK] for c in chunks]
        s5_part()
    rm = [(rm[c] + _dot(rm[c].astype(BF16), bd(pw[c]))).astype(BF16) for c in chunks]
    s5_part()
    ahead()
    edec = [jnp.exp(decay[c]) for c in chunks]
    value = [_dot(rm[c], bd(vb[sls[c]])) for c in chunks]
    kcum = [_dot(rm[c], bd(kn[sls[c]] * beta[sls[c]] * edec[c])) for c in chunks]
    s5_part()
    sq = [[gdn_out[seq_of(c, u)] for u in range(units)] for c in chunks]
    qdec = [qn[sls[c]] * edec[c] for c in chunks]
    kq = [[apply_state(jnp.concatenate([kcum[c][unit_rows(u)], qdec[c][unit_rows(u)]], axis=0), sq[c][u])
           for u in range(units)] for c in chunks]
    s5_part()
    ahead()
    wv = [cat_rows([value[c][unit_rows(u)] - kq[c][u][0:unit] for u in range(units)]) for c in chunks]
    o_chunks = [cat_rows([kq[c][u][unit:2 * unit] for u in range(units)]) + _dot(qkg[c].astype(BF16), bd(wv[c]))
                for c in chunks]
    kdec_t = [(kn[sls[c]] * jnp.exp(dtot[c] - decay[c])).T for c in chunks]
    upd = [state_updates(kdec_t[c], wv[c]) for c in chunks]
    for c in chunks:
        for u in range(units):
            gdn_out[seq_of(c, u)] = jnp.exp(unit_total(dtot[c], u)) * sq[c][u] + upd[c][u]
    while s5_done[0] < len(s5_steps):
        s5_part()
    ahead(n_conv_tiles + n_rest_tiles)

    def s5_states(m):
        return jnp.concatenate([s5buf[2 * m], s5buf[2 * m + 1]], axis=1).astype(BF16)

    y5 = vec('s5_d') * du
    for m in range(S5N // BW):
        y5 = y5 + _dot(s5_states(m), wc_s[m * BW:(m + 1) * BW, :])
    yb = jnp.concatenate([y_ssd[c] + y_int[c] for c in chunks], axis=0) + head_vec('ssd_d') * xs
    yb = yb * _silu(rest(R_GATE + BW))
    yb = yb * lax.rsqrt(jnp.mean(yb * yb, axis=-1, keepdims=True) + 1e-6) * vec('ssd_norm_w')
    mix[:, BW:2 * BW] = yb.astype(BF16)
    for m in range(S5N // BW, 2 * S5N // BW):
        y5 = y5 + _dot(s5_states(m), wc_s[m * BW:(m + 1) * BW, :])
    o = jnp.concatenate(o_chunks, axis=0)
    gdn_norm = jnp.concatenate([vec('gdn_norm_w')] * NH, axis=1)
    o = o * lax.rsqrt(head_sum(o * o) * (1.0 / HD) + 1e-6) * gdn_norm
    mix[:, 2 * BW:3 * BW] = (o * _silu(rest(R_GATE + 2 * BW))).astype(BF16)
    y5 = 0.5 * y5 * (1.0 + lax.erf(y5 * math.sqrt(0.5)))
    y5 = y5 * _sigmoid(_dot(y5.astype(BF16), glu_w[...]) + vec('s5_glu_b'))
    for k in range(BW // LANES):
        ybuf[k] = y5[:, k * LANES:(k + 1) * LANES]
    y5 = jnp.concatenate(
        [jnp.concatenate([ybuf[k, pl.ds(g * lc * SUBLANES + s, lc, stride=SUBLANES), :] for k in range(BW // LANES)],
                         axis=1) for g in range(ngroups) for s in range(SUBLANES)], axis=0)
    mix[:, 3 * BW:4 * BW] = (y5 * _silu(rest(R_GATE + 3 * BW))).astype(BF16)

    half = rows // 2
    seq_half = nseq // 2

    def out_proj(hh):
        xh = x_ref[hh * seq_half:(hh + 1) * seq_half].reshape(half, D_MODEL)
        return ALPHA * xh + _dot(mix[hh * half:(hh + 1) * half, :], w_out[...])

    def layer_norm(hh, res):
        mu = jnp.mean(res, axis=-1, keepdims=True)
        rc = res - mu
        var = jnp.mean(rc * rc, axis=-1, keepdims=True)
        y = rc * lax.rsqrt(var + 1e-5) * vec('ln_g') + vec('ln_b')
        y_ref[hh * seq_half:(hh + 1) * seq_half] = y.reshape(seq_half, lc, D_MODEL)

    res0 = out_proj(0)
    res1 = out_proj(1)
    layer_norm(0, res0)
    layer_norm(1, res1)


def _chunk_masks(unit):
    t = np.arange(CHUNK)
    same = (t[:, None] // unit) == (t[None, :] // unit)
    incl = same & (t[None, :] <= t[:, None])
    return jnp.asarray(np.tile(np.concatenate([incl, same], axis=0), (1, EXACT_PIECES)), BF16)


def _block_diag(blocks):
    *lead, n, r, c = blocks.shape
    eye = jnp.eye(n, dtype=blocks.dtype)
    out = eye[:, None, :, None] * blocks[..., :, :, None, :]
    return out.reshape(*lead, n * r, n * c)


def _prep_params(w_in, w_out, rg_gate_a_w, rg_gate_x_w, s5_lambda_re, s5_lambda_im, s5_log_dt, s5_b_re, s5_b_im,
                 s5_c_re, s5_c_im, s5_glu_w, **small):
    assert w_in.shape[-1] == IN_COLS
    out = dict(small)
    out.update(
        w_in=w_in.astype(BF16),
        w_out=w_out.astype(BF16),
        wg=jnp.concatenate([_block_diag(rg_gate_a_w), _block_diag(rg_gate_x_w)], axis=-1).astype(BF16),
        s5v=jnp.stack([s5_lambda_re.reshape(DEPTH, S5N), s5_lambda_im.reshape(DEPTH, S5N),
                       jnp.repeat(s5_log_dt, S5_STATE, axis=-1)], axis=1),
        bre=jnp.swapaxes(s5_b_re, -1, -2).reshape(DEPTH, S5_GROUPS * S5_GROUP, S5_STATE),
        bim=jnp.swapaxes(s5_b_im, -1, -2).reshape(DEPTH, S5_GROUPS * S5_GROUP, S5_STATE),
        cre=jnp.swapaxes(s5_c_re, -1, -2).reshape(DEPTH, S5N, S5_GROUP),
        cim=jnp.swapaxes(s5_c_im, -1, -2).reshape(DEPTH, S5N, S5_GROUP),
        glu_w=s5_glu_w.astype(BF16),
    )
    return [out[k] for k in _PARAM_NAMES]


def _layer_call(l, x, states, params, nseq, lc):
    bsz, seqlen, _ = x.shape
    rows = nseq * lc
    assert bsz % nseq == 0 and seqlen % lc == 0 and rows % CHUNK == 0 and lc % SUBLANES == 0 and nseq % SUBLANES == 0
    assert CHUNK % min(lc, CHUNK) == 0 and lc % min(lc, CHUNK) == 0
    pitch = _scan_pitch(lc)
    grid = (bsz // nseq, seqlen // lc)
    state_shapes = [(CONV_W - 1, BW), (CONV_W - 1, 2 * BW), (CONV_W - 1, 3 * BW), (BW,), (HD, BW), (HD, BW), (S5N,), (S5N,)]
    assert len(state_shapes) == _N_STATES

    def param_spec(name, a):
        if name == 'w_in':
            return pl.BlockSpec(memory_space=pl.ANY)
        if name in _LAYER_BLOCKS:
            return pl.BlockSpec((None,) + a.shape[1:], lambda i, j: (l,) + (0,) * (a.ndim - 1), pipeline_mode=pl.Buffered(1))
        return pl.BlockSpec(a.shape, lambda i, j: (0,) * a.ndim, pipeline_mode=pl.Buffered(1))

    lm = _chunk_masks(min(lc, CHUNK))
    x_spec = pl.BlockSpec((nseq, lc, D_MODEL), lambda i, j: (i, j, 0))
    state_args = [] if states is None else list(states)
    state_specs = [pl.BlockSpec((None, nseq) + s, lambda i, j, n=len(s): (l, i) + (0,) * n) for s in state_shapes]
    in_specs = ([x_spec] + (state_specs if state_args else [])
                + [param_spec(name, a) for name, a in zip(_PARAM_NAMES, params)]
                + [pl.BlockSpec(lm.shape, lambda i, j: (0, 0), pipeline_mode=pl.Buffered(1))])
    out_state_specs = [pl.BlockSpec((nseq,) + s, lambda i, j, n=len(s): (i,) + (0,) * n) for s in state_shapes]
    out_shape = [jax.ShapeDtypeStruct(x.shape, F32)] + [jax.ShapeDtypeStruct((bsz,) + s, F32) for s in state_shapes]
    scratch = [
        pltpu.VMEM((nseq, HIST + lc, NCONV), F32),
        pltpu.VMEM((rows, NREST), F32),
        pltpu.VMEM((rows, 4 * BW), BF16),
        pltpu.VMEM((S5_GROUPS * S5_GROUP, 2 * S5N), BF16),
        pltpu.VMEM((2 * S5N, S5_GROUPS * S5_GROUP), BF16),
        pltpu.VMEM((SUBLANES, S5N), F32),
        pltpu.VMEM((4, nseq * pitch, LANES), F32),
        pltpu.VMEM((2 * S5N // LANES, rows, LANES), F32),
        pltpu.VMEM((BW // LANES, nseq * pitch, LANES), F32),
        pltpu.VMEM((BW // LANES, rows, LANES), F32),
        pltpu.VMEM((D_MODEL, NCONV), BF16),
        pltpu.VMEM((D_MODEL, NREST), BF16),
        pltpu.VMEM((D_MODEL, IN_COLS), BF16),
        pltpu.SemaphoreType.DMA(()),
    ]
    outs = pl.pallas_call(
        functools.partial(_layer_kernel, l, nseq, lc, not state_args),
        grid=grid,
        in_specs=in_specs,
        out_specs=[x_spec] + out_state_specs,
        out_shape=out_shape,
        scratch_shapes=scratch,
        compiler_params=pltpu.CompilerParams(dimension_semantics=("arbitrary", "arbitrary"),
                                             vmem_limit_bytes=VMEM_LIMIT_BYTES),
    )(x, *state_args, *params, lm)
    return outs[0], outs[1:]


def _states_to_kernel(conv_a, h_a, conv_b, h_b, conv_c, s_c, s5_re, s5_im):
    d, bsz = h_a.shape[0], h_a.shape[1]
    ssd = jnp.transpose(h_b, (0, 1, 4, 2, 3)).reshape(d, bsz, HD, BW)
    gdn = jnp.transpose(s_c, (0, 1, 3, 2, 4)).reshape(d, bsz, HD, BW)
    return [conv_a, conv_b, conv_c, h_a, ssd, gdn, s5_re.reshape(d, bsz, S5N), s5_im.reshape(d, bsz, S5N)]


def _states_from_kernel(per_layer):
    conv_a, conv_b, conv_c, rg, ssd, gdn, s5r, s5i = (jnp.stack(t) for t in zip(*per_layer))
    d, bsz = rg.shape[0], rg.shape[1]
    return (conv_a, rg, conv_b,
            jnp.transpose(ssd.reshape(d, bsz, HD, NH, HD), (0, 1, 3, 4, 2)),
            conv_c,
            jnp.transpose(gdn.reshape(d, bsz, HD, NH, HD), (0, 1, 3, 2, 4)),
            s5r.reshape(d, bsz, S5_GROUPS, S5_STATE), s5i.reshape(d, bsz, S5_GROUPS, S5_STATE))


def kernel(x_prompt, x_sample, cache_rglru_conv, state_rglru, cache_ssd_conv, state_ssd, cache_gdn_conv, state_gdn,
           state_s5_re, state_s5_im, w_in, w_out, ln_g, ln_b, rg_conv_w, rg_conv_b, rg_gate_a_w, rg_gate_a_b,
           rg_gate_x_w, rg_gate_x_b, rg_lambda, ssd_conv_w, ssd_conv_b, ssd_dt_bias, ssd_a_log, ssd_d, ssd_norm_w,
           gdn_conv_w, gdn_conv_b, gdn_dt_bias, gdn_a_log, gdn_norm_w, s5_lambda_re, s5_lambda_im, s5_log_dt,
           s5_b_re, s5_b_im, s5_c_re, s5_c_im, s5_d, s5_glu_w, s5_glu_b):
    params = _prep_params(w_in, w_out, rg_gate_a_w, rg_gate_x_w, s5_lambda_re, s5_lambda_im, s5_log_dt, s5_b_re,
                          s5_b_im, s5_c_re, s5_c_im, s5_glu_w,
                          ln_g=ln_g, ln_b=ln_b, rg_conv_w=rg_conv_w, ssd_conv_w=ssd_conv_w, gdn_conv_w=gdn_conv_w,
                          rg_conv_b=rg_conv_b, ssd_conv_b=ssd_conv_b, gdn_conv_b=gdn_conv_b, rg_gate_a_b=rg_gate_a_b,
                          rg_gate_x_b=rg_gate_x_b, rg_lambda=rg_lambda, ssd_dt_bias=ssd_dt_bias, ssd_a_log=ssd_a_log,
                          ssd_d=ssd_d, ssd_norm_w=ssd_norm_w, gdn_dt_bias=gdn_dt_bias, gdn_a_log=gdn_a_log,
                          gdn_norm_w=gdn_norm_w, s5_d=s5_d, s5_glu_b=s5_glu_b)
    sample_states = _states_to_kernel(cache_rglru_conv, state_rglru, cache_ssd_conv, state_ssd, cache_gdn_conv,
                                      state_gdn, state_s5_re, state_s5_im)
    pb, pl_len = x_prompt.shape[0], x_prompt.shape[1]
    sl = x_sample.shape[1]
    lc_p = ROWS // pb
    assert pl_len % lc_p == 0
    yp, ys = x_prompt, x_sample
    p_new, s_new = [], []
    for l in range(DEPTH):
        yp, st_p = _layer_call(l, yp, None, params, nseq=pb, lc=lc_p)
        ys, st_s = _layer_call(l, ys, sample_states, params, nseq=SAMPLE_SEQS, lc=sl)
        p_new.append(st_p)
        s_new.append(st_s)
    return (yp, ys, *_states_from_kernel(p_new), *_states_from_kernel(s_new))
```
